```python
import math
import jax, jax.numpy as jnp
from jax import lax
import numpy as np

D_MODEL = 1024
BATCH = 16
SEQ = 256
DEPTH = 2
DEC_BATCH = 4
DEC_SEQ = 4096
PAST_LEN = 256

GRID_W = 64
N_EVEN = (DEPTH + 1) // 2
N_ODD = DEPTH // 2

MLA_HEADS = 8
MLA_Q_LORA = 256
MLA_KV_LORA = 128
MLA_NOPE = 64
MLA_ROPE = 32
MLA_V = 64
NA_HEADS = 8
NA_HD = 64
NA_KR = 8
NA_KW = 16
NA_QCB = 16
NA_KCB = 32
ML_HEADS = 4
ML_DQK = 64
ML_DV = 128
ML_CHUNK = 64
DF_HEADS = 4
DF_QK = 64
DF_V = 128
N_EXPERTS = 32
TOP_K = 4
D_FF = 1024
SWIGLU_LIMIT = 7.0
SWIGLU_ALPHA = 1.702
MOE_BLOCK = 128

ROPE_THETA = 10000.0
Q_BLOCK = 128
EPS = 1e-6

IN_A = MLA_Q_LORA + MLA_KV_LORA + MLA_ROPE + 3 * NA_HEADS * NA_HD
MIX_A = MLA_HEADS * MLA_V + NA_HEADS * NA_HD
ML_QK_W = ML_HEADS * ML_DQK
ML_V_W = ML_HEADS * ML_DV
ML_GATE_W = 4 * ML_HEADS
DF_QK_W = DF_HEADS * 2 * DF_QK
DF_V_W = DF_HEADS * DF_V
IN_B = 2 * ML_QK_W + 2 * ML_V_W + ML_GATE_W + 2 * DF_QK_W + DF_V_W
MIX_B = ML_V_W + DF_V_W

kernel_name = 'hybrid_mla_na_mlstm_diff_moe_diffusion_step'

F32 = jnp.float32


def rms_norm(x, g):
    xf = x.astype(F32)
    y = xf * lax.rsqrt(jnp.mean(xf * xf, axis=-1, keepdims=True) + EPS)
    return (y * g.astype(F32)).astype(x.dtype)


def axial_rope(n_tokens, rot_dim):
    t = jnp.arange(n_tokens)
    row = (t // GRID_W).astype(F32)
    col = (t % GRID_W).astype(F32)
    nf = rot_dim // 4
    inv = ROPE_THETA ** (-jnp.arange(nf, dtype=F32) / nf)
    ang = jnp.concatenate([row[:, None] * inv, col[:, None] * inv], axis=-1)
    return jnp.cos(ang), jnp.sin(ang)


def apply_rope(x, cos, sin):
    x2 = x.reshape(*x.shape[:-1], -1, 2)
    a, b = x2[..., 0], x2[..., 1]
    return jnp.stack([a * cos - b * sin, a * sin + b * cos], axis=-1).reshape(x.shape).astype(x.dtype)


def _query_blocks(x):
    B, S = x.shape[:2]
    return jnp.moveaxis(x.reshape(B, S // Q_BLOCK, Q_BLOCK, *x.shape[2:]), 1, 0)


def _merge_blocks(y):
    y = jnp.moveaxis(y, 0, 1)
    return y.reshape(y.shape[0], -1, *y.shape[3:])


def dense_attend(q, k, v):
    scale = q.shape[-1] ** -0.5
    def blk(qb):
        s = jnp.einsum('bqhd,bkhd->bhqk', qb, k).astype(F32) * scale
        p = jax.nn.softmax(s, axis=-1).astype(v.dtype)
        return jnp.einsum('bhqk,bkhd->bqhd', p, v)
    return _merge_blocks(lax.map(blk, _query_blocks(q)))


def mla_attend(q_nope, q_rope, k_nope, k_rope, v):
    scale = (MLA_NOPE + MLA_ROPE) ** -0.5
    def blk(qs):
        qn, qr = qs
        s = (jnp.einsum('bqhd,bkhd->bhqk', qn, k_nope)
             + jnp.einsum('bqhr,bkr->bhqk', qr, k_rope)).astype(F32) * scale
        p = jax.nn.softmax(s, axis=-1).astype(v.dtype)
        return jnp.einsum('bhqk,bkhd->bqhd', p, v)
    return _merge_blocks(lax.map(blk, (_query_blocks(q_nope), _query_blocks(q_rope))))


def diff_lambda(lam_p, lam_init):
    lp = lam_p.astype(F32)
    return jnp.exp(jnp.sum(lp[0] * lp[1])) - jnp.exp(jnp.sum(lp[2] * lp[3])) + lam_init


def diff_attend(q, k, v, lam, subln_g, lam_init):
    scale = DF_QK ** -0.5
    def blk(qb):
        s = jnp.einsum('bqhcd,bkhcd->bhcqk', qb, k).astype(F32) * scale
        p = jax.nn.softmax(s, axis=-1)
        a = (p[:, :, 0] - lam * p[:, :, 1]).astype(v.dtype)
        return jnp.einsum('bhqk,bkhd->bqhd', a, v)
    o = _merge_blocks(lax.map(blk, _query_blocks(q)))
    return rms_norm(o, subln_g) * (1.0 - lam_init)


def na_attend_latent(q, k, v, k_ctx, v_ctx, rpb):
    B, S, H, d = q.shape
    rows = S // GRID_W
    kr = min(NA_KR, rows)
    nj = GRID_W // NA_QCB
    qcol = np.arange(GRID_W).reshape(nj, NA_QCB)
    kcol = np.clip(np.arange(nj) * NA_QCB - NA_KW // 2, 0, GRID_W - NA_KCB)[:, None] + np.arange(NA_KCB)
    wstart = np.clip(qcol - NA_KW // 2, 0, GRID_W - NA_KW)
    colmask = (kcol[:, None, :] >= wstart[..., None]) & (kcol[:, None, :] < wstart[..., None] + NA_KW)
    dcol_idx = np.clip(kcol[:, None, :] - qcol[..., None] + NA_KW - 1, 0, 2 * NA_KW - 2)
    colmask = colmask[:, :, None, :]
    kg = k.reshape(B, rows, GRID_W, H, d)
    vg = v.reshape(B, rows, GRID_W, H, d)
    qg = q.reshape(B, rows, nj, NA_QCB, H, d)
    scale = d ** -0.5
    n_loc = kr * NA_KCB
    def row_blk(args):
        r, qr = args
        r0 = jnp.clip(r - kr // 2, 0, rows - kr)
        kw = lax.dynamic_slice_in_dim(kg, r0, kr, axis=1)[:, :, kcol]
        vw = lax.dynamic_slice_in_dim(vg, r0, kr, axis=1)[:, :, kcol]
        drow = r0 + jnp.arange(kr) - r + (NA_KR - 1)
        bias = rpb[:, drow[None, None, :, None], dcol_idx[:, :, None, :]].astype(F32)
        s_loc = jnp.einsum('bjqhd,bijchd->bhjqic', qr, kw).astype(F32) * scale + bias[None]
        s_loc = jnp.where(colmask, s_loc, -jnp.inf)
        s_ctx = jnp.einsum('bjqhd,blhd->bhjql', qr, k_ctx).astype(F32) * scale
        s = jnp.concatenate([s_loc.reshape(B, H, nj, NA_QCB, n_loc), s_ctx], axis=-1)
        p = jax.nn.softmax(s, axis=-1).astype(v.dtype)
        p_loc = p[..., :n_loc].reshape(B, H, nj, NA_QCB, kr, NA_KCB)
        return (jnp.einsum('bhjqic,bijchd->bjqhd', p_loc, vw)
                + jnp.einsum('bhjql,blhd->bjqhd', p[..., n_loc:], v_ctx))
    o = lax.map(row_blk, (jnp.arange(rows), jnp.moveaxis(qg, 1, 0)))
    return jnp.moveaxis(o, 0, 1).reshape(B, S, H, d)


def mlstm_chunk_scan(q, k, v, log_i, log_f, C0, n0, m0):
    B, T, H, _ = q.shape
    nc = T // ML_CHUNK
    def chunks(a):
        return jnp.moveaxis(a.astype(F32).reshape(B, nc, ML_CHUNK, *a.shape[2:]), 1, 0)
    causal = jnp.tril(jnp.ones((ML_CHUNK, ML_CHUNK), dtype=bool))[None, :, :, None]
    def step(carry, xs):
        C, n, m = carry
        qc, kc, vc, li, lf = xs
        b = jnp.cumsum(lf, axis=1)
        Dm = jnp.where(causal, b[:, :, None, :] - b[:, None, :, :] + li[:, None, :, :], -jnp.inf)
        m_inter = b + m[:, None, :]
        m_t = jnp.maximum(m_inter, jnp.max(Dm, axis=2))
        w_intra = jnp.exp(Dm - m_t[:, :, None, :]) * jnp.einsum('bthd,bshd->btsh', qc, kc)
        w_inter = jnp.exp(m_inter - m_t)
        num = (jnp.einsum('btsh,bshv->bthv', w_intra, vc)
               + w_inter[..., None] * jnp.einsum('bhvd,bthd->bthv', C, qc))
        den = jnp.sum(w_intra, axis=2) + w_inter * jnp.einsum('bhd,bthd->bth', n, qc)
        h = num / jnp.maximum(jnp.abs(den), jnp.exp(-m_t))[..., None]
        m_new = m_t[:, -1]
        w_end = jnp.exp(b[:, -1:] - b + li - m_new[:, None])
        decay = jnp.exp(b[:, -1] + m - m_new)
        C_new = decay[..., None, None] * C + jnp.einsum('bsh,bshv,bshd->bhvd', w_end, vc, kc)
        n_new = decay[..., None] * n + jnp.einsum('bsh,bshd->bhd', w_end, kc)
        return (C_new, n_new, m_new), h
    xs = (chunks(q), chunks(k), chunks(v), chunks(log_i), chunks(log_f))
    (C, n, m), h = lax.scan(step, (C0.astype(F32), n0.astype(F32), m0.astype(F32)), xs)
    h = jnp.moveaxis(h, 0, 1).reshape(B, T, H, -1)
    return (C, n, m), h


def mlstm_bidir(q, k, v, log_i, log_f, C0, n0, m0):
    def flip(a):
        return jnp.flip(a, axis=1)
    (Cf, nf, mf), hf = mlstm_chunk_scan(q, k, v, log_i[:, :, 0], log_f[:, :, 0], C0[:, 0], n0[:, 0], m0[:, 0])
    (Cb, nb, mb), hb = mlstm_chunk_scan(flip(q), flip(k), flip(v), flip(log_i[:, :, 1]), flip(log_f[:, :, 1]),
                                        C0[:, 1], n0[:, 1], m0[:, 1])
    h = (hf + flip(hb)).astype(v.dtype)
    return h, jnp.stack([Cf, Cb], axis=1), jnp.stack([nf, nb], axis=1), jnp.stack([mf, mb], axis=1)


def moe_ffn(x, w_router, b_router, w1, b1, w2, b2):
    T, D = x.shape
    logits = (x @ w_router + b_router).astype(F32)
    top_val, top_idx = lax.top_k(logits, TOP_K)
    gates = jax.nn.softmax(top_val, axis=-1).astype(x.dtype)
    A = T * TOP_K
    flat_e = top_idx.reshape(A)
    order = jnp.argsort(flat_e)
    sorted_e = flat_e[order]
    counts = jnp.bincount(flat_e, length=N_EXPERTS)
    padded = (counts + MOE_BLOCK - 1) // MOE_BLOCK * MOE_BLOCK
    pad_end = jnp.cumsum(padded)
    dest = (pad_end - padded)[sorted_e] + jnp.arange(A) - (jnp.cumsum(counts) - counts)[sorted_e]
    nb = -(-A // MOE_BLOCK) + N_EXPERTS
    n_rows = nb * MOE_BLOCK
    row_tok = jnp.zeros((n_rows,), jnp.int32).at[dest].set(order // TOP_K)
    row_gate = jnp.zeros((n_rows,), x.dtype).at[dest].set(gates.reshape(A)[order])
    blk_e = jnp.minimum(jnp.searchsorted(pad_end, jnp.arange(nb) * MOE_BLOCK, side='right'), N_EXPERTS - 1)
    def expert_block(args):
        xb, e = args
        hg = xb @ w1[e] + b1[e]
        glu = jnp.minimum(hg[:, :D_FF], SWIGLU_LIMIT)
        lin = jnp.clip(hg[:, D_FF:], -SWIGLU_LIMIT, SWIGLU_LIMIT)
        return (glu * jax.nn.sigmoid(SWIGLU_ALPHA * glu) * (lin + 1.0)) @ w2[e] + b2[e]
    yb = lax.map(expert_block, (x[row_tok].reshape(nb, MOE_BLOCK, D), blk_e))
    return jnp.zeros_like(x).at[row_tok].add(yb.reshape(n_rows, D) * row_gate[:, None])


def modulation(cond, w_mod, b_mod):
    m = jax.nn.silu(cond) @ w_mod + b_mod
    return jnp.split(m[:, None, :], 6, axis=-1)


def even_project(h, w_in, q_norm, w_uq, kv_norm):
    B, T, _ = h.shape
    p = h @ w_in
    q_lat, ckv, k_rope, na_qkv = jnp.split(
        p, [MLA_Q_LORA, MLA_Q_LORA + MLA_KV_LORA, MLA_Q_LORA + MLA_KV_LORA + MLA_ROPE], axis=-1)
    q = (rms_norm(q_lat, q_norm) @ w_uq).reshape(B, T, MLA_HEADS, MLA_NOPE + MLA_ROPE)
    ckv = rms_norm(ckv, kv_norm)
    na_qkv = na_qkv.reshape(B, T, 3, NA_HEADS, NA_HD)
    return (q[..., :MLA_NOPE], q[..., MLA_NOPE:], ckv, k_rope,
            na_qkv[:, :, 0], na_qkv[:, :, 1], na_qkv[:, :, 2])


def mla_expand(ckv, w_ukv):
    B, T, _ = ckv.shape
    kv = (ckv @ w_ukv).reshape(B, T, MLA_HEADS, MLA_NOPE + MLA_V)
    return kv[..., :MLA_NOPE], kv[..., MLA_NOPE:]


def even_output(o_mla, o_na, w_out):
    B, T = o_mla.shape[:2]
    return jnp.concatenate([o_mla.reshape(B, T, -1), o_na.reshape(B, T, -1)], axis=-1) @ w_out


def odd_project(h, w_in, b_gates):
    B, T, _ = h.shape
    p = h @ w_in
    cuts = np.cumsum([ML_QK_W, ML_QK_W, ML_V_W, ML_V_W, ML_GATE_W, DF_QK_W, DF_QK_W]).tolist()
    mq, mk, mv, mo, gt, dq, dk, dv = jnp.split(p, cuts, axis=-1)
    mq = mq.reshape(B, T, ML_HEADS, ML_DQK)
    mk = mk.reshape(B, T, ML_HEADS, ML_DQK) * (ML_DQK ** -0.5)
    mv = mv.reshape(B, T, ML_HEADS, ML_DV)
    gt = (gt.reshape(B, T, 2, 2, ML_HEADS) + b_gates).astype(F32)
    log_i = gt[:, :, :, 0]
    log_f = jax.nn.log_sigmoid(gt[:, :, :, 1])
    dq = dq.reshape(B, T, DF_HEADS, 2, DF_QK)
    dk = dk.reshape(B, T, DF_HEADS, 2, DF_QK)
    dv = dv.reshape(B, T, DF_HEADS, DF_V)
    return mq, mk, mv, mo, log_i, log_f, dq, dk, dv


def odd_output(h_ml, mo, ml_norm, o_df, w_out):
    B, T = mo.shape[:2]
    o_ml = rms_norm(h_ml, ml_norm).reshape(B, T, ML_V_W) * jax.nn.sigmoid(mo)
    return jnp.concatenate([o_ml, o_df.reshape(B, T, DF_V_W)], axis=-1) @ w_out


def setup_inputs(seed: int = 0) -> dict:
    key = jax.random.key(seed)
    keys = iter(jax.random.split(key, 64))
    def nrm(shape, scale=1.0):
        return scale * jax.random.normal(next(keys), shape, jnp.float32)
    D = D_MODEL
    b_i = nrm((N_ODD, 2, ML_HEADS), 0.1)
    b_f = 3.0 + nrm((N_ODD, 2, ML_HEADS), 0.5)
    return {
        'x_prompt': nrm((BATCH, SEQ, D)),
        'x_sample': nrm((DEC_BATCH, DEC_SEQ, D)),
        'c': nrm((DEC_BATCH, D)),
        'cache_mla_ckv': nrm((DEC_BATCH, N_EVEN, PAST_LEN, MLA_KV_LORA)),
        'cache_mla_krope': nrm((DEC_BATCH, N_EVEN, PAST_LEN, MLA_ROPE)),
        'cache_na_k': nrm((DEC_BATCH, N_EVEN, PAST_LEN, NA_HEADS, NA_HD)),
        'cache_na_v': nrm((DEC_BATCH, N_EVEN, PAST_LEN, NA_HEADS, NA_HD)),
        'state_mlstm_C': nrm((DEC_BATCH, N_ODD, 2, ML_HEADS, ML_DV, ML_DQK), 0.1),
        'state_mlstm_n': nrm((DEC_BATCH, N_ODD, 2, ML_HEADS, ML_DQK), 0.1),
        'state_mlstm_m': nrm((DEC_BATCH, N_ODD, 2, ML_HEADS), 0.5),
        'cache_diff_k': nrm((DEC_BATCH, N_ODD, PAST_LEN, DF_HEADS, 2, DF_QK)),
        'cache_diff_v': nrm((DEC_BATCH, N_ODD, PAST_LEN, DF_HEADS, DF_V)),
        'c_ctx': nrm((D,)),
        'w_mod': nrm((DEPTH, D, 6 * D), 0.5 * D ** -0.5),
        'b_mod': nrm((DEPTH, 6 * D), 0.02),
        'norm_g': 1.0 + nrm((DEPTH, 2, D), 0.01),
        'final_g': 1.0 + nrm((D,), 0.01),
        'a_w_in': nrm((N_EVEN, D, IN_A), D ** -0.5),
        'a_q_norm': 1.0 + nrm((N_EVEN, MLA_Q_LORA), 0.01),
        'a_w_uq': nrm((N_EVEN, MLA_Q_LORA, MLA_HEADS * (MLA_NOPE + MLA_ROPE)), MLA_Q_LORA ** -0.5),
        'a_kv_norm': 1.0 + nrm((N_EVEN, MLA_KV_LORA), 0.01),
        'a_w_ukv': nrm((N_EVEN, MLA_KV_LORA, MLA_HEADS * (MLA_NOPE + MLA_V)), MLA_KV_LORA ** -0.5),
        'a_rpb': nrm((N_EVEN, NA_HEADS, 2 * NA_KR - 1, 2 * NA_KW - 1), 0.1),
        'a_w_out': nrm((N_EVEN, MIX_A, D), MIX_A ** -0.5),
        'b_w_in': nrm((N_ODD, D, IN_B), D ** -0.5),
        'b_gate_bias': jnp.stack([b_i, b_f], axis=2),
        'b_ml_norm': 1.0 + nrm((N_ODD, ML_HEADS, ML_DV), 0.01),
        'b_df_lambda': nrm((N_ODD, 4, DF_QK), 0.1),
        'b_df_subln': 1.0 + nrm((N_ODD, DF_V), 0.01),
        'b_w_out': nrm((N_ODD, MIX_B, D), MIX_B ** -0.5),
        'moe_w_router': nrm((DEPTH, D, N_EXPERTS), D ** -0.5),
        'moe_b_router': nrm((DEPTH, N_EXPERTS), 0.01),
        'moe_w1': nrm((DEPTH, N_EXPERTS, D, 2 * D_FF), D ** -0.5),
        'moe_b1': nrm((DEPTH, N_EXPERTS, 2 * D_FF), 0.01),
        'moe_w2': nrm((DEPTH, N_EXPERTS, D_FF, D), D_FF ** -0.5),
        'moe_b2': nrm((DEPTH, N_EXPERTS, D), 0.01),
    }


def reference(x_prompt, x_sample, c, cache_mla_ckv, cache_mla_krope, cache_na_k, cache_na_v,
              state_mlstm_C, state_mlstm_n, state_mlstm_m, cache_diff_k, cache_diff_v,
              c_ctx, w_mod, b_mod, norm_g, final_g,
              a_w_in, a_q_norm, a_w_uq, a_kv_norm, a_w_ukv, a_rpb, a_w_out,
              b_w_in, b_gate_bias, b_ml_norm, b_df_lambda, b_df_subln, b_w_out,
              moe_w_router, moe_b_router, moe_w1, moe_b1, moe_w2, moe_b2):
    Bp, Lp, D = x_prompt.shape
    Ts = x_sample.shape[1]
    cos_a, sin_a = axial_rope(Ts, MLA_ROPE)
    cos_d, sin_d = axial_rope(Ts, DF_QK)
    xp, xs = x_prompt, x_sample
    ctx_cond = c_ctx[None, :]
    l_ckv, l_kr, l_nak, l_nav, l_C, l_n, l_m, l_dk, l_dv = [], [], [], [], [], [], [], [], []
    for l in range(DEPTH):
        mp = modulation(ctx_cond, w_mod[l], b_mod[l])
        ms = modulation(c, w_mod[l], b_mod[l])
        hp = rms_norm(xp, norm_g[l, 0]) * (1.0 + mp[1]) + mp[0]
        hs = rms_norm(xs, norm_g[l, 0]) * (1.0 + ms[1]) + ms[0]
        if l % 2 == 0:
            e = l // 2
            qn, qr, ckv, kr, nq, nk, nv = even_project(hp, a_w_in[e], a_q_norm[e], a_w_uq[e], a_kv_norm[e])
            kn, vv = mla_expand(ckv, a_w_ukv[e])
            mix_p = even_output(mla_attend(qn, qr, kn, kr, vv), dense_attend(nq, nk, nv), a_w_out[e])
            l_ckv.append(ckv)
            l_kr.append(kr)
            l_nak.append(nk)
            l_nav.append(nv)
            qn, qr, ckv, kr, nq, nk, nv = even_project(hs, a_w_in[e], a_q_norm[e], a_w_uq[e], a_kv_norm[e])
            qr = apply_rope(qr, cos_a[None, :, None, :], sin_a[None, :, None, :])
            kr = apply_rope(kr, cos_a[None], sin_a[None])
            kn, vv = mla_expand(jnp.concatenate([cache_mla_ckv[:, e], ckv], axis=1), a_w_ukv[e])
            o_mla = mla_attend(qn, qr, kn, jnp.concatenate([cache_mla_krope[:, e], kr], axis=1), vv)
            o_na = na_attend_latent(nq, nk, nv, cache_na_k[:, e], cache_na_v[:, e], a_rpb[e])
            mix_s = even_output(o_mla, o_na, a_w_out[e])
        else:
            o = l // 2
            lam_init = 0.8 - 0.6 * math.exp(-0.3 * l)
            lam = diff_lambda(b_df_lambda[o], lam_init)
            mq, mk, mv, mo, li, lf, dq, dk, dv = odd_project(hp, b_w_in[o], b_gate_bias[o])
            C0 = jnp.zeros((Bp, 2, ML_HEADS, ML_DV, ML_DQK), F32)
            n0 = jnp.zeros((Bp, 2, ML_HEADS, ML_DQK), F32)
            m0 = jnp.zeros((Bp, 2, ML_HEADS), F32)
            h_ml, Cn, nn_, mn = mlstm_bidir(mq, mk, mv, li, lf, C0, n0, m0)
            o_df = diff_attend(dq, dk, dv, lam, b_df_subln[o], lam_init)
            mix_p = odd_output(h_ml, mo, b_ml_norm[o], o_df, b_w_out[o])
            l_C.append(Cn.astype(x_prompt.dtype))
            l_n.append(nn_.astype(x_prompt.dtype))
            l_m.append(mn.astype(x_prompt.dtype))
            l_dk.append(dk)
            l_dv.append(dv)
            mq, mk, mv, mo, li, lf, dq, dk, dv = odd_project(hs, b_w_in[o], b_gate_bias[o])
            h_ml, _, _, _ = mlstm_bidir(mq, mk, mv, li, lf, state_mlstm_C[:, o], state_mlstm_n[:, o],
                                        state_mlstm_m[:, o])
            dq = apply_rope(dq, cos_d[None, :, None, None, :], sin_d[None, :, None, None, :])
            dk = apply_rope(dk, cos_d[None, :, None, None, :], sin_d[None, :, None, None, :])
            o_df = diff_attend(dq, jnp.concatenate([cache_diff_k[:, o], dk], axis=1),
                               jnp.concatenate([cache_diff_v[:, o], dv], axis=1), lam, b_df_subln[o], lam_init)
            mix_s = odd_output(h_ml, mo, b_ml_norm[o], o_df, b_w_out[o])
        xp = xp + mp[2] * mix_p
        xs = xs + ms[2] * mix_s
        hp = rms_norm(xp, norm_g[l, 1]) * (1.0 + mp[4]) + mp[3]
        hs = rms_norm(xs, norm_g[l, 1]) * (1.0 + ms[4]) + ms[3]
        xp = xp + mp[5] * moe_ffn(hp.reshape(-1, D), moe_w_router[l], moe_b_router[l], moe_w1[l], moe_b1[l],
                                  moe_w2[l], moe_b2[l]).reshape(xp.shape)
        xs = xs + ms[5] * moe_ffn(hs.reshape(-1, D), moe_w_router[l], moe_b_router[l], moe_w1[l], moe_b1[l],
                                  moe_w2[l], moe_b2[l]).reshape(xs.shape)
    y_prompt = rms_norm(xp, final_g)
    y_sample = rms_norm(xs, final_g)
    new_mla_ckv = jnp.stack(l_ckv, axis=1)
    new_mla_krope = jnp.stack(l_kr, axis=1)
    new_na_k = jnp.stack(l_nak, axis=1)
    new_na_v = jnp.stack(l_nav, axis=1)
    new_mlstm_C = jnp.stack(l_C, axis=1)
    new_mlstm_n = jnp.stack(l_n, axis=1)
    new_mlstm_m = jnp.stack(l_m, axis=1)
    new_diff_k = jnp.stack(l_dk, axis=1)
    new_diff_v = jnp.stack(l_dv, axis=1)
    return (y_prompt, y_sample, new_mla_ckv, new_mla_krope, new_na_k, new_na_v,
            new_mlstm_C, new_mlstm_n, new_mlstm_m, new_diff_k, new_diff_v)
```

```python
import functools
import math

import numpy as np
import jax
import jax.numpy as jnp
from jax import lax
from jax.experimental import pallas as pl
from jax.experimental.pallas import tpu as pltpu

F32 = jnp.float32
BF16 = jnp.bfloat16
HIGHEST = lax.Precision.HIGHEST

D_MODEL = 1024
GRID_W = 64
LANE = 128
EPS = 1e-6
ROPE_THETA = 10000.0

MLA_HEADS, MLA_Q_LORA, MLA_KV_LORA, MLA_NOPE, MLA_ROPE, MLA_V = 8, 256, 128, 64, 32, 64
NA_HEADS, NA_HD, NA_KR, NA_KW = 8, 64, 8, 16
ML_HEADS, ML_DQK, ML_DV, ML_CHUNK = 4, 64, 128, 64
DF_HEADS, DF_QK, DF_V = 4, 64, 128
N_EXPERTS, TOP_K, D_FF = 32, 4, 1024
SWIGLU_LIMIT, SWIGLU_ALPHA = 7.0, 1.702

ROW_TILE = 256
MOE_ROWS = 256
ATT_TQ, ATT_TK = 512, 256
NEG_BIG = -1e30
VMEM_LIMIT = 56 * 1024 * 1024


def _cparams(*sem):
    return pltpu.CompilerParams(dimension_semantics=sem, vmem_limit_bytes=VMEM_LIMIT)


def _rms(x, g):
    return x * lax.rsqrt(jnp.mean(x * x, axis=-1, keepdims=True) + EPS) * g


def _sigmoid(x):
    return 1.0 / (1.0 + jnp.exp(-x))


def _mod_kernel(c_ref, w_ref, b_ref, o_ref):
    c = c_ref[...]
    s = c * _sigmoid(c)
    o_ref[0] = jnp.dot(s, w_ref[0], precision=HIGHEST, preferred_element_type=F32) + b_ref[0]


def _modulation(cond8, w_mod, b_mod):
    depth, d, n6 = w_mod.shape
    tn = 1024
    out = pl.pallas_call(
        _mod_kernel,
        grid=(depth, n6 // tn),
        in_specs=[
            pl.BlockSpec((8, d), lambda l, j: (0, 0)),
            pl.BlockSpec((1, d, tn), lambda l, j: (l, 0, j)),
            pl.BlockSpec((1, 1, tn), lambda l, j: (l, 0, j)),
        ],
        out_specs=pl.BlockSpec((1, 8, tn), lambda l, j: (l, 0, j)),
        out_shape=jax.ShapeDtypeStruct((depth, 8, n6), F32),
        compiler_params=_cparams("parallel", "parallel"),
        name="modulation",
    )(cond8, w_mod, b_mod.reshape(depth, 1, n6))
    return out.reshape(depth, 8, 6, d)


def _rope_block(x, cos, sin_a, sin_b, half):
    up = pltpu.roll(x, LANE - half, axis=1)
    dn = pltpu.roll(x, half, axis=1)
    return x * cos + up * sin_a + dn * sin_b


def _even_proj_kernel(*refs, rope):
    if rope:
        (x_ref, g_ref, mod_ref, w_ref, qn_ref, wuq_ref, kvn_ref, tq_ref, tk_ref,
         q_ref, ckv_ref, kr_ref, nq_ref, nk_ref, nv_ref) = refs
    else:
        (x_ref, g_ref, mod_ref, w_ref, qn_ref, wuq_ref, kvn_ref,
         q_ref, ckv_ref, kr_ref, nq_ref, nk_ref, nv_ref) = refs
    x = x_ref[...]
    h = _rms(x, g_ref[...]) * (1.0 + mod_ref[0, 1:2, :]) + mod_ref[0, 0:1, :]
    hb = h.astype(BF16)

    def proj(a, b):
        return jnp.dot(hb, w_ref[:, a:b], preferred_element_type=F32)

    q_lat = proj(0, 256)
    qn = _rms(q_lat, qn_ref[...]).astype(BF16)
    for hd in range(MLA_HEADS):
        qh = jnp.dot(qn, wuq_ref[:, hd * LANE:(hd + 1) * LANE], preferred_element_type=F32)
        if rope:
            qh = _rope_block(qh, tq_ref[0], tq_ref[1], tq_ref[2], MLA_ROPE // 2)
        q_ref[:, hd * LANE:(hd + 1) * LANE] = qh.astype(q_ref.dtype)
    ckv_ref[...] = _rms(proj(256, 384), kvn_ref[...])
    kr = proj(384, 512)
    if rope:
        kr = _rope_block(kr, tk_ref[0], tk_ref[1], tk_ref[2], MLA_ROPE // 2)
    kr_ref[...] = kr
    nq_ref[...] = proj(512, 1536).astype(nq_ref.dtype)
    nk_ref[...] = proj(1536, 2560).astype(nk_ref.dtype)
    nv_ref[...] = proj(2560, 3584).astype(nv_ref.dtype)


def _even_proj(x, g, mod, w, qn, wuq, kvn, group_of_tile, rope_tabs, kv_dtype):
    n, d = x.shape
    tm = ROW_TILE
    rope = rope_tabs is not None
    in_specs = [
        pl.BlockSpec((tm, d), lambda i: (i, 0)),
        pl.BlockSpec((1, d), lambda i: (0, 0)),
        pl.BlockSpec((1, 6, d), lambda i: (group_of_tile(i), 0, 0)),
        pl.BlockSpec(w.shape, lambda i: (0, 0)),
        pl.BlockSpec(qn.shape, lambda i: (0, 0)),
        pl.BlockSpec(wuq.shape, lambda i: (0, 0)),
        pl.BlockSpec(kvn.shape, lambda i: (0, 0)),
    ]
    args = [x, g, mod, w, qn, wuq, kvn]
    if rope:
        tq, tk = rope_tabs
        nblk = tq.shape[1] // tm
        in_specs += [pl.BlockSpec((3, tm, LANE), lambda i: (0, i % nblk, 0)),
                     pl.BlockSpec((3, tm, LANE), lambda i: (0, i % nblk, 0))]
        args += [tq, tk]
    wide = MLA_HEADS * LANE
    out_shape = [
        jax.ShapeDtypeStruct((n, wide), BF16),
        jax.ShapeDtypeStruct((n, LANE), F32),
        jax.ShapeDtypeStruct((n, LANE), F32),
        jax.ShapeDtypeStruct((n, wide), BF16),
        jax.ShapeDtypeStruct((n, wide), kv_dtype),
        jax.ShapeDtypeStruct((n, wide), kv_dtype),
    ]
    out_specs = [pl.BlockSpec((tm, s.shape[1]), lambda i: (i, 0)) for s in out_shape]
    return pl.pallas_call(
        functools.partial(_even_proj_kernel, rope=rope),
        grid=(n // tm,),
        in_specs=in_specs,
        out_specs=out_specs,
        out_shape=out_shape,
        compiler_params=_cparams("parallel"),
        name="even_proj_rope" if rope else "even_proj",
    )(*args)


def _kv_expand_kernel(ckv_ref, kr_ref, wk_ref, sel_ref, wv_ref, k_ref, v_ref):
    c = ckv_ref[...].astype(BF16)
    r = kr_ref[...].astype(BF16)
    k = jnp.dot(c, wk_ref[...], preferred_element_type=F32)
    k = k + jnp.dot(r, sel_ref[...], preferred_element_type=F32)
    k_ref[...] = k.astype(BF16)
    v_ref[...] = jnp.dot(c, wv_ref[...], preferred_element_type=F32).astype(BF16)


def _kv_expand(ckv, kr, wk, sel, wv):
    n = ckv.shape[0]
    tm = ROW_TILE
    wide = MLA_HEADS * LANE
    const = lambda i: (0, 0)
    return pl.pallas_call(
        _kv_expand_kernel,
        grid=(n // tm,),
        in_specs=[pl.BlockSpec((tm, LANE), lambda i: (i, 0)),
                  pl.BlockSpec((tm, LANE), lambda i: (i, 0)),
                  pl.BlockSpec(wk.shape, const), pl.BlockSpec(sel.shape, const),
                  pl.BlockSpec(wv.shape, const)],
        out_specs=[pl.BlockSpec((tm, wide), lambda i: (i, 0))] * 2,
        out_shape=[jax.ShapeDtypeStruct((n, wide), BF16)] * 2,
        compiler_params=_cparams("parallel"),
        name="kv_expand",
    )(ckv, kr, wk, sel, wv)


def _flash_kernel(q_ref, k_ref, v_ref, o_ref, m_sc, l_sc, acc_sc, *, heads, v_of_head, scale):
    kj = pl.program_id(2)

    @pl.when(kj == 0)
    def _():
        m_sc[...] = jnp.full(m_sc.shape, NEG_BIG, F32)
        l_sc[...] = jnp.zeros(l_sc.shape, F32)
        acc_sc[...] = jnp.zeros(acc_sc.shape, F32)

    for hd in range(heads):
        sl = slice(hd * LANE, (hd + 1) * LANE)
        vh = v_of_head(hd)
        s = lax.dot_general(q_ref[0, :, sl], k_ref[0, :, sl], (((1,), (1,)), ((), ())),
                            preferred_element_type=F32)
        m_prev = m_sc[hd]
        m_new = jnp.maximum(m_prev, jnp.max(s, axis=1, keepdims=True))
        alpha = jnp.exp((m_prev - m_new) * scale)
        p = jnp.exp((s - m_new) * scale)
        l_sc[hd] = alpha * l_sc[hd] + jnp.sum(p, axis=1, keepdims=True)
        pv = jnp.dot(p.astype(BF16), v_ref[0, :, vh * LANE:(vh + 1) * LANE],
                     preferred_element_type=F32)
        acc_sc[:, sl] = alpha * acc_sc[:, sl] + pv
        m_sc[hd] = m_new

    @pl.when(kj == pl.num_programs(2) - 1)
    def _():
        for hd in range(heads):
            sl = slice(hd * LANE, (hd + 1) * LANE)
            o_ref[0, :, sl] = (acc_sc[:, sl] / l_sc[hd]).astype(o_ref.dtype)


def _flash(q, k, v, *, scale, v_of_head, out_dtype):
    b, s, wq = q.shape
    nk = k.shape[1]
    wv = v.shape[2]
    heads = wq // LANE
    tq = min(ATT_TQ, s)
    tk = min(ATT_TK, nk)
    return pl.pallas_call(
        functools.partial(_flash_kernel, heads=heads, v_of_head=v_of_head, scale=scale),
        grid=(b, s // tq, nk // tk),
        in_specs=[pl.BlockSpec((1, tq, wq), lambda bi, qi, ki: (bi, qi, 0)),
                  pl.BlockSpec((1, tk, wq), lambda bi, qi, ki: (bi, ki, 0)),
                  pl.BlockSpec((1, tk, wv), lambda bi, qi, ki: (bi, ki, 0))],
        out_specs=pl.BlockSpec((1, tq, wq), lambda bi, qi, ki: (bi, qi, 0)),
        out_shape=jax.ShapeDtypeStruct((b, s, wq), out_dtype),
        scratch_shapes=[pltpu.VMEM((heads, tq, 1), F32), pltpu.VMEM((heads, tq, 1), F32),
                        pltpu.VMEM((tq, wq), F32)],
        compiler_params=_cparams("parallel", "parallel", "arbitrary"),
        name="flash_attention",
    )(q, k, v)


def _na_kernel(q_ref, k_ref, v_ref, kc_ref, vc_ref, bias_ref, o_ref, *, rows, scale):
    r = pl.program_id(1)
    r0 = jnp.clip(r - NA_KR // 2, 0, rows - NA_KR)
    start = pl.multiple_of(r0 * GRID_W, GRID_W)
    nloc = NA_KR * GRID_W
    for hd in range(NA_HEADS):
        sl = slice(hd * LANE, (hd + 1) * LANE)
        qh = q_ref[0, :, sl]
        kh = k_ref[0, pl.ds(start, nloc), sl]
        vh = v_ref[0, pl.ds(start, nloc), sl]
        dn = (((1,), (1,)), ((), ()))
        s_loc = lax.dot_general(qh, kh, dn, preferred_element_type=F32) * scale + bias_ref[hd, 0]
        s_ctx = lax.dot_general(qh, kc_ref[0, :, sl], dn, preferred_element_type=F32) * scale
        m = jnp.maximum(jnp.max(s_loc, axis=1, keepdims=True), jnp.max(s_ctx, axis=1, keepdims=True))
        p_loc = jnp.exp(s_loc - m)
        p_ctx = jnp.exp(s_ctx - m)
        l = jnp.sum(p_loc, axis=1, keepdims=True) + jnp.sum(p_ctx, axis=1, keepdims=True)
        o = (jnp.dot(p_loc.astype(BF16), vh, preferred_element_type=F32)
             + jnp.dot(p_ctx.astype(BF16), vc_ref[0, :, sl], preferred_element_type=F32))
        o_ref[0, :, sl] = (o / l).astype(o_ref.dtype)


def _na_latent(q, k, v, kc, vc, bias_tab):
    b, s, wide = q.shape
    rows = s // GRID_W
    lc = kc.shape[1]

    def bias_idx(bi, r):
        return (0, jnp.clip(r - NA_KR // 2, 0, rows - NA_KR) - r + NA_KR - 1, 0, 0)

    return pl.pallas_call(
        functools.partial(_na_kernel, rows=rows, scale=NA_HD ** -0.5),
        grid=(b, rows),
        in_specs=[pl.BlockSpec((1, GRID_W, wide), lambda bi, r: (bi, r, 0)),
                  pl.BlockSpec((1, s, wide), lambda bi, r: (bi, 0, 0)),
                  pl.BlockSpec((1, s, wide), lambda bi, r: (bi, 0, 0)),
                  pl.BlockSpec((1, lc, wide), lambda bi, r: (bi, 0, 0)),
                  pl.BlockSpec((1, lc, wide), lambda bi, r: (bi, 0, 0)),
                  pl.BlockSpec((NA_HEADS, 1, GRID_W, NA_KR * GRID_W), bias_idx)],
        out_specs=pl.BlockSpec((1, GRID_W, wide), lambda bi, r: (bi, r, 0)),
        out_shape=jax.ShapeDtypeStruct((b, s, wide), BF16),
        compiler_params=_cparams("parallel", "arbitrary"),
        name="na_latent",
    )(q, k, v, kc, vc, bias_tab)


def _odd_proj_kernel(*refs, rope):
    if rope:
        (x_ref, g_ref, mod_ref, w_ref, bg_ref, td_ref,
         mq_ref, mk_ref, mv_ref, mo_ref, gt_ref, dq_ref, dk_ref, dv_ref) = refs
    else:
        (x_ref, g_ref, mod_ref, w_ref, bg_ref,
         mq_ref, mk_ref, mv_ref, mo_ref, gt_ref, dq_ref, dk_ref, dv_ref) = refs
    x = x_ref[...]
    h = _rms(x, g_ref[...]) * (1.0 + mod_ref[0, 1:2, :]) + mod_ref[0, 0:1, :]
    hb = h.astype(BF16)

    def proj(a, b):
        return jnp.dot(hb, w_ref[:, a:b], preferred_element_type=F32)

    mq_ref[...] = proj(0, 512).astype(mq_ref.dtype)
    mk_ref[...] = (proj(512, 1024) * (ML_DQK ** -0.5)).astype(mk_ref.dtype)
    mv_ref[...] = proj(1024, 1536).astype(mv_ref.dtype)
    mo_ref[...] = proj(1536, 2048)
    for blk in range(8):
        qd = proj(2048 + blk * LANE, 2048 + (blk + 1) * LANE)
        kd = proj(3072 + blk * LANE, 3072 + (blk + 1) * LANE)
        if rope:
            qd = _rope_block(qd, td_ref[0], td_ref[1], td_ref[2], DF_QK // 2)
            kd = _rope_block(kd, td_ref[0], td_ref[1], td_ref[2], DF_QK // 2)
        dq_ref[:, blk * LANE:(blk + 1) * LANE] = qd.astype(dq_ref.dtype)
        dk_ref[:, blk * LANE:(blk + 1) * LANE] = kd.astype(dk_ref.dtype)
    dv_ref[...] = proj(4096, 4608).astype(dv_ref.dtype)
    gt = proj(4608, 4736) + bg_ref[...]
    lane = lax.broadcasted_iota(jnp.int32, gt.shape, 1)
    log_f = jnp.minimum(gt, 0.0) - jnp.log(1.0 + jnp.exp(-jnp.abs(gt)))
    gt_ref[...] = jnp.where((lane % 8) >= 4, log_f, gt)


def _odd_proj(x, g, mod, w, bg, group_of_tile, rope_tab, kv_dtype):
    n, d = x.shape
    tm = ROW_TILE
    rope = rope_tab is not None
    in_specs = [
        pl.BlockSpec((tm, d), lambda i: (i, 0)),
        pl.BlockSpec((1, d), lambda i: (0, 0)),
        pl.BlockSpec((1, 6, d), lambda i: (group_of_tile(i), 0, 0)),
        pl.BlockSpec(w.shape, lambda i: (0, 0)),
        pl.BlockSpec(bg.shape, lambda i: (0, 0)),
    ]
    args = [x, g, mod, w, bg]
    if rope:
        nblk = rope_tab.shape[1] // tm
        in_specs.append(pl.BlockSpec((3, tm, LANE), lambda i: (0, i % nblk, 0)))
        args.append(rope_tab)
    out_shape = [
        jax.ShapeDtypeStruct((n, 512), BF16),
        jax.ShapeDtypeStruct((n, 512), BF16),
        jax.ShapeDtypeStruct((n, 512), BF16),
        jax.ShapeDtypeStruct((n, 512), F32),
        jax.ShapeDtypeStruct((n, LANE), F32),
        jax.ShapeDtypeStruct((n, 1024), BF16),
        jax.ShapeDtypeStruct((n, 1024), kv_dtype),
        jax.ShapeDtypeStruct((n, 512), kv_dtype),
    ]
    out_specs = [pl.BlockSpec((tm, s.shape[1]), lambda i: (i, 0)) for s in out_shape]
    return pl.pallas_call(
        functools.partial(_odd_proj_kernel, rope=rope),
        grid=(n // tm,),
        in_specs=in_specs,
        out_specs=out_specs,
        out_shape=out_shape,
        compiler_params=_cparams("parallel"),
        name="odd_proj_rope" if rope else "odd_proj",
    )(*args)


def _mlstm_kernel(q_ref, k_ref, v_ref, g_ref, c0_ref, n0_ref, m0_ref,
                  h_ref, cf_ref, nf_ref, mf_ref, c_sc, n_sc, m_sc, *, bb, direction, reverse):
    step = pl.program_id(1)
    L = ML_CHUNK

    @pl.when(step == 0)
    def _():
        c_sc[...] = c0_ref[...]
        n_sc[...] = n0_ref[...]
        m_sc[...] = m0_ref[...]

    t_idx = lax.broadcasted_iota(jnp.int32, (L, L), 0)
    s_idx = lax.broadcasted_iota(jnp.int32, (L, L), 1)
    keep = (s_idx >= t_idx) if reverse else (s_idx <= t_idx)
    tri = keep.astype(F32)
    last = 0 if reverse else L - 1
    for bi in range(bb):
        g = g_ref[bi]
        bcum = jnp.dot(tri, g, precision=HIGHEST, preferred_element_type=F32)
        g_t = g.T
        b_t = bcum.T
        for hd in range(ML_HEADS):
            sl = slice(hd * LANE, (hd + 1) * LANE)
            ci = direction * 8 + hd
            cf = direction * 8 + 4 + hd
            li_row = g_t[ci:ci + 1, :]
            b_row = b_t[cf:cf + 1, :]
            li_col = g[:, ci:ci + 1]
            b_col = bcum[:, cf:cf + 1]
            m_old = m_sc[bi, hd][:, 0:1]
            n_old = n_sc[bi, hd]
            c_old = c_sc[bi, hd]
            qh = q_ref[bi, :, sl]
            kh = k_ref[bi, :, sl]
            vh = v_ref[bi, :, sl]
            dm = jnp.where(keep, b_col - b_row + li_row, NEG_BIG)
            m_inter = b_col + m_old
            m_t = jnp.maximum(m_inter, jnp.max(dm, axis=1, keepdims=True))
            qk = lax.dot_general(qh, kh, (((1,), (1,)), ((), ())), preferred_element_type=F32)
            w_intra = jnp.where(keep, jnp.exp(dm - m_t), 0.0) * qk
            w_inter = jnp.exp(m_inter - m_t)
            qc = lax.dot_general(qh, c_old.astype(BF16), (((1,), (1,)), ((), ())),
                                 preferred_element_type=F32)
            num = jnp.dot(w_intra.astype(BF16), vh, preferred_element_type=F32) + w_inter * qc
            qn = jnp.sum(qh.astype(F32) * n_old, axis=1, keepdims=True)
            den = jnp.sum(w_intra, axis=1, keepdims=True) + w_inter * qn
            h_ref[bi, :, sl] = num / jnp.maximum(jnp.abs(den), jnp.exp(-m_t))
            m_new = m_t[last:last + 1, :]
            b_last = b_col[last:last + 1, :]
            w_end = jnp.exp(b_last - b_col + li_col - m_new)
            decay = jnp.exp(b_last + m_old - m_new)
            wv = (w_end * vh.astype(F32)).astype(BF16)
            upd = lax.dot_general(wv, kh, (((0,), (0,)), ((), ())), preferred_element_type=F32)
            c_sc[bi, hd] = decay * c_old + upd
            n_sc[bi, hd] = decay * n_old + jnp.sum(w_end * kh.astype(F32), axis=0, keepdims=True)
            m_sc[bi, hd] = jnp.broadcast_to(m_new, (1, LANE))

    @pl.when(step == pl.num_programs(1) - 1)
    def _():
        cf_ref[...] = c_sc[...]
        nf_ref[...] = n_sc[...]
        mf_ref[...] = m_sc[...]


def _mlstm_dir(q, k, v, gates, c0, n0, m0, *, direction, bb):
    b, t, w = q.shape
    nc = t // ML_CHUNK
    reverse = direction == 1
    if reverse:
        cidx = lambda bi, c: (bi, nc - 1 - c, 0)
    else:
        cidx = lambda bi, c: (bi, c, 0)
    st4 = lambda bi, c: (bi, 0, 0, 0)
    seq = pl.BlockSpec((bb, ML_CHUNK, w), cidx)
    c_spec = pl.BlockSpec((bb, ML_HEADS, LANE, LANE), st4)
    n_spec = pl.BlockSpec((bb, ML_HEADS, 1, LANE), st4)
    return pl.pallas_call(
        functools.partial(_mlstm_kernel, bb=bb, direction=direction, reverse=reverse),
        grid=(b // bb, nc),
        in_specs=[seq, seq, seq, pl.BlockSpec((bb, ML_CHUNK, LANE), cidx), c_spec, n_spec, n_spec],
        out_specs=[seq, c_spec, n_spec, n_spec],
        out_shape=[jax.ShapeDtypeStruct((b, t, w), F32),
                   jax.ShapeDtypeStruct(c0.shape, F32),
                   jax.ShapeDtypeStruct(n0.shape, F32),
                   jax.ShapeDtypeStruct(m0.shape, F32)],
        scratch_shapes=[pltpu.VMEM((bb, ML_HEADS, LANE, LANE), F32),
                        pltpu.VMEM((bb, ML_HEADS, 1, LANE), F32),
                        pltpu.VMEM((bb, ML_HEADS, 1, LANE), F32)],
        compiler_params=_cparams("parallel", "arbitrary"),
        name="mlstm_bwd" if reverse else "mlstm_fwd",
    )(q, k, v, gates, c0, n0, m0)


def _post_tail(x_ref, mix, mod_ref, g2_ref, wr_ref, br_ref, x_out_ref, h_ref, lg_ref):
    x1 = x_ref[...] + mod_ref[0, 2:3, :] * mix
    x_out_ref[...] = x1
    h2 = _rms(x1, g2_ref[...]) * (1.0 + mod_ref[0, 4:5, :]) + mod_ref[0, 3:4, :]
    h_ref[...] = h2.astype(BF16)
    lg_ref[...] = jnp.dot(h2, wr_ref[...], precision=HIGHEST, preferred_element_type=F32) + br_ref[...]


def _post_even_kernel(x_ref, a_ref, b_ref, wa_ref, wb_ref, mod_ref, g2_ref, wr_ref, br_ref,
                      x_out_ref, h_ref, lg_ref):
    mix = (jnp.dot(a_ref[...], wa_ref[...], preferred_element_type=F32)
           + jnp.dot(b_ref[...], wb_ref[...], preferred_element_type=F32))
    _post_tail(x_ref, mix, mod_ref, g2_ref, wr_ref, br_ref, x_out_ref, h_ref, lg_ref)


def _post_odd_kernel(x_ref, hf_ref, hb_ref, mo_ref, od_ref, mln_ref, sub_ref, lam_ref,
                     wa_ref, wb_ref, mod_ref, g2_ref, wr_ref, br_ref,
                     x_out_ref, h_ref, lg_ref, *, lam_init):
    lp = lam_ref[...]
    lam = (jnp.exp(jnp.sum(lp[0:1] * lp[1:2], axis=1, keepdims=True))
           - jnp.exp(jnp.sum(lp[2:3] * lp[3:4], axis=1, keepdims=True)) + lam_init)
    mix = None
    for hd in range(ML_HEADS):
        sl = slice(hd * LANE, (hd + 1) * LANE)
        hm = hf_ref[:, sl] + hb_ref[:, sl]
        om = _rms(hm, mln_ref[:, sl]) * _sigmoid(mo_ref[:, sl])
        part = jnp.dot(om.astype(BF16), wa_ref[sl, :], preferred_element_type=F32)
        mix = part if mix is None else mix + part
    for hd in range(DF_HEADS):
        o1 = od_ref[:, (2 * hd) * LANE:(2 * hd + 1) * LANE]
        o2 = od_ref[:, (2 * hd + 1) * LANE:(2 * hd + 2) * LANE]
        od = _rms(o1 - lam * o2, sub_ref[...]) * (1.0 - lam_init)
        mix = mix + jnp.dot(od.astype(BF16), wb_ref[hd * LANE:(hd + 1) * LANE, :],
                            preferred_element_type=F32)
    _post_tail(x_ref, mix, mod_ref, g2_ref, wr_ref, br_ref, x_out_ref, h_ref, lg_ref)


def _post_mix(kernel, x, row_inputs, const_inputs, mod, g2, wr, br, group_of_tile, name):
    n, d = x.shape
    tm = ROW_TILE
    row = lambda a: pl.BlockSpec((tm, a.shape[1]), lambda i: (i, 0))
    const = lambda a: pl.BlockSpec(a.shape, lambda i: (0,) * a.ndim)
    in_specs = ([row(x)] + [row(a) for a in row_inputs] + [const(a) for a in const_inputs]
                + [pl.BlockSpec((1, 6, d), lambda i: (group_of_tile(i), 0, 0)),
                   const(g2), const(wr), const(br)])
    return pl.pallas_call(
        kernel,
        grid=(n // tm,),
        in_specs=in_specs,
        out_specs=[pl.BlockSpec((tm, d), lambda i: (i, 0)),
                   pl.BlockSpec((tm, d), lambda i: (i, 0)),
                   pl.BlockSpec((tm, LANE), lambda i: (i, 0))],
        out_shape=[jax.ShapeDtypeStruct((n, d), F32), jax.ShapeDtypeStruct((n, d), BF16),
                   jax.ShapeDtypeStruct((n, LANE), F32)],
        compiler_params=_cparams("parallel"),
        name=name,
    )(x, *row_inputs, *const_inputs, mod, g2, wr, br)


def _expert_kernel(be_ref, nu_ref, x_ref, gate_ref, w1_ref, b1_ref, w2_ref, b2_ref, o_ref):
    i = pl.program_id(0)

    @pl.when(i < nu_ref[0])
    def _():
        x = x_ref[...]
        acc = None
        fc = 256
        for c in range(D_FF // fc):
            glu = jnp.dot(x, w1_ref[0, :, c * fc:(c + 1) * fc], preferred_element_type=F32)
            glu = glu + b1_ref[0, :, c * fc:(c + 1) * fc]
            lin = jnp.dot(x, w1_ref[0, :, D_FF + c * fc:D_FF + (c + 1) * fc],
                          preferred_element_type=F32)
            lin = lin + b1_ref[0, :, D_FF + c * fc:D_FF + (c + 1) * fc]
            glu = jnp.minimum(glu, SWIGLU_LIMIT)
            lin = jnp.clip(lin, -SWIGLU_LIMIT, SWIGLU_LIMIT)
            act = glu * _sigmoid(SWIGLU_ALPHA * glu) * (lin + 1.0)
            part = jnp.dot(act.astype(BF16), w2_ref[0, c * fc:(c + 1) * fc, :],
                           preferred_element_type=F32)
            acc = part if acc is None else acc + part
        o_ref[...] = (acc + b2_ref[0]) * gate_ref[...]

    @pl.when(i >= nu_ref[0])
    def _():
        o_ref[...] = jnp.zeros(o_ref.shape, F32)


def _expert_ffn(blk_e, n_used, xs, row_gate, w1, b1, w2, b2):
    n_rows, d = xs.shape
    nb = n_rows // MOE_ROWS
    grid_spec = pltpu.PrefetchScalarGridSpec(
        num_scalar_prefetch=2,
        grid=(nb,),
        in_specs=[pl.BlockSpec((MOE_ROWS, d), lambda i, be, nu: (i, 0)),
                  pl.BlockSpec((MOE_ROWS, 1), lambda i, be, nu: (i, 0)),
                  pl.BlockSpec((1, d, 2 * D_FF), lambda i, be, nu: (be[i], 0, 0)),
                  pl.BlockSpec((1, 1, 2 * D_FF), lambda i, be, nu: (be[i], 0, 0)),
                  pl.BlockSpec((1, D_FF, d), lambda i, be, nu: (be[i], 0, 0)),
                  pl.BlockSpec((1, 1, d), lambda i, be, nu: (be[i], 0, 0))],
        out_specs=pl.BlockSpec((MOE_ROWS, d), lambda i, be, nu: (i, 0)),
    )
    return pl.pallas_call(
        _expert_kernel,
        grid_spec=grid_spec,
        out_shape=jax.ShapeDtypeStruct((n_rows, d), F32),
        compiler_params=_cparams("arbitrary"),
        name="expert_ffn",
    )(blk_e, n_used, xs, row_gate, w1, b1, w2, b2)


def _combine_kernel(*refs, final):
    if final:
        x_ref, y_ref, mod_ref, fg_ref, x_out_ref, yn_ref = refs
    else:
        x_ref, y_ref, mod_ref, x_out_ref = refs
    moe = y_ref[:, 0, :] + y_ref[:, 1, :] + y_ref[:, 2, :] + y_ref[:, 3, :]
    x2 = x_ref[...] + mod_ref[0, 5:6, :] * moe
    x_out_ref[...] = x2
    if final:
        yn_ref[...] = _rms(x2, fg_ref[...])


def _combine(x, yg, row_offset_tiles, mod, group_of_tile, final_g):
    n, d = x.shape
    tm = ROW_TILE
    final = final_g is not None
    in_specs = [pl.BlockSpec((tm, d), lambda i: (i, 0)),
                pl.BlockSpec((tm, TOP_K, d), lambda i: (i + row_offset_tiles, 0, 0)),
                pl.BlockSpec((1, 6, d), lambda i: (group_of_tile(i), 0, 0))]
    args = [x, yg, mod]
    out_shape = [jax.ShapeDtypeStruct((n, d), F32)]
    if final:
        in_specs.append(pl.BlockSpec((1, d), lambda i: (0, 0)))
        args.append(final_g)
        out_shape.append(jax.ShapeDtypeStruct((n, d), F32))
    out_specs = [pl.BlockSpec((tm, d), lambda i: (i, 0)) for _ in out_shape]
    return pl.pallas_call(
        functools.partial(_combine_kernel, final=final),
        grid=(n // tm,),
        in_specs=in_specs,
        out_specs=out_specs,
        out_shape=out_shape,
        compiler_params=_cparams("parallel"),
        name="moe_combine_final" if final else "moe_combine",
    )(*args)


def _moe(h2, logits, w1, b1, w2, b2):
    n, d = h2.shape
    a = n * TOP_K
    top_val, top_idx = lax.top_k(logits[:, :N_EXPERTS], TOP_K)
    gates = jax.nn.softmax(top_val, axis=-1)
    flat_e = top_idx.reshape(a)
    order = jnp.argsort(flat_e)
    sorted_e = flat_e[order]
    counts = jnp.bincount(flat_e, length=N_EXPERTS)
    padded = (counts + MOE_ROWS - 1) // MOE_ROWS * MOE_ROWS
    pad_end = jnp.cumsum(padded)
    dest = (pad_end - padded)[sorted_e] + jnp.arange(a) - (jnp.cumsum(counts) - counts)[sorted_e]
    nb = -(-a // MOE_ROWS) + N_EXPERTS
    n_rows = nb * MOE_ROWS
    row_tok = jnp.zeros((n_rows,), jnp.int32).at[dest].set((order // TOP_K).astype(jnp.int32))
    row_gate = jnp.zeros((n_rows,), F32).at[dest].set(gates.reshape(a)[order])
    pos = jnp.zeros((a,), jnp.int32).at[order].set(dest.astype(jnp.int32))
    blk_e = jnp.minimum(jnp.searchsorted(pad_end, jnp.arange(nb) * MOE_ROWS, side='right'),
                        N_EXPERTS - 1).astype(jnp.int32)
    n_used = (pad_end[-1] // MOE_ROWS).astype(jnp.int32).reshape(1)
    xs = jnp.take(h2, row_tok, axis=0)
    yb = _expert_ffn(blk_e, n_used, xs, row_gate.reshape(n_rows, 1), w1, b1, w2, b2)
    return jnp.take(yb, pos, axis=0).reshape(n, TOP_K, d)


def _half_split_perm(n):
    return np.concatenate([np.arange(0, n, 2), np.arange(1, n, 2)])


def _pad_heads(w, heads, hd):
    k = w.shape[0]
    w = w.reshape(k, heads, hd)
    return jnp.pad(w, ((0, 0), (0, 0), (0, LANE - hd))).reshape(k, heads * LANE)


def _pad_rows(w, heads, hd):
    n = w.shape[1]
    w = w.reshape(heads, hd, n)
    return jnp.pad(w, ((0, 0), (0, LANE - hd), (0, 0))).reshape(heads * LANE, n)


def _rope_tables(t, rot_dim, lane_off):
    tok = jnp.arange(t)
    row = (tok // GRID_W).astype(F32)
    col = (tok % GRID_W).astype(F32)
    nf = rot_dim // 4
    inv = ROPE_THETA ** (-jnp.arange(nf, dtype=F32) / nf)
    ang = jnp.concatenate([row[:, None] * inv, col[:, None] * inv], axis=-1)
    cos, sin = jnp.cos(ang), jnp.sin(ang)
    half = rot_dim // 2
    cos_t = jnp.ones((t, LANE), F32)
    cos_t = cos_t.at[:, lane_off:lane_off + half].set(cos).at[:, lane_off + half:lane_off + rot_dim].set(cos)
    sin_a = jnp.zeros((t, LANE), F32).at[:, lane_off:lane_off + half].set(-sin)
    sin_b = jnp.zeros((t, LANE), F32).at[:, lane_off + half:lane_off + rot_dim].set(sin)
    return jnp.stack([cos_t, sin_a, sin_b])


def _na_bias_table(rpb):
    c = np.arange(GRID_W)[:, None]
    kc = np.arange(GRID_W)[None, :]
    wstart = np.clip(c - NA_KW // 2, 0, GRID_W - NA_KW)
    mask = (kc >= wstart) & (kc < wstart + NA_KW)
    dcol = np.clip(kc - c + NA_KW - 1, 0, 2 * NA_KW - 2)
    d0 = np.arange(NA_KR)[:, None] + np.arange(NA_KR)[None, :]
    tab = rpb[:, d0][:, :, :, dcol]
    tab = jnp.where(mask[None, None, None], tab.astype(F32), NEG_BIG)
    tab = jnp.transpose(tab, (0, 1, 3, 2, 4))
    return tab.reshape(NA_HEADS, NA_KR, GRID_W, NA_KR * GRID_W)


def kernel(x_prompt, x_sample, c, cache_mla_ckv, cache_mla_krope, cache_na_k, cache_na_v, state_mlstm_C, state_mlstm_n, state_mlstm_m, cache_diff_k, cache_diff_v, c_ctx, w_mod, b_mod, norm_g, final_g, a_w_in, a_q_norm, a_w_uq, a_kv_norm, a_w_ukv, a_rpb, a_w_out, b_w_in, b_gate_bias, b_ml_norm, b_df_lambda, b_df_subln, b_w_out, moe_w_router, moe_b_router, moe_w1, moe_b1, moe_w2, moe_b2):
    bp, lp, d = x_prompt.shape
    bs, ts, _ = x_sample.shape
    n_p, n_s = bp * lp, bs * ts
    past = cache_mla_ckv.shape[2]
    tiles_per_batch = ts // ROW_TILE
    grp_p = lambda i: 0
    grp_s = lambda i: 1 + i // tiles_per_batch

    xp = x_prompt.reshape(n_p, d)
    xs = x_sample.reshape(n_s, d)
    cond8 = jnp.concatenate([c_ctx[None, :], c, jnp.zeros((8 - 1 - bs, d), F32)], axis=0)
    mod_all = _modulation(cond8, w_mod, b_mod)

    perm_a = _half_split_perm(MLA_ROPE)
    inv_a = np.argsort(perm_a)
    perm_d = _half_split_perm(DF_QK)
    inv_d = np.argsort(perm_d)

    def moe_layer(l, x1p, x1s, h2p, h2s, lgp, lgs, final):
        h2 = jnp.concatenate([h2p, h2s], axis=0)
        lg = jnp.concatenate([lgp, lgs], axis=0)
        yg = _moe(h2, lg, moe_w1[l].astype(BF16), moe_b1[l][:, None, :],
                  moe_w2[l].astype(BF16), moe_b2[l][:, None, :])
        fg = final_g.reshape(1, d) if final else None
        outp = _combine(x1p, yg, 0, mod_all[l], grp_p, fg)
        outs = _combine(x1s, yg, n_p // ROW_TILE, mod_all[l], grp_s, fg)
        return outp, outs

    def router_w(l):
        wr = jnp.pad(moe_w_router[l], ((0, 0), (0, LANE - N_EXPERTS)))
        br = jnp.pad(moe_b_router[l], (0, LANE - N_EXPERTS)).reshape(1, LANE)
        return wr, br

    l, e = 0, 0
    w_in = a_w_in[e]
    kr_cols = w_in[:, 384:416][:, perm_a]
    w0 = jnp.concatenate([
        w_in[:, :384], jnp.pad(kr_cols, ((0, 0), (0, LANE - MLA_ROPE))),
        _pad_heads(w_in[:, 416:928], NA_HEADS, NA_HD),
        _pad_heads(w_in[:, 928:1440], NA_HEADS, NA_HD),
        _pad_heads(w_in[:, 1440:1952], NA_HEADS, NA_HD)], axis=1).astype(BF16)
    wuq = a_w_uq[e].reshape(MLA_Q_LORA, MLA_HEADS, MLA_NOPE + MLA_ROPE)
    wuq = jnp.concatenate([wuq[:, :, :MLA_NOPE], wuq[:, :, MLA_NOPE:][:, :, perm_a]], axis=2)
    wuq = _pad_heads(wuq.reshape(MLA_Q_LORA, -1), MLA_HEADS, MLA_NOPE + MLA_ROPE).astype(BF16)
    wukv = a_w_ukv[e].reshape(MLA_KV_LORA, MLA_HEADS, MLA_NOPE + MLA_V)
    wk = _pad_heads(wukv[:, :, :MLA_NOPE].reshape(MLA_KV_LORA, -1), MLA_HEADS, MLA_NOPE).astype(BF16)
    wv = _pad_heads(wukv[:, :, MLA_NOPE:].reshape(MLA_KV_LORA, -1), MLA_HEADS, MLA_V).astype(BF16)
    sel_np = np.zeros((LANE, MLA_HEADS * LANE), np.float32)
    for hd in range(MLA_HEADS):
        sel_np[np.arange(MLA_ROPE), hd * LANE + MLA_NOPE + np.arange(MLA_ROPE)] = 1.0
    sel = jnp.asarray(sel_np, BF16)
    qn_g = a_q_norm[e].reshape(1, -1)
    kvn_g = a_kv_norm[e].reshape(1, -1)
    g1 = norm_g[l, 0].reshape(1, d)
    g2 = norm_g[l, 1].reshape(1, d)
    tabs_a = (_rope_tables(ts, MLA_ROPE, MLA_NOPE), _rope_tables(ts, MLA_ROPE, 0))
    mla_scale = (MLA_NOPE + MLA_ROPE) ** -0.5
    wide = MLA_HEADS * LANE

    q_p, ckv_p, kr_p, nq_p, nk_p, nv_p = _even_proj(xp, g1, mod_all[l], w0, qn_g, wuq, kvn_g, grp_p, None, F32)
    new_mla_ckv = ckv_p.reshape(bp, 1, lp, MLA_KV_LORA)
    new_mla_krope = kr_p[:, :MLA_ROPE][:, inv_a].reshape(bp, 1, lp, MLA_ROPE)
    new_na_k = nk_p.reshape(bp, 1, lp, NA_HEADS, LANE)[..., :NA_HD]
    new_na_v = nv_p.reshape(bp, 1, lp, NA_HEADS, LANE)[..., :NA_HD]
    k_p, v_p = _kv_expand(ckv_p, kr_p, wk, sel, wv)
    ident = lambda hd: hd
    o_mla_p = _flash(q_p.reshape(bp, lp, wide), k_p.reshape(bp, lp, wide), v_p.reshape(bp, lp, wide),
                     scale=mla_scale, v_of_head=ident, out_dtype=BF16)
    o_na_p = _flash(nq_p.reshape(bp, lp, wide), nk_p.astype(BF16).reshape(bp, lp, wide),
                    nv_p.astype(BF16).reshape(bp, lp, wide),
                    scale=NA_HD ** -0.5, v_of_head=ident, out_dtype=BF16)
    q_s, ckv_s, kr_s, nq_s, nk_s, nv_s = _even_proj(xs, g1, mod_all[l], w0, qn_g, wuq, kvn_g, grp_s, tabs_a, BF16)
    ckv_all = jnp.concatenate([cache_mla_ckv[:, e], ckv_s.reshape(bs, ts, LANE)], axis=1)
    kr_cache = jnp.pad(cache_mla_krope[:, e][..., perm_a], ((0, 0), (0, 0), (0, LANE - MLA_ROPE)))
    kr_all = jnp.concatenate([kr_cache, kr_s.reshape(bs, ts, LANE)], axis=1)
    n_all = past + ts
    k_s, v_s = _kv_expand(ckv_all.reshape(bs * n_all, LANE), kr_all.reshape(bs * n_all, LANE), wk, sel, wv)
    o_mla_s = _flash(q_s.reshape(bs, ts, wide), k_s.reshape(bs, n_all, wide), v_s.reshape(bs, n_all, wide),
                     scale=mla_scale, v_of_head=ident, out_dtype=BF16)
    pad_hd = ((0, 0), (0, 0), (0, 0), (0, LANE - NA_HD))
    kc = jnp.pad(cache_na_k[:, e], pad_hd).astype(BF16).reshape(bs, past, wide)
    vc = jnp.pad(cache_na_v[:, e], pad_hd).astype(BF16).reshape(bs, past, wide)
    o_na_s = _na_latent(nq_s.reshape(bs, ts, wide), nk_s.reshape(bs, ts, wide), nv_s.reshape(bs, ts, wide),
                        kc, vc, _na_bias_table(a_rpb[e]))
    w_out = a_w_out[e]
    wa = _pad_rows(w_out[:MLA_HEADS * MLA_V], MLA_HEADS, MLA_V).astype(BF16)
    wb = _pad_rows(w_out[MLA_HEADS * MLA_V:], NA_HEADS, NA_HD).astype(BF16)
    wr, br = router_w(l)
    x1p, h2p, lgp = _post_mix(_post_even_kernel, xp, [o_mla_p.reshape(n_p, wide), o_na_p.reshape(n_p, wide)],
                              [wa, wb], mod_all[l], g2, wr, br, grp_p, "post_even")
    x1s, h2s, lgs = _post_mix(_post_even_kernel, xs, [o_mla_s.reshape(n_s, wide), o_na_s.reshape(n_s, wide)],
                              [wa, wb], mod_all[l], g2, wr, br, grp_s, "post_even")
    (xp,), (xs,) = moe_layer(l, x1p, x1s, h2p, h2s, lgp, lgs, False)

    l, o = 1, 0
    lam_init = 0.8 - 0.6 * math.exp(-0.3 * l)
    w_in = b_w_in[o]
    cuts = np.cumsum([0, 256, 256, 512, 512, 16, 512, 512, 512])
    seg = lambda i: w_in[:, cuts[i]:cuts[i + 1]]

    def diff_cols(w):
        w = w.reshape(d, 2 * DF_HEADS, DF_QK)[:, :, perm_d]
        return _pad_heads(w.reshape(d, -1), 2 * DF_HEADS, DF_QK)

    w1p = jnp.concatenate([
        _pad_heads(seg(0), ML_HEADS, ML_DQK), _pad_heads(seg(1), ML_HEADS, ML_DQK), seg(2), seg(3),
        diff_cols(seg(5)), diff_cols(seg(6)), seg(7),
        jnp.pad(seg(4), ((0, 0), (0, LANE - 16)))], axis=1).astype(BF16)
    bg = jnp.pad(b_gate_bias[o].reshape(1, 16), ((0, 0), (0, LANE - 16)))
    g1 = norm_g[l, 0].reshape(1, d)
    g2 = norm_g[l, 1].reshape(1, d)
    tab_d = _rope_tables(ts, DF_QK, 0)
    df_v_of = lambda hd: hd // 2

    mq_p, mk_p, mv_p, mo_p, gt_p, dq_p, dk_p, dv_p = _odd_proj(xp, g1, mod_all[l], w1p, bg, grp_p, None, F32)
    new_diff_k = dk_p.reshape(bp, 1, lp, DF_HEADS, 2, LANE)[..., :DF_QK][..., inv_d]
    new_diff_v = dv_p.reshape(bp, 1, lp, DF_HEADS, DF_V)
    mw = ML_HEADS * LANE
    zc = jnp.zeros((bp, ML_HEADS, LANE, LANE), F32)
    zn = jnp.zeros((bp, ML_HEADS, 1, LANE), F32)
    seq_p = lambda a: a.reshape(bp, lp, -1)
    hf_p, cf_f, nf_f, mf_f = _mlstm_dir(seq_p(mq_p), seq_p(mk_p), seq_p(mv_p), seq_p(gt_p), zc, zn, zn,
                                        direction=0, bb=4)
    hb_p, cf_b, nf_b, mf_b = _mlstm_dir(seq_p(mq_p), seq_p(mk_p), seq_p(mv_p), seq_p(gt_p), zc, zn, zn,
                                        direction=1, bb=4)
    new_mlstm_C = jnp.stack([cf_f, cf_b], axis=1)[..., :ML_DQK][:, None]
    new_mlstm_n = jnp.stack([nf_f, nf_b], axis=1)[:, :, :, 0, :ML_DQK][:, None]
    new_mlstm_m = jnp.stack([mf_f, mf_b], axis=1)[:, :, :, 0, 0][:, None]
    od_p = _flash(seq_p(dq_p), seq_p(dk_p.astype(BF16)), seq_p(dv_p.astype(BF16)),
                  scale=DF_QK ** -0.5, v_of_head=df_v_of, out_dtype=F32)

    mq_s, mk_s, mv_s, mo_s, gt_s, dq_s, dk_s, dv_s = _odd_proj(xs, g1, mod_all[l], w1p, bg, grp_s, tab_d, BF16)
    seq_s = lambda a: a.reshape(bs, ts, -1)
    c0 = jnp.pad(state_mlstm_C[:, o], ((0, 0), (0, 0), (0, 0), (0, 0), (0, LANE - ML_DQK)))
    n0 = jnp.pad(state_mlstm_n[:, o], ((0, 0), (0, 0), (0, 0), (0, LANE - ML_DQK)))[:, :, :, None, :]
    m0 = jnp.broadcast_to(state_mlstm_m[:, o][:, :, :, None, None], (bs, 2, ML_HEADS, 1, LANE))
    hf_s = _mlstm_dir(seq_s(mq_s), seq_s(mk_s), seq_s(mv_s), seq_s(gt_s), c0[:, 0], n0[:, 0], m0[:, 0],
                      direction=0, bb=4)[0]
    hb_s = _mlstm_dir(seq_s(mq_s), seq_s(mk_s), seq_s(mv_s), seq_s(gt_s), c0[:, 1], n0[:, 1], m0[:, 1],
                      direction=1, bb=4)[0]
    dk_cache = cache_diff_k[:, o][..., perm_d]
    dk_cache = jnp.pad(dk_cache, ((0, 0),) * 4 + ((0, LANE - DF_QK),)).astype(BF16).reshape(bs, past, 1024)
    dk_all = jnp.concatenate([dk_cache, seq_s(dk_s)], axis=1)
    dv_all = jnp.concatenate([cache_diff_v[:, o].astype(BF16).reshape(bs, past, 512), seq_s(dv_s)], axis=1)
    od_s = _flash(seq_s(dq_s), dk_all, dv_all, scale=DF_QK ** -0.5, v_of_head=df_v_of, out_dtype=F32)

    w_out = b_w_out[o]
    wa = w_out[:ML_HEADS * ML_DV].astype(BF16)
    wb = w_out[ML_HEADS * ML_DV:].astype(BF16)
    mln = b_ml_norm[o].reshape(1, ML_HEADS * ML_DV)
    sub = b_df_subln[o].reshape(1, DF_V)
    lam_p = jnp.pad(b_df_lambda[o], ((0, 4), (0, LANE - DF_QK)))
    wr, br = router_w(l)
    post_odd = functools.partial(_post_odd_kernel, lam_init=lam_init)
    x1p, h2p, lgp = _post_mix(post_odd, xp, [hf_p.reshape(n_p, mw), hb_p.reshape(n_p, mw), mo_p,
                                             od_p.reshape(n_p, 1024)],
                              [mln, sub, lam_p, wa, wb], mod_all[l], g2, wr, br, grp_p, "post_odd")
    x1s, h2s, lgs = _post_mix(post_odd, xs, [hf_s.reshape(n_s, mw), hb_s.reshape(n_s, mw), mo_s,
                                             od_s.reshape(n_s, 1024)],
                              [mln, sub, lam_p, wa, wb], mod_all[l], g2, wr, br, grp_s, "post_odd")
    (_, y_p), (_, y_s) = moe_layer(l, x1p, x1s, h2p, h2s, lgp, lgs, True)

    return (y_p.reshape(bp, lp, d), y_s.reshape(bs, ts, d), new_mla_ckv, new_mla_krope, new_na_k, new_na_v,
            new_mlstm_C, new_mlstm_n, new_mlstm_m, new_diff_k, new_diff_v)
```

```python
import functools
import math

import numpy as np
import jax
import jax.numpy as jnp
from jax import lax
from jax.experimental import pallas as pl
from jax.experimental.pallas import tpu as pltpu

F32 = jnp.float32
BF16 = jnp.bfloat16
HIGHEST = lax.Precision.HIGHEST

D_MODEL = 1024
GRID_W = 64
LANE = 128
EPS = 1e-6
ROPE_THETA = 10000.0

MLA_HEADS, MLA_Q_LORA, MLA_KV_LORA, MLA_NOPE, MLA_ROPE, MLA_V = 8, 256, 128, 64, 32, 64
NA_HEADS, NA_HD, NA_KR, NA_KW = 8, 64, 8, 16
ML_HEADS, ML_DQK, ML_DV, ML_CHUNK = 4, 64, 128, 64
DF_HEADS, DF_QK, DF_V = 4, 64, 128
N_EXPERTS, TOP_K, D_FF = 32, 4, 1024
SWIGLU_LIMIT, SWIGLU_ALPHA = 7.0, 1.702

ROW_TILE = 256
MOE_ROWS = 256
ATT_TQ, ATT_TK = 512, 256
NEG_BIG = -1e30
VMEM_LIMIT = 56 * 1024 * 1024


def _cparams(*sem):
    return pltpu.CompilerParams(dimension_semantics=sem, vmem_limit_bytes=VMEM_LIMIT)


def _rms(x, g):
    return x * lax.rsqrt(jnp.mean(x * x, axis=-1, keepdims=True) + EPS) * g


def _sigmoid(x):
    return 1.0 / (1.0 + jnp.exp(-x))


def _mod_kernel(c_ref, w_ref, b_ref, o_ref):
    c = c_ref[...]
    s = c * _sigmoid(c)
    o_ref[0] = jnp.dot(s, w_ref[0], precision=HIGHEST, preferred_element_type=F32) + b_ref[0]


def _modulation(cond8, w_mod, b_mod):
    depth, d, n6 = w_mod.shape
    tn = 1024
    out = pl.pallas_call(
        _mod_kernel,
        grid=(depth, n6 // tn),
        in_specs=[
            pl.BlockSpec((8, d), lambda l, j: (0, 0)),
            pl.BlockSpec((1, d, tn), lambda l, j: (l, 0, j)),
            pl.BlockSpec((1, 1, tn), lambda l, j: (l, 0, j)),
        ],
        out_specs=pl.BlockSpec((1, 8, tn), lambda l, j: (l, 0, j)),
        out_shape=jax.ShapeDtypeStruct((depth, 8, n6), F32),
        compiler_params=_cparams("parallel", "parallel"),
        name="modulation",
    )(cond8, w_mod, b_mod.reshape(depth, 1, n6))
    return out.reshape(depth, 8, 6, d)


def _rope_block(x, cos, sin_a, sin_b, half):
    up = pltpu.roll(x, LANE - half, axis=1)
    dn = pltpu.roll(x, half, axis=1)
    return x * cos + up * sin_a + dn * sin_b


def _even_proj_kernel(*refs, rope):
    if rope:
        (x_ref, g_ref, mod_ref, w_ref, qn_ref, wuq_ref, kvn_ref, tq_ref, tk_ref,
         q_ref, ckv_ref, kr_ref, nq_ref, nk_ref, nv_ref) = refs
    else:
        (x_ref, g_ref, mod_ref, w_ref, qn_ref, wuq_ref, kvn_ref,
         q_ref, ckv_ref, kr_ref, nq_ref, nk_ref, nv_ref) = refs
    x = x_ref[...]
    h = _rms(x, g_ref[...]) * (1.0 + mod_ref[0, 1:2, :]) + mod_ref[0, 0:1, :]
    hb = h.astype(BF16)

    def proj(a, b):
        return jnp.dot(hb, w_ref[:, a:b], preferred_element_type=F32)

    q_lat = proj(0, 256)
    qn = _rms(q_lat, qn_ref[...]).astype(BF16)
    for hd in range(MLA_HEADS):
        qh = jnp.dot(qn, wuq_ref[:, hd * LANE:(hd + 1) * LANE], preferred_element_type=F32)
        if rope:
            qh = _rope_block(qh, tq_ref[0], tq_ref[1], tq_ref[2], MLA_ROPE // 2)
        q_ref[:, hd * LANE:(hd + 1) * LANE] = qh.astype(q_ref.dtype)
    ckv_ref[...] = _rms(proj(256, 384), kvn_ref[...])
    kr = proj(384, 512)
    if rope:
        kr = _rope_block(kr, tk_ref[0], tk_ref[1], tk_ref[2], MLA_ROPE // 2)
    kr_ref[...] = kr
    nq_ref[...] = proj(512, 1536).astype(nq_ref.dtype)
    nk_ref[...] = proj(1536, 2560).astype(nk_ref.dtype)
    nv_ref[...] = proj(2560, 3584).astype(nv_ref.dtype)


def _even_proj(x, g, mod, w, qn, wuq, kvn, group_of_tile, rope_tabs, kv_dtype):
    n, d = x.shape
    tm = ROW_TILE
    rope = rope_tabs is not None
    in_specs = [
        pl.BlockSpec((tm, d), lambda i: (i, 0)),
        pl.BlockSpec((1, d), lambda i: (0, 0)),
        pl.BlockSpec((1, 6, d), lambda i: (group_of_tile(i), 0, 0)),
        pl.BlockSpec(w.shape, lambda i: (0, 0)),
        pl.BlockSpec(qn.shape, lambda i: (0, 0)),
        pl.BlockSpec(wuq.shape, lambda i: (0, 0)),
        pl.BlockSpec(kvn.shape, lambda i: (0, 0)),
    ]
    args = [x, g, mod, w, qn, wuq, kvn]
    if rope:
        tq, tk = rope_tabs
        nblk = tq.shape[1] // tm
        in_specs += [pl.BlockSpec((3, tm, LANE), lambda i: (0, i % nblk, 0)),
                     pl.BlockSpec((3, tm, LANE), lambda i: (0, i % nblk, 0))]
        args += [tq, tk]
    wide = MLA_HEADS * LANE
    out_shape = [
        jax.ShapeDtypeStruct((n, wide), BF16),
        jax.ShapeDtypeStruct((n, LANE), F32),
        jax.ShapeDtypeStruct((n, LANE), F32),
        jax.ShapeDtypeStruct((n, wide), BF16),
        jax.ShapeDtypeStruct((n, wide), kv_dtype),
        jax.ShapeDtypeStruct((n, wide), kv_dtype),
    ]
    out_specs = [pl.BlockSpec((tm, s.shape[1]), lambda i: (i, 0)) for s in out_shape]
    return pl.pallas_call(
        functools.partial(_even_proj_kernel, rope=rope),
        grid=(n // tm,),
        in_specs=in_specs,
        out_specs=out_specs,
        out_shape=out_shape,
        compiler_params=_cparams("parallel"),
        name="even_proj_rope" if rope else "even_proj",
    )(*args)


def _kv_expand_kernel(ckv_ref, kr_ref, wk_ref, sel_ref, wv_ref, k_ref, v_ref):
    c = ckv_ref[...].astype(BF16)
    r = kr_ref[...].astype(BF16)
    k = jnp.dot(c, wk_ref[...], preferred_element_type=F32)
    k = k + jnp.dot(r, sel_ref[...], preferred_element_type=F32)
    k_ref[...] = k.astype(BF16)
    v_ref[...] = jnp.dot(c, wv_ref[...], preferred_element_type=F32).astype(BF16)


def _kv_expand(ckv, kr, wk, sel, wv):
    n = ckv.shape[0]
    tm = ROW_TILE
    wide = MLA_HEADS * LANE
    const = lambda i: (0, 0)
    return pl.pallas_call(
        _kv_expand_kernel,
        grid=(n // tm,),
        in_specs=[pl.BlockSpec((tm, LANE), lambda i: (i, 0)),
                  pl.BlockSpec((tm, LANE), lambda i: (i, 0)),
                  pl.BlockSpec(wk.shape, const), pl.BlockSpec(sel.shape, const),
                  pl.BlockSpec(wv.shape, const)],
        out_specs=[pl.BlockSpec((tm, wide), lambda i: (i, 0))] * 2,
        out_shape=[jax.ShapeDtypeStruct((n, wide), BF16)] * 2,
        compiler_params=_cparams("parallel"),
        name="kv_expand",
    )(ckv, kr, wk, sel, wv)


def _flash_kernel(q_ref, k_ref, v_ref, o_ref, m_sc, l_sc, acc_sc, *, heads, v_of_head, scale):
    kj = pl.program_id(2)
    c2 = scale * math.log2(math.e)

    @pl.when(kj == 0)
    def _():
        m_sc[...] = jnp.full(m_sc.shape, NEG_BIG, F32)
        l_sc[...] = jnp.zeros(l_sc.shape, F32)
        acc_sc[...] = jnp.zeros(acc_sc.shape, F32)

    for hd in range(heads):
        sl = slice(hd * LANE, (hd + 1) * LANE)
        vh = v_of_head(hd)
        s_t = lax.dot_general(k_ref[0, :, sl], q_ref[0, :, sl], (((1,), (1,)), ((), ())),
                              preferred_element_type=F32)
        m_prev = m_sc[hd]
        m_new = jnp.maximum(m_prev, jnp.max(s_t, axis=0, keepdims=True))
        alpha = jnp.exp2((m_prev - m_new) * c2)
        p_t = jnp.exp2((s_t - m_new) * c2)
        l_sc[hd] = alpha * l_sc[hd] + jnp.sum(p_t, axis=0, keepdims=True)
        pv_t = lax.dot_general(v_ref[0, :, vh * LANE:(vh + 1) * LANE], p_t.astype(BF16),
                               (((0,), (0,)), ((), ())), preferred_element_type=F32)
        acc_sc[sl, :] = alpha * acc_sc[sl, :] + pv_t
        m_sc[hd] = m_new

    @pl.when(kj == pl.num_programs(2) - 1)
    def _():
        for hd in range(heads):
            sl = slice(hd * LANE, (hd + 1) * LANE)
            o_ref[0, :, sl] = (acc_sc[sl, :] / l_sc[hd]).T.astype(o_ref.dtype)


def _flash(q, k, v, *, scale, v_of_head, out_dtype):
    b, s, wq = q.shape
    nk = k.shape[1]
    wv = v.shape[2]
    heads = wq // LANE
    tq = min(ATT_TQ, s)
    tk = min(ATT_TK, nk)
    return pl.pallas_call(
        functools.partial(_flash_kernel, heads=heads, v_of_head=v_of_head, scale=scale),
        grid=(b, s // tq, nk // tk),
        in_specs=[pl.BlockSpec((1, tq, wq), lambda bi, qi, ki: (bi, qi, 0)),
                  pl.BlockSpec((1, tk, wq), lambda bi, qi, ki: (bi, ki, 0)),
                  pl.BlockSpec((1, tk, wv), lambda bi, qi, ki: (bi, ki, 0))],
        out_specs=pl.BlockSpec((1, tq, wq), lambda bi, qi, ki: (bi, qi, 0)),
        out_shape=jax.ShapeDtypeStruct((b, s, wq), out_dtype),
        scratch_shapes=[pltpu.VMEM((heads, 1, tq), F32), pltpu.VMEM((heads, 1, tq), F32),
                        pltpu.VMEM((wq, tq), F32)],
        compiler_params=_cparams("parallel", "parallel", "arbitrary"),
        name="flash_attention",
    )(q, k, v)


def _na_kernel(q_ref, k_ref, v_ref, kc_ref, vc_ref, bias_ref, o_ref, *, rows, scale):
    r = pl.program_id(1)
    r0 = jnp.clip(r - NA_KR // 2, 0, rows - NA_KR)
    start = pl.multiple_of(r0 * GRID_W, GRID_W)
    nloc = NA_KR * GRID_W
    for hd in range(NA_HEADS):
        sl = slice(hd * LANE, (hd + 1) * LANE)
        qh = q_ref[0, :, sl]
        kh = k_ref[0, pl.ds(start, nloc), sl]
        vh = v_ref[0, pl.ds(start, nloc), sl]
        dn = (((1,), (1,)), ((), ()))
        s_loc = lax.dot_general(qh, kh, dn, preferred_element_type=F32) * scale + bias_ref[hd, 0]
        s_ctx = lax.dot_general(qh, kc_ref[0, :, sl], dn, preferred_element_type=F32) * scale
        m = jnp.maximum(jnp.max(s_loc, axis=1, keepdims=True), jnp.max(s_ctx, axis=1, keepdims=True))
        p_loc = jnp.exp(s_loc - m)
        p_ctx = jnp.exp(s_ctx - m)
        l = jnp.sum(p_loc, axis=1, keepdims=True) + jnp.sum(p_ctx, axis=1, keepdims=True)
        o = (jnp.dot(p_loc.astype(BF16), vh, preferred_element_type=F32)
             + jnp.dot(p_ctx.astype(BF16), vc_ref[0, :, sl], preferred_element_type=F32))
        o_ref[0, :, sl] = (o / l).astype(o_ref.dtype)


def _na_latent(q, k, v, kc, vc, bias_tab):
    b, s, wide = q.shape
    rows = s // GRID_W
    lc = kc.shape[1]

    def bias_idx(bi, r):
        return (0, jnp.clip(r - NA_KR // 2, 0, rows - NA_KR) - r + NA_KR - 1, 0, 0)

    return pl.pallas_call(
        functools.partial(_na_kernel, rows=rows, scale=NA_HD ** -0.5),
        grid=(b, rows),
        in_specs=[pl.BlockSpec((1, GRID_W, wide), lambda bi, r: (bi, r, 0)),
                  pl.BlockSpec((1, s, wide), lambda bi, r: (bi, 0, 0)),
                  pl.BlockSpec((1, s, wide), lambda bi, r: (bi, 0, 0)),
                  pl.BlockSpec((1, lc, wide), lambda bi, r: (bi, 0, 0)),
                  pl.BlockSpec((1, lc, wide), lambda bi, r: (bi, 0, 0)),
                  pl.BlockSpec((NA_HEADS, 1, GRID_W, NA_KR * GRID_W), bias_idx)],
        out_specs=pl.BlockSpec((1, GRID_W, wide), lambda bi, r: (bi, r, 0)),
        out_shape=jax.ShapeDtypeStruct((b, s, wide), BF16),
        compiler_params=_cparams("parallel", "arbitrary"),
        name="na_latent",
    )(q, k, v, kc, vc, bias_tab)


def _odd_proj_kernel(*refs, rope):
    if rope:
        (x_ref, g_ref, mod_ref, w_ref, bg_ref, td_ref,
         mq_ref, mk_ref, mv_ref, mo_ref, gt_ref, dq_ref, dk_ref, dv_ref) = refs
    else:
        (x_ref, g_ref, mod_ref, w_ref, bg_ref,
         mq_ref, mk_ref, mv_ref, mo_ref, gt_ref, dq_ref, dk_ref, dv_ref) = refs
    x = x_ref[...]
    h = _rms(x, g_ref[...]) * (1.0 + mod_ref[0, 1:2, :]) + mod_ref[0, 0:1, :]
    hb = h.astype(BF16)

    def proj(a, b):
        return jnp.dot(hb, w_ref[:, a:b], preferred_element_type=F32)

    mq_ref[...] = proj(0, 512).astype(mq_ref.dtype)
    mk_ref[...] = (proj(512, 1024) * (ML_DQK ** -0.5)).astype(mk_ref.dtype)
    mv_ref[...] = proj(1024, 1536).astype(mv_ref.dtype)
    mo_ref[...] = proj(1536, 2048)
    for blk in range(8):
        qd = proj(2048 + blk * LANE, 2048 + (blk + 1) * LANE)
        kd = proj(3072 + blk * LANE, 3072 + (blk + 1) * LANE)
        if rope:
            qd = _rope_block(qd, td_ref[0], td_ref[1], td_ref[2], DF_QK // 2)
            kd = _rope_block(kd, td_ref[0], td_ref[1], td_ref[2], DF_QK // 2)
        dq_ref[:, blk * LANE:(blk + 1) * LANE] = qd.astype(dq_ref.dtype)
        dk_ref[:, blk * LANE:(blk + 1) * LANE] = kd.astype(dk_ref.dtype)
    dv_ref[...] = proj(4096, 4608).astype(dv_ref.dtype)
    gt = proj(4608, 4736) + bg_ref[...]
    lane = lax.broadcasted_iota(jnp.int32, gt.shape, 1)
    log_f = jnp.minimum(gt, 0.0) - jnp.log(1.0 + jnp.exp(-jnp.abs(gt)))
    gt_ref[...] = jnp.where((lane % 8) >= 4, log_f, gt)


def _odd_proj(x, g, mod, w, bg, group_of_tile, rope_tab, kv_dtype):
    n, d = x.shape
    tm = ROW_TILE
    rope = rope_tab is not None
    in_specs = [
        pl.BlockSpec((tm, d), lambda i: (i, 0)),
        pl.BlockSpec((1, d), lambda i: (0, 0)),
        pl.BlockSpec((1, 6, d), lambda i: (group_of_tile(i), 0, 0)),
        pl.BlockSpec(w.shape, lambda i: (0, 0)),
        pl.BlockSpec(bg.shape, lambda i: (0, 0)),
    ]
    args = [x, g, mod, w, bg]
    if rope:
        nblk = rope_tab.shape[1] // tm
        in_specs.append(pl.BlockSpec((3, tm, LANE), lambda i: (0, i % nblk, 0)))
        args.append(rope_tab)
    out_shape = [
        jax.ShapeDtypeStruct((n, 512), BF16),
        jax.ShapeDtypeStruct((n, 512), BF16),
        jax.ShapeDtypeStruct((n, 512), BF16),
        jax.ShapeDtypeStruct((n, 512), F32),
        jax.ShapeDtypeStruct((n, LANE), F32),
        jax.ShapeDtypeStruct((n, 1024), BF16),
        jax.ShapeDtypeStruct((n, 1024), kv_dtype),
        jax.ShapeDtypeStruct((n, 512), kv_dtype),
    ]
    out_specs = [pl.BlockSpec((tm, s.shape[1]), lambda i: (i, 0)) for s in out_shape]
    return pl.pallas_call(
        functools.partial(_odd_proj_kernel, rope=rope),
        grid=(n // tm,),
        in_specs=in_specs,
        out_specs=out_specs,
        out_shape=out_shape,
        compiler_params=_cparams("parallel"),
        name="odd_proj_rope" if rope else "odd_proj",
    )(*args)


def _mlstm_kernel(q_ref, k_ref, v_ref, g_ref, c0_ref, n0_ref, m0_ref,
                  h_ref, cf_ref, nf_ref, mf_ref, c_sc, n_sc, m_sc, *, bb, direction, reverse):
    step = pl.program_id(1)
    L = ML_CHUNK

    @pl.when(step == 0)
    def _():
        c_sc[...] = c0_ref[...]
        n_sc[...] = n0_ref[...]
        m_sc[...] = m0_ref[...]

    t_idx = lax.broadcasted_iota(jnp.int32, (L, L), 0)
    s_idx = lax.broadcasted_iota(jnp.int32, (L, L), 1)
    keep = (s_idx >= t_idx) if reverse else (s_idx <= t_idx)
    tri = keep.astype(F32)
    last = 0 if reverse else L - 1
    for bi in range(bb):
        g = g_ref[bi]
        bcum = jnp.dot(tri, g, precision=HIGHEST, preferred_element_type=F32)
        g_t = g.T
        b_t = bcum.T
        for hd in range(ML_HEADS):
            sl = slice(hd * LANE, (hd + 1) * LANE)
            ci = direction * 8 + hd
            cf = direction * 8 + 4 + hd
            li_row = g_t[ci:ci + 1, :]
            b_row = b_t[cf:cf + 1, :]
            li_col = g[:, ci:ci + 1]
            b_col = bcum[:, cf:cf + 1]
            m_old = m_sc[bi, hd][:, 0:1]
            n_old = n_sc[bi, hd]
            c_old = c_sc[bi, hd]
            qh = q_ref[bi, :, sl]
            kh = k_ref[bi, :, sl]
            vh = v_ref[bi, :, sl]
            dm = jnp.where(keep, b_col - b_row + li_row, NEG_BIG)
            m_inter = b_col + m_old
            m_t = jnp.maximum(m_inter, jnp.max(dm, axis=1, keepdims=True))
            qk = lax.dot_general(qh, kh, (((1,), (1,)), ((), ())), preferred_element_type=F32)
            w_intra = jnp.where(keep, jnp.exp(dm - m_t), 0.0) * qk
            w_inter = jnp.exp(m_inter - m_t)
            qc = lax.dot_general(qh, c_old.astype(BF16), (((1,), (1,)), ((), ())),
                                 preferred_element_type=F32)
            num = jnp.dot(w_intra.astype(BF16), vh, preferred_element_type=F32) + w_inter * qc
            qn = jnp.sum(qh.astype(F32) * n_old, axis=1, keepdims=True)
            den = jnp.sum(w_intra, axis=1, keepdims=True) + w_inter * qn
            h_ref[bi, :, sl] = num / jnp.maximum(jnp.abs(den), jnp.exp(-m_t))
            m_new = m_t[last:last + 1, :]
            b_last = b_col[last:last + 1, :]
            w_end = jnp.exp(b_last - b_col + li_col - m_new)
            decay = jnp.exp(b_last + m_old - m_new)
            wv = (w_end * vh.astype(F32)).astype(BF16)
            upd = lax.dot_general(wv, kh, (((0,), (0,)), ((), ())), preferred_element_type=F32)
            c_sc[bi, hd] = decay * c_old + upd
            n_sc[bi, hd] = decay * n_old + jnp.sum(w_end * kh.astype(F32), axis=0, keepdims=True)
            m_sc[bi, hd] = jnp.broadcast_to(m_new, (1, LANE))

    @pl.when(step == pl.num_programs(1) - 1)
    def _():
        cf_ref[...] = c_sc[...]
        nf_ref[...] = n_sc[...]
        mf_ref[...] = m_sc[...]


def _mlstm_dir(q, k, v, gates, c0, n0, m0, *, direction, bb):
    b, t, w = q.shape
    nc = t // ML_CHUNK
    reverse = direction == 1
    if reverse:
        cidx = lambda bi, c: (bi, nc - 1 - c, 0)
    else:
        cidx = lambda bi, c: (bi, c, 0)
    st4 = lambda bi, c: (bi, 0, 0, 0)
    seq = pl.BlockSpec((bb, ML_CHUNK, w), cidx)
    c_spec = pl.BlockSpec((bb, ML_HEADS, LANE, LANE), st4)
    n_spec = pl.BlockSpec((bb, ML_HEADS, 1, LANE), st4)
    return pl.pallas_call(
        functools.partial(_mlstm_kernel, bb=bb, direction=direction, reverse=reverse),
        grid=(b // bb, nc),
        in_specs=[seq, seq, seq, pl.BlockSpec((bb, ML_CHUNK, LANE), cidx), c_spec, n_spec, n_spec],
        out_specs=[seq, c_spec, n_spec, n_spec],
        out_shape=[jax.ShapeDtypeStruct((b, t, w), F32),
                   jax.ShapeDtypeStruct(c0.shape, F32),
                   jax.ShapeDtypeStruct(n0.shape, F32),
                   jax.ShapeDtypeStruct(m0.shape, F32)],
        scratch_shapes=[pltpu.VMEM((bb, ML_HEADS, LANE, LANE), F32),
                        pltpu.VMEM((bb, ML_HEADS, 1, LANE), F32),
                        pltpu.VMEM((bb, ML_HEADS, 1, LANE), F32)],
        compiler_params=_cparams("parallel", "arbitrary"),
        name="mlstm_bwd" if reverse else "mlstm_fwd",
    )(q, k, v, gates, c0, n0, m0)


ROUTE_IDX, ROUTE_RANK, ROUTE_GATE = 0, 4, 8


def _post_tail(x_ref, mix, mod_ref, g2_ref, wr_ref, br_ref, cin_ref,
               x_out_ref, h_ref, rt_ref, cout_ref, cnt_sc):
    @pl.when(pl.program_id(0) == 0)
    def _():
        cnt_sc[...] = cin_ref[...]

    x1 = x_ref[...] + mod_ref[0, 2:3, :] * mix
    x_out_ref[...] = x1
    h2 = _rms(x1, g2_ref[...]) * (1.0 + mod_ref[0, 4:5, :]) + mod_ref[0, 3:4, :]
    h_ref[...] = h2
    logits = jnp.dot(h2, wr_ref[...], precision=HIGHEST, preferred_element_type=F32) + br_ref[...]
    tm = logits.shape[0]
    lane = lax.broadcasted_iota(jnp.int32, (tm, LANE), 1)
    lg = jnp.where(lane < N_EXPERTS, logits, NEG_BIG)
    vals, idxs, hots = [], [], []
    for _ in range(TOP_K):
        mk = jnp.max(lg, axis=1, keepdims=True)
        idx = jnp.min(jnp.where(lg == mk, lane, LANE), axis=1, keepdims=True)
        hot = lane == idx
        vals.append(mk)
        idxs.append(idx)
        hots.append(hot)
        lg = jnp.where(hot, NEG_BIG, lg)
    ex = [jnp.exp(v - vals[0]) for v in vals]
    tot = ex[0] + ex[1] + ex[2] + ex[3]
    sel = jnp.zeros((tm, LANE), F32)
    for hot in hots:
        sel = sel + hot.astype(F32)
    t_idx = lax.broadcasted_iota(jnp.int32, (tm, tm), 0)
    s_idx = lax.broadcasted_iota(jnp.int32, (tm, tm), 1)
    earlier = (s_idx < t_idx).astype(BF16)
    prefix = jnp.dot(earlier, sel.astype(BF16), preferred_element_type=F32) + cnt_sc[...]
    route = jnp.zeros((tm, LANE), F32)
    for k in range(TOP_K):
        rank = jnp.sum(jnp.where(hots[k], prefix, 0.0), axis=1, keepdims=True)
        route = jnp.where(lane == ROUTE_IDX + k, idxs[k].astype(F32), route)
        route = jnp.where(lane == ROUTE_RANK + k, rank, route)
        route = jnp.where(lane == ROUTE_GATE + k, ex[k] / tot, route)
    rt_ref[...] = route
    cnt_new = cnt_sc[...] + jnp.sum(sel, axis=0, keepdims=True)
    cnt_sc[...] = cnt_new
    cout_ref[...] = cnt_new


def _post_even_kernel(x_ref, a_ref, b_ref, wa_ref, wb_ref, *rest):
    mix = (jnp.dot(a_ref[...], wa_ref[...], preferred_element_type=F32)
           + jnp.dot(b_ref[...], wb_ref[...], preferred_element_type=F32))
    _post_tail(x_ref, mix, *rest)


def _post_odd_kernel(x_ref, hf_ref, hb_ref, mo_ref, od_ref, mln_ref, sub_ref, lam_ref,
                     wa_ref, wb_ref, *rest, lam_init):
    lp = lam_ref[...]
    lam = (jnp.exp(jnp.sum(lp[0:1] * lp[1:2], axis=1, keepdims=True))
           - jnp.exp(jnp.sum(lp[2:3] * lp[3:4], axis=1, keepdims=True)) + lam_init)
    mix = None
    for hd in range(ML_HEADS):
        sl = slice(hd * LANE, (hd + 1) * LANE)
        hm = hf_ref[:, sl] + hb_ref[:, sl]
        om = _rms(hm, mln_ref[:, sl]) * _sigmoid(mo_ref[:, sl])
        part = jnp.dot(om.astype(BF16), wa_ref[sl, :], preferred_element_type=F32)
        mix = part if mix is None else mix + part
    for hd in range(DF_HEADS):
        o1 = od_ref[:, (2 * hd) * LANE:(2 * hd + 1) * LANE]
        o2 = od_ref[:, (2 * hd + 1) * LANE:(2 * hd + 2) * LANE]
        od = _rms(o1 - lam * o2, sub_ref[...]) * (1.0 - lam_init)
        mix = mix + jnp.dot(od.astype(BF16), wb_ref[hd * LANE:(hd + 1) * LANE, :],
                            preferred_element_type=F32)
    _post_tail(x_ref, mix, *rest)


def _post_mix(kernel, x, row_inputs, const_inputs, mod, g2, wr, br, cnt_in, group_of_tile, name):
    n, d = x.shape
    tm = ROW_TILE
    row = lambda a: pl.BlockSpec((tm, a.shape[1]), lambda i: (i, 0))
    const = lambda a: pl.BlockSpec(a.shape, lambda i: (0,) * a.ndim)
    in_specs = ([row(x)] + [row(a) for a in row_inputs] + [const(a) for a in const_inputs]
                + [pl.BlockSpec((1, 6, d), lambda i: (group_of_tile(i), 0, 0)),
                   const(g2), const(wr), const(br), const(cnt_in)])
    return pl.pallas_call(
        kernel,
        grid=(n // tm,),
        in_specs=in_specs,
        out_specs=[pl.BlockSpec((tm, d), lambda i: (i, 0)),
                   pl.BlockSpec((tm, d), lambda i: (i, 0)),
                   pl.BlockSpec((tm, LANE), lambda i: (i, 0)),
                   pl.BlockSpec((1, LANE), lambda i: (0, 0))],
        out_shape=[jax.ShapeDtypeStruct((n, d), F32), jax.ShapeDtypeStruct((n, d), F32),
                   jax.ShapeDtypeStruct((n, LANE), F32), jax.ShapeDtypeStruct((1, LANE), F32)],
        scratch_shapes=[pltpu.VMEM((1, LANE), F32)],
        compiler_params=_cparams("arbitrary"),
        name=name,
    )(x, *row_inputs, *const_inputs, mod, g2, wr, br, cnt_in)


DISPATCH_TOKENS = 256


def _dispatch_kernel(tail_ref, pend_ref, nu_ref, dest_hbm, hp_ref, hs_ref, xs_ref,
                     idx_smem, zero_buf, sem_idx, sem_rows, sem_zero, *, tiles_p, n_blocks):
    i = pl.program_id(0)
    nt = pl.num_programs(0)
    slot = i % 2
    n_copy = DISPATCH_TOKENS * TOP_K

    def idx_copy(step, sl):
        return pltpu.make_async_copy(dest_hbm.at[step], idx_smem.at[sl], sem_idx.at[sl])

    def zero_block(start):
        return pltpu.make_async_copy(zero_buf, xs_ref.at[pl.ds(start, MOE_ROWS)], sem_zero)

    def zero_row(row):
        return pltpu.make_async_copy(zero_buf.at[pl.ds(0, 1)], xs_ref.at[pl.ds(row, 1)], sem_rows)

    @pl.when(i == 0)
    def _():
        idx_copy(0, 0).start()
        zero_buf[...] = jnp.zeros(zero_buf.shape, F32)

        def per_expert(e, carry):
            def one(r, c):
                zero_row(r).start()
                return c
            return lax.fori_loop(tail_ref[e], pend_ref[e], one, carry)

        lax.fori_loop(0, N_EXPERTS, per_expert, 0)

        def zstart(b, carry):
            zero_block(pl.multiple_of(b * MOE_ROWS, MOE_ROWS)).start()
            return carry

        lax.fori_loop(nu_ref[0], n_blocks, zstart, 0)

        def per_expert_wait(e, carry):
            def one(r, c):
                zero_row(0).wait()
                return c
            return lax.fori_loop(tail_ref[e], pend_ref[e], one, carry)

        lax.fori_loop(0, N_EXPERTS, per_expert_wait, 0)

        def zwait(b, carry):
            zero_block(0).wait()
            return carry

        lax.fori_loop(nu_ref[0], n_blocks, zwait, 0)

    @pl.when(i + 1 < nt)
    def _():
        idx_copy(i + 1, 1 - slot).start()

    idx_copy(i, slot).wait()

    def scatter(h_ref):
        def issue(j, carry):
            row = idx_smem[slot, j]
            pltpu.make_async_copy(h_ref.at[pl.ds(j // TOP_K, 1)], xs_ref.at[pl.ds(row, 1)], sem_rows).start()
            return carry

        lax.fori_loop(0, n_copy, issue, 0, unroll=8)

        def drain(j, carry):
            pltpu.make_async_copy(h_ref.at[pl.ds(0, 1)], xs_ref.at[pl.ds(0, 1)], sem_rows).wait()
            return carry

        lax.fori_loop(0, n_copy, drain, 0, unroll=8)

    @pl.when(i < tiles_p)
    def _():
        scatter(hp_ref)

    @pl.when(i >= tiles_p)
    def _():
        scatter(hs_ref)


def _dispatch(tail_start, pad_end, n_used, dest, h2p, h2s, n_blocks):
    d = h2p.shape[1]
    tm = DISPATCH_TOKENS
    tiles_p = h2p.shape[0] // tm
    nt = tiles_p + h2s.shape[0] // tm
    grid_spec = pltpu.PrefetchScalarGridSpec(
        num_scalar_prefetch=3,
        grid=(nt,),
        in_specs=[pl.BlockSpec(memory_space=pl.ANY),
                  pl.BlockSpec((tm, d), lambda i, *_: (jnp.minimum(i, tiles_p - 1), 0)),
                  pl.BlockSpec((tm, d), lambda i, *_: (jnp.maximum(i - tiles_p, 0), 0))],
        out_specs=pl.BlockSpec(memory_space=pl.ANY),
        scratch_shapes=[pltpu.SMEM((2, tm * TOP_K), jnp.int32),
                        pltpu.VMEM((MOE_ROWS, d), F32),
                        pltpu.SemaphoreType.DMA((2,)),
                        pltpu.SemaphoreType.DMA(()),
                        pltpu.SemaphoreType.DMA(())],
    )
    return pl.pallas_call(
        functools.partial(_dispatch_kernel, tiles_p=tiles_p, n_blocks=n_blocks),
        grid_spec=grid_spec,
        out_shape=jax.ShapeDtypeStruct((n_blocks * MOE_ROWS, d), F32),
        compiler_params=_cparams("arbitrary"),
        name="moe_dispatch",
    )(tail_start, pad_end, n_used, dest.reshape(nt, tm * TOP_K), h2p, h2s)


def _expert_kernel(be_ref, nu_ref, x_ref, w1_ref, b1_ref, w2_ref, b2_ref, o_ref):
    i = pl.program_id(0)

    @pl.when(i < nu_ref[0])
    def _():
        x = x_ref[...].astype(BF16)
        acc = None
        fc = 256
        for c in range(D_FF // fc):
            glu = jnp.dot(x, w1_ref[0, :, c * fc:(c + 1) * fc], preferred_element_type=F32)
            glu = glu + b1_ref[0, :, c * fc:(c + 1) * fc]
            lin = jnp.dot(x, w1_ref[0, :, D_FF + c * fc:D_FF + (c + 1) * fc],
                          preferred_element_type=F32)
            lin = lin + b1_ref[0, :, D_FF + c * fc:D_FF + (c + 1) * fc]
            glu = jnp.minimum(glu, SWIGLU_LIMIT)
            lin = jnp.clip(lin, -SWIGLU_LIMIT, SWIGLU_LIMIT)
            act = glu * _sigmoid(SWIGLU_ALPHA * glu) * (lin + 1.0)
            part = jnp.dot(act.astype(BF16), w2_ref[0, c * fc:(c + 1) * fc, :],
                           preferred_element_type=F32)
            acc = part if acc is None else acc + part
        o_ref[...] = acc + b2_ref[0]

    @pl.when(i >= nu_ref[0])
    def _():
        o_ref[...] = jnp.zeros(o_ref.shape, F32)


def _expert_ffn(blk_e, n_used, xs, w1, b1, w2, b2):
    d = xs.shape[1]
    nb = blk_e.shape[0]
    grid_spec = pltpu.PrefetchScalarGridSpec(
        num_scalar_prefetch=2,
        grid=(nb,),
        in_specs=[pl.BlockSpec((MOE_ROWS, d), lambda i, be, nu: (i, 0)),
                  pl.BlockSpec((1, d, 2 * D_FF), lambda i, be, nu: (be[i], 0, 0)),
                  pl.BlockSpec((1, 1, 2 * D_FF), lambda i, be, nu: (be[i], 0, 0)),
                  pl.BlockSpec((1, D_FF, d), lambda i, be, nu: (be[i], 0, 0)),
                  pl.BlockSpec((1, 1, d), lambda i, be, nu: (be[i], 0, 0))],
        out_specs=pl.BlockSpec((MOE_ROWS, d), lambda i, be, nu: (i, 0)),
    )
    return pl.pallas_call(
        _expert_kernel,
        grid_spec=grid_spec,
        out_shape=jax.ShapeDtypeStruct((nb * MOE_ROWS, d), F32),
        compiler_params=_cparams("arbitrary"),
        name="expert_ffn",
    )(blk_e, n_used, xs, w1, b1, w2, b2)


COMBINE_TOKENS = 128


def _combine_kernel(*refs, final):
    if final:
        (dest_hbm, x_ref, rt_ref, mod_ref, fg_ref, yb_ref, x_out_ref, yn_ref,
         idx_smem, buf, sem_idx, sem_rows) = refs
    else:
        (dest_hbm, x_ref, rt_ref, mod_ref, yb_ref, x_out_ref,
         idx_smem, buf, sem_idx, sem_rows) = refs
    i = pl.program_id(0)
    nt = pl.num_programs(0)
    slot = i % 2
    n_copy = COMBINE_TOKENS * TOP_K

    def idx_copy(step, sl):
        return pltpu.make_async_copy(dest_hbm.at[step], idx_smem.at[sl], sem_idx.at[sl])

    @pl.when(i == 0)
    def _():
        idx_copy(0, 0).start()

    @pl.when(i + 1 < nt)
    def _():
        idx_copy(i + 1, 1 - slot).start()

    idx_copy(i, slot).wait()

    def issue(j, carry):
        row = idx_smem[slot, j]
        pltpu.make_async_copy(yb_ref.at[pl.ds(row, 1)], buf.at[j % TOP_K, pl.ds(j // TOP_K, 1)],
                              sem_rows).start()
        return carry

    lax.fori_loop(0, n_copy, issue, 0, unroll=8)

    def drain(j, carry):
        pltpu.make_async_copy(yb_ref.at[pl.ds(0, 1)], buf.at[0, pl.ds(0, 1)], sem_rows).wait()
        return carry

    lax.fori_loop(0, n_copy, drain, 0, unroll=8)

    moe = None
    for k in range(TOP_K):
        part = rt_ref[:, ROUTE_GATE + k:ROUTE_GATE + k + 1] * buf[k]
        moe = part if moe is None else moe + part
    x2 = x_ref[...] + mod_ref[0, 5:6, :] * moe
    x_out_ref[...] = x2
    if final:
        yn_ref[...] = _rms(x2, fg_ref[...])


def _combine(x, route, dest, yb, mod, group_of_tile, final_g):
    n, d = x.shape
    tm = COMBINE_TOKENS
    nt = n // tm
    final = final_g is not None
    per_group = ROW_TILE // tm
    in_specs = [pl.BlockSpec(memory_space=pl.ANY),
                pl.BlockSpec((tm, d), lambda i: (i, 0)),
                pl.BlockSpec((tm, LANE), lambda i: (i, 0)),
                pl.BlockSpec((1, 6, d), lambda i: (group_of_tile(i // per_group), 0, 0))]
    args = [dest.reshape(nt, tm * TOP_K), x, route, mod]
    out_shape = [jax.ShapeDtypeStruct((n, d), F32)]
    if final:
        in_specs.append(pl.BlockSpec((1, d), lambda i: (0, 0)))
        args.append(final_g)
        out_shape.append(jax.ShapeDtypeStruct((n, d), F32))
    in_specs.append(pl.BlockSpec(memory_space=pl.ANY))
    args.append(yb)
    out_specs = [pl.BlockSpec((tm, d), lambda i: (i, 0)) for _ in out_shape]
    return pl.pallas_call(
        functools.partial(_combine_kernel, final=final),
        grid=(nt,),
        in_specs=in_specs,
        out_specs=out_specs,
        out_shape=out_shape,
        scratch_shapes=[pltpu.SMEM((2, tm * TOP_K), jnp.int32),
                        pltpu.VMEM((TOP_K, tm, d), F32),
                        pltpu.SemaphoreType.DMA((2,)),
                        pltpu.SemaphoreType.DMA(())],
        compiler_params=_cparams("arbitrary"),
        name="moe_combine_final" if final else "moe_combine",
    )(*args)


def _block_plan(counts_f, n_assign):
    counts = counts_f[0, :N_EXPERTS].astype(jnp.int32)
    padded = (counts + MOE_ROWS - 1) // MOE_ROWS * MOE_ROWS
    pad_end = jnp.cumsum(padded)
    pad_start = pad_end - padded
    nb = -(-n_assign // MOE_ROWS) + N_EXPERTS
    first_row = jnp.arange(nb, dtype=jnp.int32) * MOE_ROWS
    blk_e = jnp.minimum(jnp.sum((pad_end[None, :] <= first_row[:, None]).astype(jnp.int32), axis=1),
                        N_EXPERTS - 1)
    n_used = (pad_end[-1] // MOE_ROWS).reshape(1)
    return pad_start, pad_start + counts, pad_end, blk_e, n_used, nb


def _dest_rows(route, pad_start):
    e = route[:, ROUTE_IDX:ROUTE_IDX + TOP_K].astype(jnp.int32)
    rank = route[:, ROUTE_RANK:ROUTE_RANK + TOP_K].astype(jnp.int32)
    hit = e[:, :, None] == jnp.arange(N_EXPERTS, dtype=jnp.int32)[None, None, :]
    return jnp.sum(jnp.where(hit, pad_start[None, None, :], 0), axis=2) + rank


def _half_split_perm(n):
    return np.concatenate([np.arange(0, n, 2), np.arange(1, n, 2)])


def _pad_heads(w, heads, hd):
    k = w.shape[0]
    w = w.reshape(k, heads, hd)
    return jnp.pad(w, ((0, 0), (0, 0), (0, LANE - hd))).reshape(k, heads * LANE)


def _pad_rows(w, heads, hd):
    n = w.shape[1]
    w = w.reshape(heads, hd, n)
    return jnp.pad(w, ((0, 0), (0, LANE - hd), (0, 0))).reshape(heads * LANE, n)


def _rope_tables(t, rot_dim, lane_off):
    tok = jnp.arange(t)
    row = (tok // GRID_W).astype(F32)
    col = (tok % GRID_W).astype(F32)
    nf = rot_dim // 4
    inv = ROPE_THETA ** (-jnp.arange(nf, dtype=F32) / nf)
    ang = jnp.concatenate([row[:, None] * inv, col[:, None] * inv], axis=-1)
    cos, sin = jnp.cos(ang), jnp.sin(ang)
    half = rot_dim // 2
    cos_t = jnp.ones((t, LANE), F32)
    cos_t = cos_t.at[:, lane_off:lane_off + half].set(cos).at[:, lane_off + half:lane_off + rot_dim].set(cos)
    sin_a = jnp.zeros((t, LANE), F32).at[:, lane_off:lane_off + half].set(-sin)
    sin_b = jnp.zeros((t, LANE), F32).at[:, lane_off + half:lane_off + rot_dim].set(sin)
    return jnp.stack([cos_t, sin_a, sin_b])


def _na_bias_table(rpb):
    c = np.arange(GRID_W)[:, None]
    kc = np.arange(GRID_W)[None, :]
    wstart = np.clip(c - NA_KW // 2, 0, GRID_W - NA_KW)
    mask = (kc >= wstart) & (kc < wstart + NA_KW)
    dcol = np.clip(kc - c + NA_KW - 1, 0, 2 * NA_KW - 2)
    d0 = np.arange(NA_KR)[:, None] + np.arange(NA_KR)[None, :]
    tab = rpb[:, d0][:, :, :, dcol]
    tab = jnp.where(mask[None, None, None], tab.astype(F32), NEG_BIG)
    tab = jnp.transpose(tab, (0, 1, 3, 2, 4))
    return tab.reshape(NA_HEADS, NA_KR, GRID_W, NA_KR * GRID_W)


def kernel(x_prompt, x_sample, c, cache_mla_ckv, cache_mla_krope, cache_na_k, cache_na_v, state_mlstm_C, state_mlstm_n, state_mlstm_m, cache_diff_k, cache_diff_v, c_ctx, w_mod, b_mod, norm_g, final_g, a_w_in, a_q_norm, a_w_uq, a_kv_norm, a_w_ukv, a_rpb, a_w_out, b_w_in, b_gate_bias, b_ml_norm, b_df_lambda, b_df_subln, b_w_out, moe_w_router, moe_b_router, moe_w1, moe_b1, moe_w2, moe_b2):
    bp, lp, d = x_prompt.shape
    bs, ts, _ = x_sample.shape
    n_p, n_s = bp * lp, bs * ts
    past = cache_mla_ckv.shape[2]
    tiles_per_batch = ts // ROW_TILE
    grp_p = lambda i: 0
    grp_s = lambda i: 1 + i // tiles_per_batch

    xp = x_prompt.reshape(n_p, d)
    xs = x_sample.reshape(n_s, d)
    cond8 = jnp.concatenate([c_ctx[None, :], c, jnp.zeros((8 - 1 - bs, d), F32)], axis=0)
    mod_all = _modulation(cond8, w_mod, b_mod)

    perm_a = _half_split_perm(MLA_ROPE)
    inv_a = np.argsort(perm_a)
    perm_d = _half_split_perm(DF_QK)
    inv_d = np.argsort(perm_d)

    def moe_layer(l, x1p, x1s, h2p, h2s, rtp, rts, counts, final):
        pad_start, tail_start, pad_end, blk_e, n_used, nb = _block_plan(counts, (n_p + n_s) * TOP_K)
        dest_p = _dest_rows(rtp, pad_start)
        dest_s = _dest_rows(rts, pad_start)
        xs_rows = _dispatch(tail_start, pad_end, n_used, jnp.concatenate([dest_p, dest_s], axis=0),
                            h2p, h2s, nb + 1)
        yb = _expert_ffn(blk_e, n_used, xs_rows, moe_w1[l].astype(BF16), moe_b1[l][:, None, :],
                         moe_w2[l].astype(BF16), moe_b2[l][:, None, :])
        fg = final_g.reshape(1, d) if final else None
        outp = _combine(x1p, rtp, dest_p, yb, mod_all[l], grp_p, fg)
        outs = _combine(x1s, rts, dest_s, yb, mod_all[l], grp_s, fg)
        return outp, outs

    def router_w(l):
        wr = jnp.pad(moe_w_router[l], ((0, 0), (0, LANE - N_EXPERTS)))
        br = jnp.pad(moe_b_router[l], (0, LANE - N_EXPERTS)).reshape(1, LANE)
        return wr, br

    l, e = 0, 0
    w_in = a_w_in[e]
    kr_cols = w_in[:, 384:416][:, perm_a]
    w0 = jnp.concatenate([
        w_in[:, :384], jnp.pad(kr_cols, ((0, 0), (0, LANE - MLA_ROPE))),
        _pad_heads(w_in[:, 416:928], NA_HEADS, NA_HD),
        _pad_heads(w_in[:, 928:1440], NA_HEADS, NA_HD),
        _pad_heads(w_in[:, 1440:1952], NA_HEADS, NA_HD)], axis=1).astype(BF16)
    wuq = a_w_uq[e].reshape(MLA_Q_LORA, MLA_HEADS, MLA_NOPE + MLA_ROPE)
    wuq = jnp.concatenate([wuq[:, :, :MLA_NOPE], wuq[:, :, MLA_NOPE:][:, :, perm_a]], axis=2)
    wuq = _pad_heads(wuq.reshape(MLA_Q_LORA, -1), MLA_HEADS, MLA_NOPE + MLA_ROPE).astype(BF16)
    wukv = a_w_ukv[e].reshape(MLA_KV_LORA, MLA_HEADS, MLA_NOPE + MLA_V)
    wk = _pad_heads(wukv[:, :, :MLA_NOPE].reshape(MLA_KV_LORA, -1), MLA_HEADS, MLA_NOPE).astype(BF16)
    wv = _pad_heads(wukv[:, :, MLA_NOPE:].reshape(MLA_KV_LORA, -1), MLA_HEADS, MLA_V).astype(BF16)
    sel_np = np.zeros((LANE, MLA_HEADS * LANE), np.float32)
    for hd in range(MLA_HEADS):
        sel_np[np.arange(MLA_ROPE), hd * LANE + MLA_NOPE + np.arange(MLA_ROPE)] = 1.0
    sel = jnp.asarray(sel_np, BF16)
    qn_g = a_q_norm[e].reshape(1, -1)
    kvn_g = a_kv_norm[e].reshape(1, -1)
    g1 = norm_g[l, 0].reshape(1, d)
    g2 = norm_g[l, 1].reshape(1, d)
    tabs_a = (_rope_tables(ts, MLA_ROPE, MLA_NOPE), _rope_tables(ts, MLA_ROPE, 0))
    mla_scale = (MLA_NOPE + MLA_ROPE) ** -0.5
    wide = MLA_HEADS * LANE

    q_p, ckv_p, kr_p, nq_p, nk_p, nv_p = _even_proj(xp, g1, mod_all[l], w0, qn_g, wuq, kvn_g, grp_p, None, F32)
    new_mla_ckv = ckv_p.reshape(bp, 1, lp, MLA_KV_LORA)
    new_mla_krope = kr_p[:, :MLA_ROPE][:, inv_a].reshape(bp, 1, lp, MLA_ROPE)
    new_na_k = nk_p.reshape(bp, 1, lp, NA_HEADS, LANE)[..., :NA_HD]
    new_na_v = nv_p.reshape(bp, 1, lp, NA_HEADS, LANE)[..., :NA_HD]
    k_p, v_p = _kv_expand(ckv_p, kr_p, wk, sel, wv)
    ident = lambda hd: hd
    o_mla_p = _flash(q_p.reshape(bp, lp, wide), k_p.reshape(bp, lp, wide), v_p.reshape(bp, lp, wide),
                     scale=mla_scale, v_of_head=ident, out_dtype=BF16)
    o_na_p = _flash(nq_p.reshape(bp, lp, wide), nk_p.astype(BF16).reshape(bp, lp, wide),
                    nv_p.astype(BF16).reshape(bp, lp, wide),
                    scale=NA_HD ** -0.5, v_of_head=ident, out_dtype=BF16)
    q_s, ckv_s, kr_s, nq_s, nk_s, nv_s = _even_proj(xs, g1, mod_all[l], w0, qn_g, wuq, kvn_g, grp_s, tabs_a, BF16)
    ckv_all = jnp.concatenate([cache_mla_ckv[:, e], ckv_s.reshape(bs, ts, LANE)], axis=1)
    kr_cache = jnp.pad(cache_mla_krope[:, e][..., perm_a], ((0, 0), (0, 0), (0, LANE - MLA_ROPE)))
    kr_all = jnp.concatenate([kr_cache, kr_s.reshape(bs, ts, LANE)], axis=1)
    n_all = past + ts
    k_s, v_s = _kv_expand(ckv_all.reshape(bs * n_all, LANE), kr_all.reshape(bs * n_all, LANE), wk, sel, wv)
    o_mla_s = _flash(q_s.reshape(bs, ts, wide), k_s.reshape(bs, n_all, wide), v_s.reshape(bs, n_all, wide),
                     scale=mla_scale, v_of_head=ident, out_dtype=BF16)
    pad_hd = ((0, 0), (0, 0), (0, 0), (0, LANE - NA_HD))
    kc = jnp.pad(cache_na_k[:, e], pad_hd).astype(BF16).reshape(bs, past, wide)
    vc = jnp.pad(cache_na_v[:, e], pad_hd).astype(BF16).reshape(bs, past, wide)
    o_na_s = _na_latent(nq_s.reshape(bs, ts, wide), nk_s.reshape(bs, ts, wide), nv_s.reshape(bs, ts, wide),
                        kc, vc, _na_bias_table(a_rpb[e]))
    w_out = a_w_out[e]
    wa = _pad_rows(w_out[:MLA_HEADS * MLA_V], MLA_HEADS, MLA_V).astype(BF16)
    wb = _pad_rows(w_out[MLA_HEADS * MLA_V:], NA_HEADS, NA_HD).astype(BF16)
    wr, br = router_w(l)
    cnt0 = jnp.zeros((1, LANE), F32)
    x1p, h2p, rtp, cnt_p = _post_mix(_post_even_kernel, xp,
                                     [o_mla_p.reshape(n_p, wide), o_na_p.reshape(n_p, wide)],
                                     [wa, wb], mod_all[l], g2, wr, br, cnt0, grp_p, "post_even")
    x1s, h2s, rts, cnt_s = _post_mix(_post_even_kernel, xs,
                                     [o_mla_s.reshape(n_s, wide), o_na_s.reshape(n_s, wide)],
                                     [wa, wb], mod_all[l], g2, wr, br, cnt_p, grp_s, "post_even")
    (xp,), (xs,) = moe_layer(l, x1p, x1s, h2p, h2s, rtp, rts, cnt_s, False)

    l, o = 1, 0
    lam_init = 0.8 - 0.6 * math.exp(-0.3 * l)
    w_in = b_w_in[o]
    cuts = np.cumsum([0, 256, 256, 512, 512, 16, 512, 512, 512])
    seg = lambda i: w_in[:, cuts[i]:cuts[i + 1]]

    def diff_cols(w):
        w = w.reshape(d, 2 * DF_HEADS, DF_QK)[:, :, perm_d]
        return _pad_heads(w.reshape(d, -1), 2 * DF_HEADS, DF_QK)

    w1p = jnp.concatenate([
        _pad_heads(seg(0), ML_HEADS, ML_DQK), _pad_heads(seg(1), ML_HEADS, ML_DQK), seg(2), seg(3),
        diff_cols(seg(5)), diff_cols(seg(6)), seg(7),
        jnp.pad(seg(4), ((0, 0), (0, LANE - 16)))], axis=1).astype(BF16)
    bg = jnp.pad(b_gate_bias[o].reshape(1, 16), ((0, 0), (0, LANE - 16)))
    g1 = norm_g[l, 0].reshape(1, d)
    g2 = norm_g[l, 1].reshape(1, d)
    tab_d = _rope_tables(ts, DF_QK, 0)
    df_v_of = lambda hd: hd // 2

    mq_p, mk_p, mv_p, mo_p, gt_p, dq_p, dk_p, dv_p = _odd_proj(xp, g1, mod_all[l], w1p, bg, grp_p, None, F32)
    new_diff_k = dk_p.reshape(bp, 1, lp, DF_HEADS, 2, LANE)[..., :DF_QK][..., inv_d]
    new_diff_v = dv_p.reshape(bp, 1, lp, DF_HEADS, DF_V)
    mw = ML_HEADS * LANE
    zc = jnp.zeros((bp, ML_HEADS, LANE, LANE), F32)
    zn = jnp.zeros((bp, ML_HEADS, 1, LANE), F32)
    seq_p = lambda a: a.reshape(bp, lp, -1)
    hf_p, cf_f, nf_f, mf_f = _mlstm_dir(seq_p(mq_p), seq_p(mk_p), seq_p(mv_p), seq_p(gt_p), zc, zn, zn,
                                        direction=0, bb=4)
    hb_p, cf_b, nf_b, mf_b = _mlstm_dir(seq_p(mq_p), seq_p(mk_p), seq_p(mv_p), seq_p(gt_p), zc, zn, zn,
                                        direction=1, bb=4)
    new_mlstm_C = jnp.stack([cf_f, cf_b], axis=1)[..., :ML_DQK][:, None]
    new_mlstm_n = jnp.stack([nf_f, nf_b], axis=1)[:, :, :, 0, :ML_DQK][:, None]
    new_mlstm_m = jnp.stack([mf_f, mf_b], axis=1)[:, :, :, 0, 0][:, None]
    od_p = _flash(seq_p(dq_p), seq_p(dk_p.astype(BF16)), seq_p(dv_p.astype(BF16)),
                  scale=DF_QK ** -0.5, v_of_head=df_v_of, out_dtype=F32)

    mq_s, mk_s, mv_s, mo_s, gt_s, dq_s, dk_s, dv_s = _odd_proj(xs, g1, mod_all[l], w1p, bg, grp_s, tab_d, BF16)
    seq_s = lambda a: a.reshape(bs, ts, -1)
    c0 = jnp.pad(state_mlstm_C[:, o], ((0, 0), (0, 0), (0, 0), (0, 0), (0, LANE - ML_DQK)))
    n0 = jnp.pad(state_mlstm_n[:, o], ((0, 0), (0, 0), (0, 0), (0, LANE - ML_DQK)))[:, :, :, None, :]
    m0 = jnp.broadcast_to(state_mlstm_m[:, o][:, :, :, None, None], (bs, 2, ML_HEADS, 1, LANE))
    hf_s = _mlstm_dir(seq_s(mq_s), seq_s(mk_s), seq_s(mv_s), seq_s(gt_s), c0[:, 0], n0[:, 0], m0[:, 0],
                      direction=0, bb=4)[0]
    hb_s = _mlstm_dir(seq_s(mq_s), seq_s(mk_s), seq_s(mv_s), seq_s(gt_s), c0[:, 1], n0[:, 1], m0[:, 1],
                      direction=1, bb=4)[0]
    dk_cache = cache_diff_k[:, o][..., perm_d]
    dk_cache = jnp.pad(dk_cache, ((0, 0),) * 4 + ((0, LANE - DF_QK),)).astype(BF16).reshape(bs, past, 1024)
    dk_all = jnp.concatenate([dk_cache, seq_s(dk_s)], axis=1)
    dv_all = jnp.concatenate([cache_diff_v[:, o].astype(BF16).reshape(bs, past, 512), seq_s(dv_s)], axis=1)
    od_s = _flash(seq_s(dq_s), dk_all, dv_all, scale=DF_QK ** -0.5, v_of_head=df_v_of, out_dtype=F32)

    w_out = b_w_out[o]
    wa = w_out[:ML_HEADS * ML_DV].astype(BF16)
    wb = w_out[ML_HEADS * ML_DV:].astype(BF16)
    mln = b_ml_norm[o].reshape(1, ML_HEADS * ML_DV)
    sub = b_df_subln[o].reshape(1, DF_V)
    lam_p = jnp.pad(b_df_lambda[o], ((0, 4), (0, LANE - DF_QK)))
    wr, br = router_w(l)
    post_odd = functools.partial(_post_odd_kernel, lam_init=lam_init)
    x1p, h2p, rtp, cnt_p = _post_mix(post_odd, xp, [hf_p.reshape(n_p, mw), hb_p.reshape(n_p, mw), mo_p,
                                                    od_p.reshape(n_p, 1024)],
                                     [mln, sub, lam_p, wa, wb], mod_all[l], g2, wr, br, cnt0, grp_p, "post_odd")
    x1s, h2s, rts, cnt_s = _post_mix(post_odd, xs, [hf_s.reshape(n_s, mw), hb_s.reshape(n_s, mw), mo_s,
                                                    od_s.reshape(n_s, 1024)],
                                     [mln, sub, lam_p, wa, wb], mod_all[l], g2, wr, br, cnt_p, grp_s, "post_odd")
    (_, y_p), (_, y_s) = moe_layer(l, x1p, x1s, h2p, h2s, rtp, rts, cnt_s, True)

    return (y_p.reshape(bp, lp, d), y_s.reshape(bs, ts, d), new_mla_ckv, new_mla_krope, new_na_k, new_na_v,
            new_mlstm_C, new_mlstm_n, new_mlstm_m, new_diff_k, new_diff_v)
```

```python
import functools
import math

import numpy as np
import jax
import jax.numpy as jnp
from jax import lax
from jax.experimental import pallas as pl
from jax.experimental.pallas import tpu as pltpu

F32 = jnp.float32
BF16 = jnp.bfloat16
HIGHEST = lax.Precision.HIGHEST

D_MODEL = 1024
GRID_W = 64
LANE = 128
EPS = 1e-6
ROPE_THETA = 10000.0

MLA_HEADS, MLA_Q_LORA, MLA_KV_LORA, MLA_NOPE, MLA_ROPE, MLA_V = 8, 256, 128, 64, 32, 64
NA_HEADS, NA_HD, NA_KR, NA_KW = 8, 64, 8, 16
ML_HEADS, ML_DQK, ML_DV, ML_CHUNK = 4, 64, 128, 64
DF_HEADS, DF_QK, DF_V = 4, 64, 128
N_EXPERTS, TOP_K, D_FF = 32, 4, 1024
SWIGLU_LIMIT, SWIGLU_ALPHA = 7.0, 1.702

ROW_TILE = 256
MOE_ROWS = 512
ATT_TQ, ATT_TK = 1024, 256
MXU_AHEAD = 2
FF_CHUNK = 512
NEG_BIG = -1e30
VMEM_LIMIT = 56 * 1024 * 1024


def _cparams(*sem):
    return pltpu.CompilerParams(dimension_semantics=sem, vmem_limit_bytes=VMEM_LIMIT)


def _rms(x, g):
    return x * lax.rsqrt(jnp.mean(x * x, axis=-1, keepdims=True) + EPS) * g


def _sigmoid(x):
    return 1.0 / (1.0 + jnp.exp(-x))


def _mod_kernel(c_ref, w_ref, b_ref, o_ref):
    c = c_ref[...]
    s = c * _sigmoid(c)
    o_ref[0] = jnp.dot(s, w_ref[0], precision=HIGHEST, preferred_element_type=F32) + b_ref[0]


def _modulation(cond8, w_mod, b_mod):
    depth, d, n6 = w_mod.shape
    tn = 1024
    out = pl.pallas_call(
        _mod_kernel,
        grid=(depth, n6 // tn),
        in_specs=[
            pl.BlockSpec((8, d), lambda l, j: (0, 0)),
            pl.BlockSpec((1, d, tn), lambda l, j: (l, 0, j)),
            pl.BlockSpec((1, 1, tn), lambda l, j: (l, 0, j)),
        ],
        out_specs=pl.BlockSpec((1, 8, tn), lambda l, j: (l, 0, j)),
        out_shape=jax.ShapeDtypeStruct((depth, 8, n6), F32),
        compiler_params=_cparams("parallel", "parallel"),
        name="modulation",
    )(cond8, w_mod, b_mod.reshape(depth, 1, n6))
    return out.reshape(depth, 8, 6, d)


def _rope_block(x, cos, sin_a, sin_b, half):
    up = pltpu.roll(x, LANE - half, axis=1)
    dn = pltpu.roll(x, half, axis=1)
    return x * cos + up * sin_a + dn * sin_b


def _even_proj_kernel(*refs, rope):
    if rope:
        (x_ref, g_ref, mod_ref, w_ref, qn_ref, wuq_ref, kvn_ref, tq_ref, tk_ref,
         q_ref, ckv_ref, kr_ref, nq_ref, nk_ref, nv_ref) = refs
    else:
        (x_ref, g_ref, mod_ref, w_ref, qn_ref, wuq_ref, kvn_ref,
         q_ref, ckv_ref, kr_ref, nq_ref, nk_ref, nv_ref) = refs
    x = x_ref[...]
    h = _rms(x, g_ref[...]) * (1.0 + mod_ref[0, 1:2, :]) + mod_ref[0, 0:1, :]
    hb = h.astype(BF16)

    def proj(a, b):
        return jnp.dot(hb, w_ref[:, a:b], preferred_element_type=F32)

    q_lat = proj(0, 256)
    qn = _rms(q_lat, qn_ref[...]).astype(BF16)
    for hd in range(MLA_HEADS):
        qh = jnp.dot(qn, wuq_ref[:, hd * LANE:(hd + 1) * LANE], preferred_element_type=F32)
        if rope:
            qh = _rope_block(qh, tq_ref[0], tq_ref[1], tq_ref[2], MLA_ROPE // 2)
        q_ref[:, hd * LANE:(hd + 1) * LANE] = qh.astype(q_ref.dtype)
    ckv_ref[...] = _rms(proj(256, 384), kvn_ref[...])
    kr = proj(384, 512)
    if rope:
        kr = _rope_block(kr, tk_ref[0], tk_ref[1], tk_ref[2], MLA_ROPE // 2)
    kr_ref[...] = kr
    nq_ref[...] = proj(512, 1536).astype(nq_ref.dtype)
    nk_ref[...] = proj(1536, 2560).astype(nk_ref.dtype)
    nv_ref[...] = proj(2560, 3584).astype(nv_ref.dtype)


def _even_proj(x, g, mod, w, qn, wuq, kvn, group_of_tile, rope_tabs, kv_dtype):
    n, d = x.shape
    tm = ROW_TILE
    rope = rope_tabs is not None
    in_specs = [
        pl.BlockSpec((tm, d), lambda i: (i, 0)),
        pl.BlockSpec((1, d), lambda i: (0, 0)),
        pl.BlockSpec((1, 6, d), lambda i: (group_of_tile(i), 0, 0)),
        pl.BlockSpec(w.shape, lambda i: (0, 0)),
        pl.BlockSpec(qn.shape, lambda i: (0, 0)),
        pl.BlockSpec(wuq.shape, lambda i: (0, 0)),
        pl.BlockSpec(kvn.shape, lambda i: (0, 0)),
    ]
    args = [x, g, mod, w, qn, wuq, kvn]
    if rope:
        tq, tk = rope_tabs
        nblk = tq.shape[1] // tm
        in_specs += [pl.BlockSpec((3, tm, LANE), lambda i: (0, i % nblk, 0)),
                     pl.BlockSpec((3, tm, LANE), lambda i: (0, i % nblk, 0))]
        args += [tq, tk]
    wide = MLA_HEADS * LANE
    out_shape = [
        jax.ShapeDtypeStruct((n, wide), BF16),
        jax.ShapeDtypeStruct((n, LANE), F32),
        jax.ShapeDtypeStruct((n, LANE), F32),
        jax.ShapeDtypeStruct((n, wide), BF16),
        jax.ShapeDtypeStruct((n, wide), kv_dtype),
        jax.ShapeDtypeStruct((n, wide), kv_dtype),
    ]
    out_specs = [pl.BlockSpec((tm, s.shape[1]), lambda i: (i, 0)) for s in out_shape]
    return pl.pallas_call(
        functools.partial(_even_proj_kernel, rope=rope),
        grid=(n // tm,),
        in_specs=in_specs,
        out_specs=out_specs,
        out_shape=out_shape,
        compiler_params=_cparams("parallel"),
        name="even_proj_rope" if rope else "even_proj",
    )(*args)


def _kv_expand_kernel(ckv_ref, kr_ref, wk_ref, sel_ref, wv_ref, k_ref, v_ref):
    c = ckv_ref[...].astype(BF16)
    r = kr_ref[...].astype(BF16)
    k = jnp.dot(c, wk_ref[...], preferred_element_type=F32)
    k = k + jnp.dot(r, sel_ref[...], preferred_element_type=F32)
    k_ref[...] = k.astype(BF16)
    v_ref[...] = jnp.dot(c, wv_ref[...], preferred_element_type=F32).astype(BF16)


def _kv_expand(ckv, kr, wk, sel, wv):
    n = ckv.shape[0]
    tm = ROW_TILE
    wide = MLA_HEADS * LANE
    const = lambda i: (0, 0)
    return pl.pallas_call(
        _kv_expand_kernel,
        grid=(n // tm,),
        in_specs=[pl.BlockSpec((tm, LANE), lambda i: (i, 0)),
                  pl.BlockSpec((tm, LANE), lambda i: (i, 0)),
                  pl.BlockSpec(wk.shape, const), pl.BlockSpec(sel.shape, const),
                  pl.BlockSpec(wv.shape, const)],
        out_specs=[pl.BlockSpec((tm, wide), lambda i: (i, 0))] * 2,
        out_shape=[jax.ShapeDtypeStruct((n, wide), BF16)] * 2,
        compiler_params=_cparams("parallel"),
        name="kv_expand",
    )(ckv, kr, wk, sel, wv)


def _flash_kernel(q_ref, k_ref, v_ref, o_ref, m_sc, l_sc, acc_sc, *, heads, v_of_head, scale):
    kj = pl.program_id(2)
    c2 = scale * math.log2(math.e)

    @pl.when(kj == 0)
    def _():
        m_sc[...] = jnp.full(m_sc.shape, NEG_BIG, F32)
        l_sc[...] = jnp.zeros(l_sc.shape, F32)
        acc_sc[...] = jnp.zeros(acc_sc.shape, F32)

    def scores(hd):
        sl = slice(hd * LANE, (hd + 1) * LANE)
        return lax.dot_general(k_ref[0, :, sl], q_ref[0, :, sl], (((1,), (1,)), ((), ())),
                               preferred_element_type=F32)

    pending = [scores(hd) for hd in range(min(MXU_AHEAD, heads))]
    for hd in range(heads):
        sl = slice(hd * LANE, (hd + 1) * LANE)
        vh = v_of_head(hd)
        s_t = pending.pop(0)
        if hd + MXU_AHEAD < heads:
            pending.append(scores(hd + MXU_AHEAD))
        m_prev = m_sc[hd]
        m_new = jnp.maximum(m_prev, jnp.max(s_t, axis=0, keepdims=True))
        alpha = jnp.exp2((m_prev - m_new) * c2)
        p_t = jnp.exp2((s_t - m_new) * c2)
        l_sc[hd] = alpha * l_sc[hd] + jnp.sum(p_t, axis=0, keepdims=True)
        pv_t = lax.dot_general(v_ref[0, :, vh * LANE:(vh + 1) * LANE], p_t.astype(BF16),
                               (((0,), (0,)), ((), ())), preferred_element_type=F32)
        acc_sc[sl, :] = alpha * acc_sc[sl, :] + pv_t
        m_sc[hd] = m_new

    @pl.when(kj == pl.num_programs(2) - 1)
    def _():
        for hd in range(heads):
            sl = slice(hd * LANE, (hd + 1) * LANE)
            o_ref[0, :, sl] = (acc_sc[sl, :] / l_sc[hd]).T.astype(o_ref.dtype)


def _flash(q, k, v, *, scale, v_of_head, out_dtype):
    b, s, wq = q.shape
    nk = k.shape[1]
    wv = v.shape[2]
    heads = wq // LANE
    tq = min(ATT_TQ, s)
    tk = min(ATT_TK, nk)
    return pl.pallas_call(
        functools.partial(_flash_kernel, heads=heads, v_of_head=v_of_head, scale=scale),
        grid=(b, s // tq, nk // tk),
        in_specs=[pl.BlockSpec((1, tq, wq), lambda bi, qi, ki: (bi, qi, 0)),
                  pl.BlockSpec((1, tk, wq), lambda bi, qi, ki: (bi, ki, 0)),
                  pl.BlockSpec((1, tk, wv), lambda bi, qi, ki: (bi, ki, 0))],
        out_specs=pl.BlockSpec((1, tq, wq), lambda bi, qi, ki: (bi, qi, 0)),
        out_shape=jax.ShapeDtypeStruct((b, s, wq), out_dtype),
        scratch_shapes=[pltpu.VMEM((heads, 1, tq), F32), pltpu.VMEM((heads, 1, tq), F32),
                        pltpu.VMEM((wq, tq), F32)],
        compiler_params=_cparams("parallel", "parallel", "arbitrary"),
        name="flash_attention",
    )(q, k, v)


def _na_kernel(q_ref, k_ref, v_ref, kc_ref, vc_ref, bias_ref, o_ref, *, rows, scale):
    r = pl.program_id(1)
    r0 = jnp.clip(r - NA_KR // 2, 0, rows - NA_KR)
    start = pl.multiple_of(r0 * GRID_W, GRID_W)
    nloc = NA_KR * GRID_W
    dn = (((1,), (1,)), ((), ()))

    def scores(hd):
        sl = slice(hd * LANE, (hd + 1) * LANE)
        qh = q_ref[0, :, sl]
        kh = k_ref[0, pl.ds(start, nloc), sl]
        return (lax.dot_general(qh, kh, dn, preferred_element_type=F32),
                lax.dot_general(qh, kc_ref[0, :, sl], dn, preferred_element_type=F32))

    ahead = 2 * MXU_AHEAD
    pending = [scores(hd) for hd in range(min(ahead, NA_HEADS))]
    for hd in range(NA_HEADS):
        sl = slice(hd * LANE, (hd + 1) * LANE)
        s_loc, s_ctx = pending.pop(0)
        if hd + ahead < NA_HEADS:
            pending.append(scores(hd + ahead))
        vh = v_ref[0, pl.ds(start, nloc), sl]
        s_loc = s_loc * scale + bias_ref[hd, 0]
        s_ctx = s_ctx * scale
        m = jnp.maximum(jnp.max(s_loc, axis=1, keepdims=True), jnp.max(s_ctx, axis=1, keepdims=True))
        p_loc = jnp.exp(s_loc - m)
        p_ctx = jnp.exp(s_ctx - m)
        l = jnp.sum(p_loc, axis=1, keepdims=True) + jnp.sum(p_ctx, axis=1, keepdims=True)
        o = (jnp.dot(p_loc.astype(BF16), vh, preferred_element_type=F32)
             + jnp.dot(p_ctx.astype(BF16), vc_ref[0, :, sl], preferred_element_type=F32))
        o_ref[0, :, sl] = (o / l).astype(o_ref.dtype)


def _na_latent(q, k, v, kc, vc, bias_tab):
    b, s, wide = q.shape
    rows = s // GRID_W
    lc = kc.shape[1]

    def bias_idx(bi, r):
        return (0, jnp.clip(r - NA_KR // 2, 0, rows - NA_KR) - r + NA_KR - 1, 0, 0)

    return pl.pallas_call(
        functools.partial(_na_kernel, rows=rows, scale=NA_HD ** -0.5),
        grid=(b, rows),
        in_specs=[pl.BlockSpec((1, GRID_W, wide), lambda bi, r: (bi, r, 0)),
                  pl.BlockSpec((1, s, wide), lambda bi, r: (bi, 0, 0)),
                  pl.BlockSpec((1, s, wide), lambda bi, r: (bi, 0, 0)),
                  pl.BlockSpec((1, lc, wide), lambda bi, r: (bi, 0, 0)),
                  pl.BlockSpec((1, lc, wide), lambda bi, r: (bi, 0, 0)),
                  pl.BlockSpec((NA_HEADS, 1, GRID_W, NA_KR * GRID_W), bias_idx)],
        out_specs=pl.BlockSpec((1, GRID_W, wide), lambda bi, r: (bi, r, 0)),
        out_shape=jax.ShapeDtypeStruct((b, s, wide), BF16),
        compiler_params=_cparams("parallel", "arbitrary"),
        name="na_latent",
    )(q, k, v, kc, vc, bias_tab)


def _odd_proj_kernel(*refs, rope):
    if rope:
        (x_ref, g_ref, mod_ref, w_ref, bg_ref, td_ref,
         mq_ref, mk_ref, mv_ref, mo_ref, gt_ref, dq_ref, dk_ref, dv_ref) = refs
    else:
        (x_ref, g_ref, mod_ref, w_ref, bg_ref,
         mq_ref, mk_ref, mv_ref, mo_ref, gt_ref, dq_ref, dk_ref, dv_ref) = refs
    x = x_ref[...]
    h = _rms(x, g_ref[...]) * (1.0 + mod_ref[0, 1:2, :]) + mod_ref[0, 0:1, :]
    hb = h.astype(BF16)

    def proj(a, b):
        return jnp.dot(hb, w_ref[:, a:b], preferred_element_type=F32)

    mq_ref[...] = proj(0, 512).astype(mq_ref.dtype)
    mk_ref[...] = (proj(512, 1024) * (ML_DQK ** -0.5)).astype(mk_ref.dtype)
    mv_ref[...] = proj(1024, 1536).astype(mv_ref.dtype)
    mo_ref[...] = proj(1536, 2048)
    for blk in range(8):
        qd = proj(2048 + blk * LANE, 2048 + (blk + 1) * LANE)
        kd = proj(3072 + blk * LANE, 3072 + (blk + 1) * LANE)
        if rope:
            qd = _rope_block(qd, td_ref[0], td_ref[1], td_ref[2], DF_QK // 2)
            kd = _rope_block(kd, td_ref[0], td_ref[1], td_ref[2], DF_QK // 2)
        dq_ref[:, blk * LANE:(blk + 1) * LANE] = qd.astype(dq_ref.dtype)
        dk_ref[:, blk * LANE:(blk + 1) * LANE] = kd.astype(dk_ref.dtype)
    dv_ref[...] = proj(4096, 4608).astype(dv_ref.dtype)
    gt = proj(4608, 4736) + bg_ref[...]
    lane = lax.broadcasted_iota(jnp.int32, gt.shape, 1)
    log_f = jnp.minimum(gt, 0.0) - jnp.log(1.0 + jnp.exp(-jnp.abs(gt)))
    gt_ref[...] = jnp.where((lane % 8) >= 4, log_f, gt)


def _odd_proj(x, g, mod, w, bg, group_of_tile, rope_tab, kv_dtype):
    n, d = x.shape
    tm = ROW_TILE
    rope = rope_tab is not None
    in_specs = [
        pl.BlockSpec((tm, d), lambda i: (i, 0)),
        pl.BlockSpec((1, d), lambda i: (0, 0)),
        pl.BlockSpec((1, 6, d), lambda i: (group_of_tile(i), 0, 0)),
        pl.BlockSpec(w.shape, lambda i: (0, 0)),
        pl.BlockSpec(bg.shape, lambda i: (0, 0)),
    ]
    args = [x, g, mod, w, bg]
    if rope:
        nblk = rope_tab.shape[1] // tm
        in_specs.append(pl.BlockSpec((3, tm, LANE), lambda i: (0, i % nblk, 0)))
        args.append(rope_tab)
    out_shape = [
        jax.ShapeDtypeStruct((n, 512), BF16),
        jax.ShapeDtypeStruct((n, 512), BF16),
        jax.ShapeDtypeStruct((n, 512), BF16),
        jax.ShapeDtypeStruct((n, 512), F32),
        jax.ShapeDtypeStruct((n, LANE), F32),
        jax.ShapeDtypeStruct((n, 1024), BF16),
        jax.ShapeDtypeStruct((n, 1024), kv_dtype),
        jax.ShapeDtypeStruct((n, 512), kv_dtype),
    ]
    out_specs = [pl.BlockSpec((tm, s.shape[1]), lambda i: (i, 0)) for s in out_shape]
    return pl.pallas_call(
        functools.partial(_odd_proj_kernel, rope=rope),
        grid=(n // tm,),
        in_specs=in_specs,
        out_specs=out_specs,
        out_shape=out_shape,
        compiler_params=_cparams("parallel"),
        name="odd_proj_rope" if rope else "odd_proj",
    )(*args)


def _mlstm_kernel(q_ref, k_ref, v_ref, g_ref, c0_ref, n0_ref, m0_ref,
                  h_ref, cf_ref, nf_ref, mf_ref, c_sc, n_sc, m_sc, *, bb, direction, reverse):
    step = pl.program_id(1)
    L = ML_CHUNK

    @pl.when(step == 0)
    def _():
        c_sc[...] = c0_ref[...]
        n_sc[...] = n0_ref[...]
        m_sc[...] = m0_ref[...]

    t_idx = lax.broadcasted_iota(jnp.int32, (L, L), 0)
    s_idx = lax.broadcasted_iota(jnp.int32, (L, L), 1)
    keep = (s_idx >= t_idx) if reverse else (s_idx <= t_idx)
    tri = keep.astype(F32)
    last = 0 if reverse else L - 1
    for bi in range(bb):
        g = g_ref[bi]
        bcum = jnp.dot(tri, g, precision=HIGHEST, preferred_element_type=F32)
        g_t = g.T
        b_t = bcum.T
        for hd in range(ML_HEADS):
            sl = slice(hd * LANE, (hd + 1) * LANE)
            ci = direction * 8 + hd
            cf = direction * 8 + 4 + hd
            li_row = g_t[ci:ci + 1, :]
            b_row = b_t[cf:cf + 1, :]
            li_col = g[:, ci:ci + 1]
            b_col = bcum[:, cf:cf + 1]
            m_old = m_sc[bi, hd][:, 0:1]
            n_old = n_sc[bi, hd]
            c_old = c_sc[bi, hd]
            qh = q_ref[bi, :, sl]
            kh = k_ref[bi, :, sl]
            vh = v_ref[bi, :, sl]
            dm = jnp.where(keep, b_col - b_row + li_row, NEG_BIG)
            m_inter = b_col + m_old
            m_t = jnp.maximum(m_inter, jnp.max(dm, axis=1, keepdims=True))
            qk = lax.dot_general(qh, kh, (((1,), (1,)), ((), ())), preferred_element_type=F32)
            w_intra = jnp.where(keep, jnp.exp(dm - m_t), 0.0) * qk
            w_inter = jnp.exp(m_inter - m_t)
            qc = lax.dot_general(qh, c_old.astype(BF16), (((1,), (1,)), ((), ())),
                                 preferred_element_type=F32)
            num = jnp.dot(w_intra.astype(BF16), vh, preferred_element_type=F32) + w_inter * qc
            qn = jnp.sum(qh.astype(F32) * n_old, axis=1, keepdims=True)
            den = jnp.sum(w_intra, axis=1, keepdims=True) + w_inter * qn
            h_ref[bi, :, sl] = num / jnp.maximum(jnp.abs(den), jnp.exp(-m_t))
            m_new = m_t[last:last + 1, :]
            b_last = b_col[last:last + 1, :]
            w_end = jnp.exp(b_last - b_col + li_col - m_new)
            decay = jnp.exp(b_last + m_old - m_new)
            wv = (w_end * vh.astype(F32)).astype(BF16)
            upd = lax.dot_general(wv, kh, (((0,), (0,)), ((), ())), preferred_element_type=F32)
            c_sc[bi, hd] = decay * c_old + upd
            n_sc[bi, hd] = decay * n_old + jnp.sum(w_end * kh.astype(F32), axis=0, keepdims=True)
            m_sc[bi, hd] = jnp.broadcast_to(m_new, (1, LANE))

    @pl.when(step == pl.num_programs(1) - 1)
    def _():
        cf_ref[...] = c_sc[...]
        nf_ref[...] = n_sc[...]
        mf_ref[...] = m_sc[...]


def _mlstm_dir(q, k, v, gates, c0, n0, m0, *, direction, bb):
    b, t, w = q.shape
    nc = t // ML_CHUNK
    reverse = direction == 1
    if reverse:
        cidx = lambda bi, c: (bi, nc - 1 - c, 0)
    else:
        cidx = lambda bi, c: (bi, c, 0)
    st4 = lambda bi, c: (bi, 0, 0, 0)
    seq = pl.BlockSpec((bb, ML_CHUNK, w), cidx)
    c_spec = pl.BlockSpec((bb, ML_HEADS, LANE, LANE), st4)
    n_spec = pl.BlockSpec((bb, ML_HEADS, 1, LANE), st4)
    return pl.pallas_call(
        functools.partial(_mlstm_kernel, bb=bb, direction=direction, reverse=reverse),
        grid=(b // bb, nc),
        in_specs=[seq, seq, seq, pl.BlockSpec((bb, ML_CHUNK, LANE), cidx), c_spec, n_spec, n_spec],
        out_specs=[seq, c_spec, n_spec, n_spec],
        out_shape=[jax.ShapeDtypeStruct((b, t, w), F32),
                   jax.ShapeDtypeStruct(c0.shape, F32),
                   jax.ShapeDtypeStruct(n0.shape, F32),
                   jax.ShapeDtypeStruct(m0.shape, F32)],
        scratch_shapes=[pltpu.VMEM((bb, ML_HEADS, LANE, LANE), F32),
                        pltpu.VMEM((bb, ML_HEADS, 1, LANE), F32),
                        pltpu.VMEM((bb, ML_HEADS, 1, LANE), F32)],
        compiler_params=_cparams("parallel", "arbitrary"),
        name="mlstm_bwd" if reverse else "mlstm_fwd",
    )(q, k, v, gates, c0, n0, m0)


ROUTE_IDX, ROUTE_ROW, ROUTE_GATE = 0, 4, 8
TAB_LEN, TAB_OFF, TAB_BASE = 0, 1, 2
SUBLANES = 8


def _post_tail(x_ref, mix, mod_ref, g2_ref, wr_ref, br_ref, cin_ref,
               x_out_ref, h_ref, rt_ref, tab_ref, cout_ref, cnt_sc):
    @pl.when(pl.program_id(0) == 0)
    def _():
        cnt_sc[...] = cin_ref[...]

    x1 = x_ref[...] + mod_ref[0, 2:3, :] * mix
    x_out_ref[...] = x1
    h2 = _rms(x1, g2_ref[...]) * (1.0 + mod_ref[0, 4:5, :]) + mod_ref[0, 3:4, :]
    h_ref[...] = h2
    logits = jnp.dot(h2, wr_ref[...], precision=HIGHEST, preferred_element_type=F32) + br_ref[...]
    tm = logits.shape[0]
    lane = lax.broadcasted_iota(jnp.int32, (tm, LANE), 1)
    lg = jnp.where(lane < N_EXPERTS, logits, NEG_BIG)
    vals, idxs, hots = [], [], []
    for _ in range(TOP_K):
        mk = jnp.max(lg, axis=1, keepdims=True)
        idx = jnp.min(jnp.where(lg == mk, lane, LANE), axis=1, keepdims=True)
        hot = lane == idx
        vals.append(mk)
        idxs.append(idx)
        hots.append(hot)
        lg = jnp.where(hot, NEG_BIG, lg)
    ex = [jnp.exp(v - vals[0]) for v in vals]
    tot = ex[0] + ex[1] + ex[2] + ex[3]
    sel = jnp.zeros((tm, LANE), F32)
    for hot in hots:
        sel = sel + hot.astype(F32)
    t_idx = lax.broadcasted_iota(jnp.int32, (tm, tm), 0)
    s_idx = lax.broadcasted_iota(jnp.int32, (tm, tm), 1)
    earlier = (s_idx < t_idx).astype(BF16)
    prefix = jnp.dot(earlier, sel.astype(BF16), preferred_element_type=F32)
    run_len = jnp.floor((jnp.sum(sel, axis=0, keepdims=True) + (SUBLANES - 1)) / SUBLANES) * SUBLANES
    e_src = lax.broadcasted_iota(jnp.int32, (LANE, LANE), 0)
    e_dst = lax.broadcasted_iota(jnp.int32, (LANE, LANE), 1)
    before = (e_src < e_dst).astype(F32)
    run_off = jnp.dot(jnp.broadcast_to(run_len, (SUBLANES, LANE)), before, precision=HIGHEST,
                      preferred_element_type=F32)[0:1]
    local = prefix + run_off
    route = jnp.zeros((tm, LANE), F32)
    for k in range(TOP_K):
        row = jnp.sum(jnp.where(hots[k], local, 0.0), axis=1, keepdims=True)
        route = jnp.where(lane == ROUTE_IDX + k, idxs[k].astype(F32), route)
        route = jnp.where(lane == ROUTE_ROW + k, row, route)
        route = jnp.where(lane == ROUTE_GATE + k, ex[k] / tot, route)
    rt_ref[...] = route
    sub = lax.broadcasted_iota(jnp.int32, (SUBLANES, LANE), 0)
    tab = jnp.where(sub == TAB_LEN, run_len, 0.0)
    tab = jnp.where(sub == TAB_OFF, run_off, tab)
    tab = jnp.where(sub == TAB_BASE, cnt_sc[...], tab)
    tab_ref[0] = tab
    cnt_new = cnt_sc[...] + run_len
    cnt_sc[...] = cnt_new
    cout_ref[...] = cnt_new


def _post_even_kernel(x_ref, a_ref, b_ref, wa_ref, wb_ref, *rest):
    mix = (jnp.dot(a_ref[...], wa_ref[...], preferred_element_type=F32)
           + jnp.dot(b_ref[...], wb_ref[...], preferred_element_type=F32))
    _post_tail(x_ref, mix, *rest)


def _post_odd_kernel(x_ref, hf_ref, hb_ref, mo_ref, od_ref, mln_ref, sub_ref, lam_ref,
                     wa_ref, wb_ref, *rest, lam_init):
    lp = lam_ref[...]
    lam = (jnp.exp(jnp.sum(lp[0:1] * lp[1:2], axis=1, keepdims=True))
           - jnp.exp(jnp.sum(lp[2:3] * lp[3:4], axis=1, keepdims=True)) + lam_init)
    mix = None
    for hd in range(ML_HEADS):
        sl = slice(hd * LANE, (hd + 1) * LANE)
        hm = hf_ref[:, sl] + hb_ref[:, sl]
        om = _rms(hm, mln_ref[:, sl]) * _sigmoid(mo_ref[:, sl])
        part = jnp.dot(om.astype(BF16), wa_ref[sl, :], preferred_element_type=F32)
        mix = part if mix is None else mix + part
    for hd in range(DF_HEADS):
        o1 = od_ref[:, (2 * hd) * LANE:(2 * hd + 1) * LANE]
        o2 = od_ref[:, (2 * hd + 1) * LANE:(2 * hd + 2) * LANE]
        od = _rms(o1 - lam * o2, sub_ref[...]) * (1.0 - lam_init)
        mix = mix + jnp.dot(od.astype(BF16), wb_ref[hd * LANE:(hd + 1) * LANE, :],
                            preferred_element_type=F32)
    _post_tail(x_ref, mix, *rest)


def _post_mix(kernel, x, row_inputs, const_inputs, mod, g2, wr, br, cnt_in, group_of_tile, name):
    n, d = x.shape
    tm = ROW_TILE
    row = lambda a: pl.BlockSpec((tm, a.shape[1]), lambda i: (i, 0))
    const = lambda a: pl.BlockSpec(a.shape, lambda i: (0,) * a.ndim)
    in_specs = ([row(x)] + [row(a) for a in row_inputs] + [const(a) for a in const_inputs]
                + [pl.BlockSpec((1, 6, d), lambda i: (group_of_tile(i), 0, 0)),
                   const(g2), const(wr), const(br), const(cnt_in)])
    return pl.pallas_call(
        kernel,
        grid=(n // tm,),
        in_specs=in_specs,
        out_specs=[pl.BlockSpec((tm, d), lambda i: (i, 0)),
                   pl.BlockSpec((tm, d), lambda i: (i, 0)),
                   pl.BlockSpec((tm, LANE), lambda i: (i, 0)),
                   pl.BlockSpec((1, SUBLANES, LANE), lambda i: (i, 0, 0)),
                   pl.BlockSpec((1, LANE), lambda i: (0, 0))],
        out_shape=[jax.ShapeDtypeStruct((n, d), F32), jax.ShapeDtypeStruct((n, d), F32),
                   jax.ShapeDtypeStruct((n, LANE), F32),
                   jax.ShapeDtypeStruct((n // tm, SUBLANES, LANE), F32),
                   jax.ShapeDtypeStruct((1, LANE), F32)],
        scratch_shapes=[pltpu.VMEM((1, LANE), F32)],
        compiler_params=_cparams("arbitrary"),
        name=name,
    )(x, *row_inputs, *const_inputs, mod, g2, wr, br, cnt_in)


LOCAL_ROWS = ROW_TILE * TOP_K + N_EXPERTS * SUBLANES
TAB_WORDS = 128


def _split_bf16(x):
    hi = x.astype(BF16)
    return hi, (x - hi.astype(F32)).astype(BF16)


def _run_copies(tab_smem, slot, make_copy, start):
    def per_expert(e, carry):
        n_chunks = tab_smem[slot, e]
        off = tab_smem[slot, N_EXPERTS + e]
        base = tab_smem[slot, 2 * N_EXPERTS + e]

        def one(j, c):
            cp = make_copy(pl.multiple_of(off + j * SUBLANES, SUBLANES),
                           pl.multiple_of(base + j * SUBLANES, SUBLANES))
            if start:
                cp.start()
            else:
                cp.wait()
            return c

        return lax.fori_loop(0, n_chunks, one, carry)

    lax.fori_loop(0, N_EXPERTS, per_expert, 0)


def _dispatch_kernel(tail_ref, pend_ref, nu_ref, tab_hbm, hp_ref, hs_ref, rp_ref, rs_ref, xs_ref,
                     tab_smem, local, zero_buf, sem_tab, sem_rows, sem_zero, *, tiles_p, n_blocks):
    i = pl.program_id(0)
    nt = pl.num_programs(0)
    slot = i % 2
    d = hp_ref.shape[1]

    def tab_copy(step, sl):
        return pltpu.make_async_copy(tab_hbm.at[step], tab_smem.at[sl], sem_tab.at[sl])

    def zero_block(start):
        return pltpu.make_async_copy(zero_buf, xs_ref.at[pl.ds(start, MOE_ROWS)], sem_zero)

    def zero_chunk(start):
        return pltpu.make_async_copy(zero_buf.at[pl.ds(0, SUBLANES)], xs_ref.at[pl.ds(start, SUBLANES)],
                                     sem_rows)

    @pl.when(i == 0)
    def _():
        tab_copy(0, 0).start()
        zero_buf[...] = jnp.zeros(zero_buf.shape, F32)

        def tails(start):
            def per_expert(e, carry):
                def one(j, c):
                    cp = zero_chunk(pl.multiple_of(tail_ref[e] + j * SUBLANES, SUBLANES))
                    if start:
                        cp.start()
                    else:
                        cp.wait()
                    return c
                return lax.fori_loop(0, (pend_ref[e] - tail_ref[e]) // SUBLANES, one, carry)
            lax.fori_loop(0, N_EXPERTS, per_expert, 0)

        def blocks(start):
            def one(b, c):
                cp = zero_block(pl.multiple_of(b * MOE_ROWS, MOE_ROWS))
                if start:
                    cp.start()
                else:
                    cp.wait()
                return c
            lax.fori_loop(nu_ref[0], n_blocks, one, 0)

        tails(True)
        blocks(True)
        tails(False)
        blocks(False)

    @pl.when(i + 1 < nt)
    def _():
        tab_copy(i + 1, 1 - slot).start()

    def sort_tile(h_ref, rt_ref):
        rt_t = rt_ref[...].T
        row_id = lax.broadcasted_iota(jnp.int32, (LOCAL_ROWS, ROW_TILE), 0).astype(F32)
        pick = jnp.zeros((LOCAL_ROWS, ROW_TILE), F32)
        gate = jnp.zeros((LOCAL_ROWS, ROW_TILE), F32)
        for k in range(TOP_K):
            hit = row_id == rt_t[ROUTE_ROW + k:ROUTE_ROW + k + 1, :]
            pick = jnp.where(hit, 1.0, pick)
            gate = jnp.where(hit, rt_t[ROUTE_GATE + k:ROUTE_GATE + k + 1, :], gate)
        local[:, 0:d] = jnp.dot(pick.astype(BF16), h_ref[...].astype(BF16), preferred_element_type=F32)
        g_hi, g_lo = _split_bf16(gate)
        ones = jnp.ones((ROW_TILE, LANE), BF16)
        local[:, d:d + LANE] = (jnp.dot(g_hi, ones, preferred_element_type=F32)
                                + jnp.dot(g_lo, ones, preferred_element_type=F32))

    @pl.when(i < tiles_p)
    def _():
        sort_tile(hp_ref, rp_ref)

    @pl.when(i >= tiles_p)
    def _():
        sort_tile(hs_ref, rs_ref)

    tab_copy(i, slot).wait()

    def run_copy(off, base):
        return pltpu.make_async_copy(local.at[pl.ds(off, SUBLANES)], xs_ref.at[pl.ds(base, SUBLANES)], sem_rows)

    _run_copies(tab_smem, slot, run_copy, True)
    _run_copies(tab_smem, slot, run_copy, False)


def _dispatch(tail_start, pad_end, n_used, tab, h2p, h2s, rtp, rts, n_blocks):
    d = h2p.shape[1]
    tm = ROW_TILE
    tiles_p = h2p.shape[0] // tm
    nt = tiles_p + h2s.shape[0] // tm
    idx_p = lambda i, *_: (jnp.minimum(i, tiles_p - 1), 0)
    idx_s = lambda i, *_: (jnp.maximum(i - tiles_p, 0), 0)
    grid_spec = pltpu.PrefetchScalarGridSpec(
        num_scalar_prefetch=3,
        grid=(nt,),
        in_specs=[pl.BlockSpec(memory_space=pl.ANY),
                  pl.BlockSpec((tm, d), idx_p), pl.BlockSpec((tm, d), idx_s),
                  pl.BlockSpec((tm, LANE), idx_p), pl.BlockSpec((tm, LANE), idx_s)],
        out_specs=pl.BlockSpec(memory_space=pl.ANY),
        scratch_shapes=[pltpu.SMEM((2, TAB_WORDS), jnp.int32),
                        pltpu.VMEM((LOCAL_ROWS, d + LANE), F32),
                        pltpu.VMEM((MOE_ROWS, d + LANE), F32),
                        pltpu.SemaphoreType.DMA((2,)),
                        pltpu.SemaphoreType.DMA(()),
                        pltpu.SemaphoreType.DMA(())],
    )
    return pl.pallas_call(
        functools.partial(_dispatch_kernel, tiles_p=tiles_p, n_blocks=n_blocks),
        grid_spec=grid_spec,
        out_shape=jax.ShapeDtypeStruct((n_blocks * MOE_ROWS, d + LANE), F32),
        compiler_params=_cparams("arbitrary"),
        name="moe_dispatch",
    )(tail_start, pad_end, n_used, tab, h2p, h2s, rtp, rts)


def _expert_kernel(be_ref, nu_ref, x_ref, w1_ref, b1_ref, w2_ref, b2_ref, o_ref):
    i = pl.program_id(0)
    d = o_ref.shape[1]

    @pl.when(i < nu_ref[0])
    def _():
        x = x_ref[:, 0:d].astype(BF16)
        gate = x_ref[:, d:d + 1]
        acc = None
        fc = FF_CHUNK
        for c in range(D_FF // fc):
            glu = jnp.dot(x, w1_ref[0, :, c * fc:(c + 1) * fc], preferred_element_type=F32)
            glu = glu + b1_ref[0, :, c * fc:(c + 1) * fc]
            lin = jnp.dot(x, w1_ref[0, :, D_FF + c * fc:D_FF + (c + 1) * fc],
                          preferred_element_type=F32)
            lin = lin + b1_ref[0, :, D_FF + c * fc:D_FF + (c + 1) * fc]
            glu = jnp.minimum(glu, SWIGLU_LIMIT)
            lin = jnp.clip(lin, -SWIGLU_LIMIT, SWIGLU_LIMIT)
            act = glu * _sigmoid(SWIGLU_ALPHA * glu) * (lin + 1.0)
            part = jnp.dot(act.astype(BF16), w2_ref[0, c * fc:(c + 1) * fc, :],
                           preferred_element_type=F32)
            acc = part if acc is None else acc + part
        o_ref[...] = (acc + b2_ref[0]) * gate

    @pl.when(i >= nu_ref[0])
    def _():
        o_ref[...] = jnp.zeros(o_ref.shape, F32)


def _expert_ffn(blk_e, n_used, xs, w1, b1, w2, b2):
    d = w2.shape[2]
    nb = blk_e.shape[0]
    grid_spec = pltpu.PrefetchScalarGridSpec(
        num_scalar_prefetch=2,
        grid=(nb,),
        in_specs=[pl.BlockSpec((MOE_ROWS, xs.shape[1]), lambda i, be, nu: (i, 0)),
                  pl.BlockSpec((1, d, 2 * D_FF), lambda i, be, nu: (be[i], 0, 0)),
                  pl.BlockSpec((1, 1, 2 * D_FF), lambda i, be, nu: (be[i], 0, 0)),
                  pl.BlockSpec((1, D_FF, d), lambda i, be, nu: (be[i], 0, 0)),
                  pl.BlockSpec((1, 1, d), lambda i, be, nu: (be[i], 0, 0))],
        out_specs=pl.BlockSpec((MOE_ROWS, d), lambda i, be, nu: (i, 0)),
    )
    return pl.pallas_call(
        _expert_kernel,
        grid_spec=grid_spec,
        out_shape=jax.ShapeDtypeStruct((nb * MOE_ROWS, d), F32),
        compiler_params=_cparams("arbitrary"),
        name="expert_ffn",
    )(blk_e, n_used, xs, w1, b1, w2, b2)


def _combine_kernel(*refs, final):
    if final:
        (tab_hbm, x_ref, rt_ref, mod_ref, fg_ref, yb_ref, x_out_ref, yn_ref,
         tab_smem, local, sem_tab, sem_rows) = refs
    else:
        (tab_hbm, x_ref, rt_ref, mod_ref, yb_ref, x_out_ref,
         tab_smem, local, sem_tab, sem_rows) = refs
    i = pl.program_id(0)
    nt = pl.num_programs(0)
    slot = i % 2

    def tab_copy(step, sl):
        return pltpu.make_async_copy(tab_hbm.at[step], tab_smem.at[sl], sem_tab.at[sl])

    @pl.when(i == 0)
    def _():
        tab_copy(0, 0).start()
        local[...] = jnp.zeros(local.shape, F32)

    @pl.when(i + 1 < nt)
    def _():
        tab_copy(i + 1, 1 - slot).start()

    tab_copy(i, slot).wait()

    def run_copy(off, base):
        return pltpu.make_async_copy(yb_ref.at[pl.ds(base, SUBLANES)], local.at[pl.ds(off, SUBLANES)], sem_rows)

    _run_copies(tab_smem, slot, run_copy, True)
    _run_copies(tab_smem, slot, run_copy, False)

    rt = rt_ref[...]
    col_id = lax.broadcasted_iota(jnp.int32, (ROW_TILE, LOCAL_ROWS), 1).astype(F32)
    pick = jnp.zeros((ROW_TILE, LOCAL_ROWS), F32)
    for k in range(TOP_K):
        pick = jnp.where(col_id == rt[:, ROUTE_ROW + k:ROUTE_ROW + k + 1], 1.0, pick)
    pick = pick.astype(BF16)
    y_hi, y_lo = _split_bf16(local[...])
    moe = (jnp.dot(pick, y_hi, preferred_element_type=F32)
           + jnp.dot(pick, y_lo, preferred_element_type=F32))
    x2 = x_ref[...] + mod_ref[0, 5:6, :] * moe
    x_out_ref[...] = x2
    if final:
        yn_ref[...] = _rms(x2, fg_ref[...])


def _combine(x, route, tab, yb, mod, group_of_tile, final_g):
    n, d = x.shape
    tm = ROW_TILE
    nt = n // tm
    final = final_g is not None
    in_specs = [pl.BlockSpec(memory_space=pl.ANY),
                pl.BlockSpec((tm, d), lambda i: (i, 0)),
                pl.BlockSpec((tm, LANE), lambda i: (i, 0)),
                pl.BlockSpec((1, 6, d), lambda i: (group_of_tile(i), 0, 0))]
    args = [tab, x, route, mod]
    out_shape = [jax.ShapeDtypeStruct((n, d), F32)]
    if final:
        in_specs.append(pl.BlockSpec((1, d), lambda i: (0, 0)))
        args.append(final_g)
        out_shape.append(jax.ShapeDtypeStruct((n, d), F32))
    in_specs.append(pl.BlockSpec(memory_space=pl.ANY))
    args.append(yb)
    out_specs = [pl.BlockSpec((tm, d), lambda i: (i, 0)) for _ in out_shape]
    return pl.pallas_call(
        functools.partial(_combine_kernel, final=final),
        grid=(nt,),
        in_specs=in_specs,
        out_specs=out_specs,
        out_shape=out_shape,
        scratch_shapes=[pltpu.SMEM((2, TAB_WORDS), jnp.int32),
                        pltpu.VMEM((LOCAL_ROWS, d), F32),
                        pltpu.SemaphoreType.DMA((2,)),
                        pltpu.SemaphoreType.DMA(())],
        compiler_params=_cparams("arbitrary"),
        name="moe_combine_final" if final else "moe_combine",
    )(*args)


def _block_plan(counts_f, n_tiles):
    counts = counts_f[0, :N_EXPERTS].astype(jnp.int32)
    padded = (counts + MOE_ROWS - 1) // MOE_ROWS * MOE_ROWS
    pad_end = jnp.cumsum(padded)
    pad_start = pad_end - padded
    max_rows = n_tiles * LOCAL_ROWS
    nb = -(-max_rows // MOE_ROWS) + N_EXPERTS
    first_row = jnp.arange(nb, dtype=jnp.int32) * MOE_ROWS
    blk_e = jnp.minimum(jnp.sum((pad_end[None, :] <= first_row[:, None]).astype(jnp.int32), axis=1),
                        N_EXPERTS - 1)
    n_used = (pad_end[-1] // MOE_ROWS).reshape(1)
    return pad_start, pad_start + counts, pad_end, blk_e, n_used, nb


def _run_table(tab_f, pad_start):
    t = tab_f[:, :, :N_EXPERTS].astype(jnp.int32)
    n_chunks = t[:, TAB_LEN] // SUBLANES
    base = t[:, TAB_BASE] + pad_start[None, :]
    out = jnp.concatenate([n_chunks, t[:, TAB_OFF], base], axis=1)
    return jnp.pad(out, ((0, 0), (0, TAB_WORDS - 3 * N_EXPERTS)))


def _half_split_perm(n):
    return np.concatenate([np.arange(0, n, 2), np.arange(1, n, 2)])


def _pad_heads(w, heads, hd):
    k = w.shape[0]
    w = w.reshape(k, heads, hd)
    return jnp.pad(w, ((0, 0), (0, 0), (0, LANE - hd))).reshape(k, heads * LANE)


def _pad_rows(w, heads, hd):
    n = w.shape[1]
    w = w.reshape(heads, hd, n)
    return jnp.pad(w, ((0, 0), (0, LANE - hd), (0, 0))).reshape(heads * LANE, n)


def _rope_tables(t, rot_dim, lane_off):
    tok = jnp.arange(t)
    row = (tok // GRID_W).astype(F32)
    col = (tok % GRID_W).astype(F32)
    nf = rot_dim // 4
    inv = ROPE_THETA ** (-jnp.arange(nf, dtype=F32) / nf)
    ang = jnp.concatenate([row[:, None] * inv, col[:, None] * inv], axis=-1)
    cos, sin = jnp.cos(ang), jnp.sin(ang)
    half = rot_dim // 2
    cos_t = jnp.ones((t, LANE), F32)
    cos_t = cos_t.at[:, lane_off:lane_off + half].set(cos).at[:, lane_off + half:lane_off + rot_dim].set(cos)
    sin_a = jnp.zeros((t, LANE), F32).at[:, lane_off:lane_off + half].set(-sin)
    sin_b = jnp.zeros((t, LANE), F32).at[:, lane_off + half:lane_off + rot_dim].set(sin)
    return jnp.stack([cos_t, sin_a, sin_b])


def _na_bias_table(rpb):
    c = np.arange(GRID_W)[:, None]
    kc = np.arange(GRID_W)[None, :]
    wstart = np.clip(c - NA_KW // 2, 0, GRID_W - NA_KW)
    mask = (kc >= wstart) & (kc < wstart + NA_KW)
    dcol = np.clip(kc - c + NA_KW - 1, 0, 2 * NA_KW - 2)
    d0 = np.arange(NA_KR)[:, None] + np.arange(NA_KR)[None, :]
    tab = rpb[:, d0][:, :, :, dcol]
    tab = jnp.where(mask[None, None, None], tab.astype(F32), NEG_BIG)
    tab = jnp.transpose(tab, (0, 1, 3, 2, 4))
    return tab.reshape(NA_HEADS, NA_KR, GRID_W, NA_KR * GRID_W)


def kernel(x_prompt, x_sample, c, cache_mla_ckv, cache_mla_krope, cache_na_k, cache_na_v, state_mlstm_C, state_mlstm_n, state_mlstm_m, cache_diff_k, cache_diff_v, c_ctx, w_mod, b_mod, norm_g, final_g, a_w_in, a_q_norm, a_w_uq, a_kv_norm, a_w_ukv, a_rpb, a_w_out, b_w_in, b_gate_bias, b_ml_norm, b_df_lambda, b_df_subln, b_w_out, moe_w_router, moe_b_router, moe_w1, moe_b1, moe_w2, moe_b2):
    bp, lp, d = x_prompt.shape
    bs, ts, _ = x_sample.shape
    n_p, n_s = bp * lp, bs * ts
    past = cache_mla_ckv.shape[2]
    tiles_per_batch = ts // ROW_TILE
    grp_p = lambda i: 0
    grp_s = lambda i: 1 + i // tiles_per_batch

    xp = x_prompt.reshape(n_p, d)
    xs = x_sample.reshape(n_s, d)
    cond8 = jnp.concatenate([c_ctx[None, :], c, jnp.zeros((8 - 1 - bs, d), F32)], axis=0)
    mod_all = _modulation(cond8, w_mod, b_mod)

    perm_a = _half_split_perm(MLA_ROPE)
    inv_a = np.argsort(perm_a)
    perm_d = _half_split_perm(DF_QK)
    inv_d = np.argsort(perm_d)

    def moe_layer(l, x1p, x1s, h2p, h2s, rtp, rts, tabp, tabs_, counts, final):
        n_tiles = (n_p + n_s) // ROW_TILE
        pad_start, tail_start, pad_end, blk_e, n_used, nb = _block_plan(counts, n_tiles)
        tab_p = _run_table(tabp, pad_start)
        tab_s = _run_table(tabs_, pad_start)
        xs_rows = _dispatch(tail_start, pad_end, n_used, jnp.concatenate([tab_p, tab_s], axis=0),
                            h2p, h2s, rtp, rts, nb)
        yb = _expert_ffn(blk_e, n_used, xs_rows, moe_w1[l].astype(BF16), moe_b1[l][:, None, :],
                         moe_w2[l].astype(BF16), moe_b2[l][:, None, :])
        fg = final_g.reshape(1, d) if final else None
        outp = _combine(x1p, rtp, tab_p, yb, mod_all[l], grp_p, fg)
        outs = _combine(x1s, rts, tab_s, yb, mod_all[l], grp_s, fg)
        return outp, outs

    def router_w(l):
        wr = jnp.pad(moe_w_router[l], ((0, 0), (0, LANE - N_EXPERTS)))
        br = jnp.pad(moe_b_router[l], (0, LANE - N_EXPERTS)).reshape(1, LANE)
        return wr, br

    l, e = 0, 0
    w_in = a_w_in[e]
    kr_cols = w_in[:, 384:416][:, perm_a]
    w0 = jnp.concatenate([
        w_in[:, :384], jnp.pad(kr_cols, ((0, 0), (0, LANE - MLA_ROPE))),
        _pad_heads(w_in[:, 416:928], NA_HEADS, NA_HD),
        _pad_heads(w_in[:, 928:1440], NA_HEADS, NA_HD),
        _pad_heads(w_in[:, 1440:1952], NA_HEADS, NA_HD)], axis=1).astype(BF16)
    wuq = a_w_uq[e].reshape(MLA_Q_LORA, MLA_HEADS, MLA_NOPE + MLA_ROPE)
    wuq = jnp.concatenate([wuq[:, :, :MLA_NOPE], wuq[:, :, MLA_NOPE:][:, :, perm_a]], axis=2)
    wuq = _pad_heads(wuq.reshape(MLA_Q_LORA, -1), MLA_HEADS, MLA_NOPE + MLA_ROPE).astype(BF16)
    wukv = a_w_ukv[e].reshape(MLA_KV_LORA, MLA_HEADS, MLA_NOPE + MLA_V)
    wk = _pad_heads(wukv[:, :, :MLA_NOPE].reshape(MLA_KV_LORA, -1), MLA_HEADS, MLA_NOPE).astype(BF16)
    wv = _pad_heads(wukv[:, :, MLA_NOPE:].reshape(MLA_KV_LORA, -1), MLA_HEADS, MLA_V).astype(BF16)
    sel_np = np.zeros((LANE, MLA_HEADS * LANE), np.float32)
    for hd in range(MLA_HEADS):
        sel_np[np.arange(MLA_ROPE), hd * LANE + MLA_NOPE + np.arange(MLA_ROPE)] = 1.0
    sel = jnp.asarray(sel_np, BF16)
    qn_g = a_q_norm[e].reshape(1, -1)
    kvn_g = a_kv_norm[e].reshape(1, -1)
    g1 = norm_g[l, 0].reshape(1, d)
    g2 = norm_g[l, 1].reshape(1, d)
    tabs_a = (_rope_tables(ts, MLA_ROPE, MLA_NOPE), _rope_tables(ts, MLA_ROPE, 0))
    mla_scale = (MLA_NOPE + MLA_ROPE) ** -0.5
    wide = MLA_HEADS * LANE

    q_p, ckv_p, kr_p, nq_p, nk_p, nv_p = _even_proj(xp, g1, mod_all[l], w0, qn_g, wuq, kvn_g, grp_p, None, F32)
    new_mla_ckv = ckv_p.reshape(bp, 1, lp, MLA_KV_LORA)
    new_mla_krope = kr_p[:, :MLA_ROPE][:, inv_a].reshape(bp, 1, lp, MLA_ROPE)
    new_na_k = nk_p.reshape(bp, 1, lp, NA_HEADS, LANE)[..., :NA_HD]
    new_na_v = nv_p.reshape(bp, 1, lp, NA_HEADS, LANE)[..., :NA_HD]
    k_p, v_p = _kv_expand(ckv_p, kr_p, wk, sel, wv)
    ident = lambda hd: hd
    o_mla_p = _flash(q_p.reshape(bp, lp, wide), k_p.reshape(bp, lp, wide), v_p.reshape(bp, lp, wide),
                     scale=mla_scale, v_of_head=ident, out_dtype=BF16)
    o_na_p = _flash(nq_p.reshape(bp, lp, wide), nk_p.astype(BF16).reshape(bp, lp, wide),
                    nv_p.astype(BF16).reshape(bp, lp, wide),
                    scale=NA_HD ** -0.5, v_of_head=ident, out_dtype=BF16)
    q_s, ckv_s, kr_s, nq_s, nk_s, nv_s = _even_proj(xs, g1, mod_all[l], w0, qn_g, wuq, kvn_g, grp_s, tabs_a, BF16)
    ckv_all = jnp.concatenate([cache_mla_ckv[:, e], ckv_s.reshape(bs, ts, LANE)], axis=1)
    kr_cache = jnp.pad(cache_mla_krope[:, e][..., perm_a], ((0, 0), (0, 0), (0, LANE - MLA_ROPE)))
    kr_all = jnp.concatenate([kr_cache, kr_s.reshape(bs, ts, LANE)], axis=1)
    n_all = past + ts
    k_s, v_s = _kv_expand(ckv_all.reshape(bs * n_all, LANE), kr_all.reshape(bs * n_all, LANE), wk, sel, wv)
    o_mla_s = _flash(q_s.reshape(bs, ts, wide), k_s.reshape(bs, n_all, wide), v_s.reshape(bs, n_all, wide),
                     scale=mla_scale, v_of_head=ident, out_dtype=BF16)
    pad_hd = ((0, 0), (0, 0), (0, 0), (0, LANE - NA_HD))
    kc = jnp.pad(cache_na_k[:, e], pad_hd).astype(BF16).reshape(bs, past, wide)
    vc = jnp.pad(cache_na_v[:, e], pad_hd).astype(BF16).reshape(bs, past, wide)
    o_na_s = _na_latent(nq_s.reshape(bs, ts, wide), nk_s.reshape(bs, ts, wide), nv_s.reshape(bs, ts, wide),
                        kc, vc, _na_bias_table(a_rpb[e]))
    w_out = a_w_out[e]
    wa = _pad_rows(w_out[:MLA_HEADS * MLA_V], MLA_HEADS, MLA_V).astype(BF16)
    wb = _pad_rows(w_out[MLA_HEADS * MLA_V:], NA_HEADS, NA_HD).astype(BF16)
    wr, br = router_w(l)
    cnt0 = jnp.zeros((1, LANE), F32)
    x1p, h2p, rtp, tbp, cnt_p = _post_mix(_post_even_kernel, xp,
                                     [o_mla_p.reshape(n_p, wide), o_na_p.reshape(n_p, wide)],
                                     [wa, wb], mod_all[l], g2, wr, br, cnt0, grp_p, "post_even")
    x1s, h2s, rts, tbs, cnt_s = _post_mix(_post_even_kernel, xs,
                                     [o_mla_s.reshape(n_s, wide), o_na_s.reshape(n_s, wide)],
                                     [wa, wb], mod_all[l], g2, wr, br, cnt_p, grp_s, "post_even")
    (xp,), (xs,) = moe_layer(l, x1p, x1s, h2p, h2s, rtp, rts, tbp, tbs, cnt_s, False)

    l, o = 1, 0
    lam_init = 0.8 - 0.6 * math.exp(-0.3 * l)
    w_in = b_w_in[o]
    cuts = np.cumsum([0, 256, 256, 512, 512, 16, 512, 512, 512])
    seg = lambda i: w_in[:, cuts[i]:cuts[i + 1]]

    def diff_cols(w):
        w = w.reshape(d, 2 * DF_HEADS, DF_QK)[:, :, perm_d]
        return _pad_heads(w.reshape(d, -1), 2 * DF_HEADS, DF_QK)

    w1p = jnp.concatenate([
        _pad_heads(seg(0), ML_HEADS, ML_DQK), _pad_heads(seg(1), ML_HEADS, ML_DQK), seg(2), seg(3),
        diff_cols(seg(5)), diff_cols(seg(6)), seg(7),
        jnp.pad(seg(4), ((0, 0), (0, LANE - 16)))], axis=1).astype(BF16)
    bg = jnp.pad(b_gate_bias[o].reshape(1, 16), ((0, 0), (0, LANE - 16)))
    g1 = norm_g[l, 0].reshape(1, d)
    g2 = norm_g[l, 1].reshape(1, d)
    tab_d = _rope_tables(ts, DF_QK, 0)
    df_v_of = lambda hd: hd // 2

    mq_p, mk_p, mv_p, mo_p, gt_p, dq_p, dk_p, dv_p = _odd_proj(xp, g1, mod_all[l], w1p, bg, grp_p, None, F32)
    new_diff_k = dk_p.reshape(bp, 1, lp, DF_HEADS, 2, LANE)[..., :DF_QK][..., inv_d]
    new_diff_v = dv_p.reshape(bp, 1, lp, DF_HEADS, DF_V)
    mw = ML_HEADS * LANE
    zc = jnp.zeros((bp, ML_HEADS, LANE, LANE), F32)
    zn = jnp.zeros((bp, ML_HEADS, 1, LANE), F32)
    seq_p = lambda a: a.reshape(bp, lp, -1)
    hf_p, cf_f, nf_f, mf_f = _mlstm_dir(seq_p(mq_p), seq_p(mk_p), seq_p(mv_p), seq_p(gt_p), zc, zn, zn,
                                        direction=0, bb=4)
    hb_p, cf_b, nf_b, mf_b = _mlstm_dir(seq_p(mq_p), seq_p(mk_p), seq_p(mv_p), seq_p(gt_p), zc, zn, zn,
                                        direction=1, bb=4)
    new_mlstm_C = jnp.stack([cf_f, cf_b], axis=1)[..., :ML_DQK][:, None]
    new_mlstm_n = jnp.stack([nf_f, nf_b], axis=1)[:, :, :, 0, :ML_DQK][:, None]
    new_mlstm_m = jnp.stack([mf_f, mf_b], axis=1)[:, :, :, 0, 0][:, None]
    od_p = _flash(seq_p(dq_p), seq_p(dk_p.astype(BF16)), seq_p(dv_p.astype(BF16)),
                  scale=DF_QK ** -0.5, v_of_head=df_v_of, out_dtype=F32)

    mq_s, mk_s, mv_s, mo_s, gt_s, dq_s, dk_s, dv_s = _odd_proj(xs, g1, mod_all[l], w1p, bg, grp_s, tab_d, BF16)
    seq_s = lambda a: a.reshape(bs, ts, -1)
    c0 = jnp.pad(state_mlstm_C[:, o], ((0, 0), (0, 0), (0, 0), (0, 0), (0, LANE - ML_DQK)))
    n0 = jnp.pad(state_mlstm_n[:, o], ((0, 0), (0, 0), (0, 0), (0, LANE - ML_DQK)))[:, :, :, None, :]
    m0 = jnp.broadcast_to(state_mlstm_m[:, o][:, :, :, None, None], (bs, 2, ML_HEADS, 1, LANE))
    hf_s = _mlstm_dir(seq_s(mq_s), seq_s(mk_s), seq_s(mv_s), seq_s(gt_s), c0[:, 0], n0[:, 0], m0[:, 0],
                      direction=0, bb=4)[0]
    hb_s = _mlstm_dir(seq_s(mq_s), seq_s(mk_s), seq_s(mv_s), seq_s(gt_s), c0[:, 1], n0[:, 1], m0[:, 1],
                      direction=1, bb=4)[0]
    dk_cache = cache_diff_k[:, o][..., perm_d]
    dk_cache = jnp.pad(dk_cache, ((0, 0),) * 4 + ((0, LANE - DF_QK),)).astype(BF16).reshape(bs, past, 1024)
    dk_all = jnp.concatenate([dk_cache, seq_s(dk_s)], axis=1)
    dv_all = jnp.concatenate([cache_diff_v[:, o].astype(BF16).reshape(bs, past, 512), seq_s(dv_s)], axis=1)
    od_s = _flash(seq_s(dq_s), dk_all, dv_all, scale=DF_QK ** -0.5, v_of_head=df_v_of, out_dtype=F32)

    w_out = b_w_out[o]
    wa = w_out[:ML_HEADS * ML_DV].astype(BF16)
    wb = w_out[ML_HEADS * ML_DV:].astype(BF16)
    mln = b_ml_norm[o].reshape(1, ML_HEADS * ML_DV)
    sub = b_df_subln[o].reshape(1, DF_V)
    lam_p = jnp.pad(b_df_lambda[o], ((0, 4), (0, LANE - DF_QK)))
    wr, br = router_w(l)
    post_odd = functools.partial(_post_odd_kernel, lam_init=lam_init)
    x1p, h2p, rtp, tbp, cnt_p = _post_mix(post_odd, xp, [hf_p.reshape(n_p, mw), hb_p.reshape(n_p, mw), mo_p,
                                                    od_p.reshape(n_p, 1024)],
                                     [mln, sub, lam_p, wa, wb], mod_all[l], g2, wr, br, cnt0, grp_p, "post_odd")
    x1s, h2s, rts, tbs, cnt_s = _post_mix(post_odd, xs, [hf_s.reshape(n_s, mw), hb_s.reshape(n_s, mw), mo_s,
                                                    od_s.reshape(n_s, 1024)],
                                     [mln, sub, lam_p, wa, wb], mod_all[l], g2, wr, br, cnt_p, grp_s, "post_odd")
    (_, y_p), (_, y_s) = moe_layer(l, x1p, x1s, h2p, h2s, rtp, rts, tbp, tbs, cnt_s, True)

    return (y_p.reshape(bp, lp, d), y_s.reshape(bs, ts, d), new_mla_ckv, new_mla_krope, new_na_k, new_na_v,
            new_mlstm_C, new_mlstm_n, new_mlstm_m, new_diff_k, new_diff_v)
```

```python
import functools
import math

import numpy as np
import jax
import jax.numpy as jnp
from jax import lax
from jax.experimental import pallas as pl
from jax.experimental.pallas import tpu as pltpu

F32 = jnp.float32
BF16 = jnp.bfloat16
HIGHEST = lax.Precision.HIGHEST

D_MODEL = 1024
GRID_W = 64
LANE = 128
EPS = 1e-6
ROPE_THETA = 10000.0

MLA_HEADS, MLA_Q_LORA, MLA_KV_LORA, MLA_NOPE, MLA_ROPE, MLA_V = 8, 256, 128, 64, 32, 64
NA_HEADS, NA_HD, NA_KR, NA_KW = 8, 64, 8, 16
ML_HEADS, ML_DQK, ML_DV, ML_CHUNK = 4, 64, 128, 64
DF_HEADS, DF_QK, DF_V = 4, 64, 128
N_EXPERTS, TOP_K, D_FF = 32, 4, 1024
SWIGLU_LIMIT, SWIGLU_ALPHA = 7.0, 1.702
MLA_SCALE = (MLA_NOPE + MLA_ROPE) ** -0.5

ROW_TILE = 256
MOE_ROWS = 512
ATT_TQ, ATT_TK = 1024, 256
MXU_AHEAD = 2
FF_CHUNK = 512
NEG_BIG = -1e30
VMEM_LIMIT = 56 * 1024 * 1024


def _cparams(*sem):
    return pltpu.CompilerParams(dimension_semantics=sem, vmem_limit_bytes=VMEM_LIMIT)


def _rms(x, g):
    return x * lax.rsqrt(jnp.mean(x * x, axis=-1, keepdims=True) + EPS) * g


def _sigmoid(x):
    return 1.0 / (1.0 + jnp.exp(-x))


def _mod_kernel(c_ref, w_ref, b_ref, o_ref):
    c = c_ref[...]
    s = c * _sigmoid(c)
    o_ref[0] = jnp.dot(s, w_ref[0], precision=HIGHEST, preferred_element_type=F32) + b_ref[0]


def _modulation(cond8, w_mod, b_mod):
    depth, d, n6 = w_mod.shape
    tn = 1024
    out = pl.pallas_call(
        _mod_kernel,
        grid=(depth, n6 // tn),
        in_specs=[
            pl.BlockSpec((8, d), lambda l, j: (0, 0)),
            pl.BlockSpec((1, d, tn), lambda l, j: (l, 0, j)),
            pl.BlockSpec((1, 1, tn), lambda l, j: (l, 0, j)),
        ],
        out_specs=pl.BlockSpec((1, 8, tn), lambda l, j: (l, 0, j)),
        out_shape=jax.ShapeDtypeStruct((depth, 8, n6), F32),
        compiler_params=_cparams("parallel", "parallel"),
        name="modulation",
    )(cond8, w_mod, b_mod.reshape(depth, 1, n6))
    return out.reshape(depth, 8, 6, d)


def _rope_block(x, cos, sin_a, sin_b, half):
    up = pltpu.roll(x, LANE - half, axis=1)
    dn = pltpu.roll(x, half, axis=1)
    return x * cos + up * sin_a + dn * sin_b


def _even_proj_kernel(*refs, rope):
    if rope:
        (x_ref, g_ref, mod_ref, w_ref, qn_ref, wuq_ref, kvn_ref, tq_ref, tk_ref,
         q_ref, ckv_ref, kr_ref, nq_ref, nk_ref, nv_ref) = refs
    else:
        (x_ref, g_ref, mod_ref, w_ref, qn_ref, wuq_ref, kvn_ref,
         q_ref, ckv_ref, kr_ref, nq_ref, nk_ref, nv_ref) = refs
    x = x_ref[...]
    h = _rms(x, g_ref[...]) * (1.0 + mod_ref[0, 1:2, :]) + mod_ref[0, 0:1, :]
    hb = h.astype(BF16)

    def proj(a, b):
        return jnp.dot(hb, w_ref[:, a:b], preferred_element_type=F32)

    q_lat = proj(0, 256)
    qn = _rms(q_lat, qn_ref[...]).astype(BF16)
    for hd in range(MLA_HEADS):
        qh = jnp.dot(qn, wuq_ref[:, hd * LANE:(hd + 1) * LANE], preferred_element_type=F32)
        if rope:
            qh = _rope_block(qh, tq_ref[0], tq_ref[1], tq_ref[2], MLA_ROPE // 2)
        q_ref[:, hd * LANE:(hd + 1) * LANE] = (qh * _exp2_scale(MLA_SCALE)).astype(q_ref.dtype)
    ckv_ref[...] = _rms(proj(256, 384), kvn_ref[...])
    kr = proj(384, 512)
    if rope:
        kr = _rope_block(kr, tk_ref[0], tk_ref[1], tk_ref[2], MLA_ROPE // 2)
    kr_ref[...] = kr
    nq = proj(512, 1536) if rope else proj(512, 1536) * _exp2_scale(NA_HD ** -0.5)
    nq_ref[...] = nq.astype(nq_ref.dtype)
    nk_ref[...] = proj(1536, 2560).astype(nk_ref.dtype)
    nv_ref[...] = proj(2560, 3584).astype(nv_ref.dtype)


def _even_proj(x, g, mod, w, qn, wuq, kvn, group_of_tile, rope_tabs, kv_dtype):
    n, d = x.shape
    tm = ROW_TILE
    rope = rope_tabs is not None
    in_specs = [
        pl.BlockSpec((tm, d), lambda i: (i, 0)),
        pl.BlockSpec((1, d), lambda i: (0, 0)),
        pl.BlockSpec((1, 6, d), lambda i: (group_of_tile(i), 0, 0)),
        pl.BlockSpec(w.shape, lambda i: (0, 0)),
        pl.BlockSpec(qn.shape, lambda i: (0, 0)),
        pl.BlockSpec(wuq.shape, lambda i: (0, 0)),
        pl.BlockSpec(kvn.shape, lambda i: (0, 0)),
    ]
    args = [x, g, mod, w, qn, wuq, kvn]
    if rope:
        tq, tk = rope_tabs
        nblk = tq.shape[1] // tm
        in_specs += [pl.BlockSpec((3, tm, LANE), lambda i: (0, i % nblk, 0)),
                     pl.BlockSpec((3, tm, LANE), lambda i: (0, i % nblk, 0))]
        args += [tq, tk]
    wide = MLA_HEADS * LANE
    out_shape = [
        jax.ShapeDtypeStruct((n, wide), BF16),
        jax.ShapeDtypeStruct((n, LANE), F32),
        jax.ShapeDtypeStruct((n, LANE), F32),
        jax.ShapeDtypeStruct((n, wide), BF16),
        jax.ShapeDtypeStruct((n, wide), kv_dtype),
        jax.ShapeDtypeStruct((n, wide), kv_dtype),
    ]
    out_specs = [pl.BlockSpec((tm, s.shape[1]), lambda i: (i, 0)) for s in out_shape]
    return pl.pallas_call(
        functools.partial(_even_proj_kernel, rope=rope),
        grid=(n // tm,),
        in_specs=in_specs,
        out_specs=out_specs,
        out_shape=out_shape,
        compiler_params=_cparams("parallel"),
        name="even_proj_rope" if rope else "even_proj",
    )(*args)


def _kv_expand_kernel(ckv_ref, kr_ref, wk_ref, sel_ref, wv_ref, k_ref, v_ref):
    c = ckv_ref[...].astype(BF16)
    r = kr_ref[...].astype(BF16)
    k = jnp.dot(c, wk_ref[...], preferred_element_type=F32)
    k = k + jnp.dot(r, sel_ref[...], preferred_element_type=F32)
    k_ref[...] = k.astype(BF16)
    v_ref[...] = jnp.dot(c, wv_ref[...], preferred_element_type=F32).astype(BF16)


def _kv_expand(ckv, kr, wk, sel, wv):
    n = ckv.shape[0]
    tm = ROW_TILE
    wide = MLA_HEADS * LANE
    const = lambda i: (0, 0)
    return pl.pallas_call(
        _kv_expand_kernel,
        grid=(n // tm,),
        in_specs=[pl.BlockSpec((tm, LANE), lambda i: (i, 0)),
                  pl.BlockSpec((tm, LANE), lambda i: (i, 0)),
                  pl.BlockSpec(wk.shape, const), pl.BlockSpec(sel.shape, const),
                  pl.BlockSpec(wv.shape, const)],
        out_specs=[pl.BlockSpec((tm, wide), lambda i: (i, 0))] * 2,
        out_shape=[jax.ShapeDtypeStruct((n, wide), BF16)] * 2,
        compiler_params=_cparams("parallel"),
        name="kv_expand",
    )(ckv, kr, wk, sel, wv)


def _exp2_scale(scale):
    return scale * math.log2(math.e)


def _flash_kernel(q_ref, k_ref, v_ref, o_ref, m_sc, l_sc, acc_sc, *, heads, v_of_head):
    kj = pl.program_id(2)

    @pl.when(kj == 0)
    def _():
        m_sc[...] = jnp.full(m_sc.shape, NEG_BIG, F32)
        l_sc[...] = jnp.zeros(l_sc.shape, F32)
        acc_sc[...] = jnp.zeros(acc_sc.shape, F32)

    def scores(hd):
        sl = slice(hd * LANE, (hd + 1) * LANE)
        return lax.dot_general(k_ref[0, :, sl], q_ref[0, :, sl], (((1,), (1,)), ((), ())),
                               preferred_element_type=F32)

    pending = [scores(hd) for hd in range(min(MXU_AHEAD, heads))]
    for hd in range(heads):
        sl = slice(hd * LANE, (hd + 1) * LANE)
        vh = v_of_head(hd)
        s_t = pending.pop(0)
        if hd + MXU_AHEAD < heads:
            pending.append(scores(hd + MXU_AHEAD))
        m_prev = m_sc[hd]
        m_new = jnp.maximum(m_prev, jnp.max(s_t, axis=0, keepdims=True))
        alpha = jnp.exp2(m_prev - m_new)
        p_t = jnp.exp2(s_t - m_new)
        l_sc[hd] = alpha * l_sc[hd] + jnp.sum(p_t, axis=0, keepdims=True)
        pv_t = lax.dot_general(v_ref[0, :, vh * LANE:(vh + 1) * LANE], p_t.astype(BF16),
                               (((0,), (0,)), ((), ())), preferred_element_type=F32)
        acc_sc[sl, :] = alpha * acc_sc[sl, :] + pv_t
        m_sc[hd] = m_new

    @pl.when(kj == pl.num_programs(2) - 1)
    def _():
        for hd in range(heads):
            sl = slice(hd * LANE, (hd + 1) * LANE)
            o_ref[0, :, sl] = (acc_sc[sl, :] / l_sc[hd]).T.astype(o_ref.dtype)


def _flash(q, k, v, *, v_of_head, out_dtype):
    b, s, wq = q.shape
    nk = k.shape[1]
    wv = v.shape[2]
    heads = wq // LANE
    tq = min(ATT_TQ if out_dtype == F32 else 2 * ATT_TQ, s)
    tk = min(ATT_TK, nk)
    return pl.pallas_call(
        functools.partial(_flash_kernel, heads=heads, v_of_head=v_of_head),
        grid=(b, s // tq, nk // tk),
        in_specs=[pl.BlockSpec((1, tq, wq), lambda bi, qi, ki: (bi, qi, 0)),
                  pl.BlockSpec((1, tk, wq), lambda bi, qi, ki: (bi, ki, 0)),
                  pl.BlockSpec((1, tk, wv), lambda bi, qi, ki: (bi, ki, 0))],
        out_specs=pl.BlockSpec((1, tq, wq), lambda bi, qi, ki: (bi, qi, 0)),
        out_shape=jax.ShapeDtypeStruct((b, s, wq), out_dtype),
        scratch_shapes=[pltpu.VMEM((heads, 1, tq), F32), pltpu.VMEM((heads, 1, tq), F32),
                        pltpu.VMEM((wq, tq), F32)],
        compiler_params=_cparams("parallel", "parallel", "arbitrary"),
        name="flash_attention",
    )(q, k, v)


def _na_kernel(q_ref, k_ref, v_ref, kc_ref, vc_ref, bias_ref, o_ref, *, rows, scale):
    r = pl.program_id(1)
    r0 = jnp.clip(r - NA_KR // 2, 0, rows - NA_KR)
    start = pl.multiple_of(r0 * GRID_W, GRID_W)
    nloc = NA_KR * GRID_W
    dn = (((1,), (1,)), ((), ()))

    def scores(hd):
        sl = slice(hd * LANE, (hd + 1) * LANE)
        qh = q_ref[0, :, sl]
        kh = k_ref[0, pl.ds(start, nloc), sl]
        return (lax.dot_general(qh, kh, dn, preferred_element_type=F32),
                lax.dot_general(qh, kc_ref[0, :, sl], dn, preferred_element_type=F32))

    ahead = 2 * MXU_AHEAD
    pending = [scores(hd) for hd in range(min(ahead, NA_HEADS))]
    for hd in range(NA_HEADS):
        sl = slice(hd * LANE, (hd + 1) * LANE)
        s_loc, s_ctx = pending.pop(0)
        if hd + ahead < NA_HEADS:
            pending.append(scores(hd + ahead))
        vh = v_ref[0, pl.ds(start, nloc), sl]
        s_loc = s_loc * scale + bias_ref[hd, 0]
        s_ctx = s_ctx * scale
        m = jnp.maximum(jnp.max(s_loc, axis=1, keepdims=True), jnp.max(s_ctx, axis=1, keepdims=True))
        p_loc = jnp.exp(s_loc - m)
        p_ctx = jnp.exp(s_ctx - m)
        l = jnp.sum(p_loc, axis=1, keepdims=True) + jnp.sum(p_ctx, axis=1, keepdims=True)
        o = (jnp.dot(p_loc.astype(BF16), vh, preferred_element_type=F32)
             + jnp.dot(p_ctx.astype(BF16), vc_ref[0, :, sl], preferred_element_type=F32))
        o_ref[0, :, sl] = (o / l).astype(o_ref.dtype)


def _na_latent(q, k, v, kc, vc, bias_tab):
    b, s, wide = q.shape
    rows = s // GRID_W
    lc = kc.shape[1]

    def bias_idx(bi, r):
        return (0, jnp.clip(r - NA_KR // 2, 0, rows - NA_KR) - r + NA_KR - 1, 0, 0)

    return pl.pallas_call(
        functools.partial(_na_kernel, rows=rows, scale=NA_HD ** -0.5),
        grid=(b, rows),
        in_specs=[pl.BlockSpec((1, GRID_W, wide), lambda bi, r: (bi, r, 0)),
                  pl.BlockSpec((1, s, wide), lambda bi, r: (bi, 0, 0)),
                  pl.BlockSpec((1, s, wide), lambda bi, r: (bi, 0, 0)),
                  pl.BlockSpec((1, lc, wide), lambda bi, r: (bi, 0, 0)),
                  pl.BlockSpec((1, lc, wide), lambda bi, r: (bi, 0, 0)),
                  pl.BlockSpec((NA_HEADS, 1, GRID_W, NA_KR * GRID_W), bias_idx)],
        out_specs=pl.BlockSpec((1, GRID_W, wide), lambda bi, r: (bi, r, 0)),
        out_shape=jax.ShapeDtypeStruct((b, s, wide), BF16),
        compiler_params=_cparams("parallel", "arbitrary"),
        name="na_latent",
    )(q, k, v, kc, vc, bias_tab)


def _odd_proj_kernel(*refs, rope):
    if rope:
        (x_ref, g_ref, mod_ref, w_ref, bg_ref, td_ref,
         mq_ref, mk_ref, mv_ref, mo_ref, gt_ref, dq_ref, dk_ref, dv_ref) = refs
    else:
        (x_ref, g_ref, mod_ref, w_ref, bg_ref,
         mq_ref, mk_ref, mv_ref, mo_ref, gt_ref, dq_ref, dk_ref, dv_ref) = refs
    x = x_ref[...]
    h = _rms(x, g_ref[...]) * (1.0 + mod_ref[0, 1:2, :]) + mod_ref[0, 0:1, :]
    hb = h.astype(BF16)

    def proj(a, b):
        return jnp.dot(hb, w_ref[:, a:b], preferred_element_type=F32)

    mq_ref[...] = proj(0, 512).astype(mq_ref.dtype)
    mk_ref[...] = (proj(512, 1024) * (ML_DQK ** -0.5)).astype(mk_ref.dtype)
    mv_ref[...] = proj(1024, 1536).astype(mv_ref.dtype)
    mo_ref[...] = proj(1536, 2048)
    for blk in range(8):
        qd = proj(2048 + blk * LANE, 2048 + (blk + 1) * LANE)
        kd = proj(3072 + blk * LANE, 3072 + (blk + 1) * LANE)
        if rope:
            qd = _rope_block(qd, td_ref[0], td_ref[1], td_ref[2], DF_QK // 2)
            kd = _rope_block(kd, td_ref[0], td_ref[1], td_ref[2], DF_QK // 2)
        dq_ref[:, blk * LANE:(blk + 1) * LANE] = (qd * _exp2_scale(DF_QK ** -0.5)).astype(dq_ref.dtype)
        dk_ref[:, blk * LANE:(blk + 1) * LANE] = kd.astype(dk_ref.dtype)
    dv_ref[...] = proj(4096, 4608).astype(dv_ref.dtype)
    gt = proj(4608, 4736) + bg_ref[...]
    lane = lax.broadcasted_iota(jnp.int32, gt.shape, 1)
    log_f = jnp.minimum(gt, 0.0) - jnp.log(1.0 + jnp.exp(-jnp.abs(gt)))
    gt_ref[...] = jnp.where((lane % 8) >= 4, log_f, gt)


def _odd_proj(x, g, mod, w, bg, group_of_tile, rope_tab, kv_dtype):
    n, d = x.shape
    tm = ROW_TILE
    rope = rope_tab is not None
    in_specs = [
        pl.BlockSpec((tm, d), lambda i: (i, 0)),
        pl.BlockSpec((1, d), lambda i: (0, 0)),
        pl.BlockSpec((1, 6, d), lambda i: (group_of_tile(i), 0, 0)),
        pl.BlockSpec(w.shape, lambda i: (0, 0)),
        pl.BlockSpec(bg.shape, lambda i: (0, 0)),
    ]
    args = [x, g, mod, w, bg]
    if rope:
        nblk = rope_tab.shape[1] // tm
        in_specs.append(pl.BlockSpec((3, tm, LANE), lambda i: (0, i % nblk, 0)))
        args.append(rope_tab)
    out_shape = [
        jax.ShapeDtypeStruct((n, 512), BF16),
        jax.ShapeDtypeStruct((n, 512), BF16),
        jax.ShapeDtypeStruct((n, 512), BF16),
        jax.ShapeDtypeStruct((n, 512), F32),
        jax.ShapeDtypeStruct((n, LANE), F32),
        jax.ShapeDtypeStruct((n, 1024), BF16),
        jax.ShapeDtypeStruct((n, 1024), kv_dtype),
        jax.ShapeDtypeStruct((n, 512), kv_dtype),
    ]
    out_specs = [pl.BlockSpec((tm, s.shape[1]), lambda i: (i, 0)) for s in out_shape]
    return pl.pallas_call(
        functools.partial(_odd_proj_kernel, rope=rope),
        grid=(n // tm,),
        in_specs=in_specs,
        out_specs=out_specs,
        out_shape=out_shape,
        compiler_params=_cparams("parallel"),
        name="odd_proj_rope" if rope else "odd_proj",
    )(*args)


def _mlstm_kernel(q_ref, k_ref, v_ref, g_ref, c0_ref, n0_ref, m0_ref,
                  h_ref, cf_ref, nf_ref, mf_ref, c_sc, n_sc, m_sc, *, bb, direction, reverse):
    step = pl.program_id(1)
    L = ML_CHUNK

    @pl.when(step == 0)
    def _():
        c_sc[...] = c0_ref[...]
        n_sc[...] = n0_ref[...]
        m_sc[...] = m0_ref[...]

    t_idx = lax.broadcasted_iota(jnp.int32, (L, L), 0)
    s_idx = lax.broadcasted_iota(jnp.int32, (L, L), 1)
    keep = (s_idx >= t_idx) if reverse else (s_idx <= t_idx)
    tri = keep.astype(F32)
    last = 0 if reverse else L - 1
    for bi in range(bb):
        g = g_ref[bi]
        bcum = jnp.dot(tri, g, precision=HIGHEST, preferred_element_type=F32)
        g_t = g.T
        b_t = bcum.T
        for hd in range(ML_HEADS):
            sl = slice(hd * LANE, (hd + 1) * LANE)
            ci = direction * 8 + hd
            cf = direction * 8 + 4 + hd
            li_row = g_t[ci:ci + 1, :]
            b_row = b_t[cf:cf + 1, :]
            li_col = g[:, ci:ci + 1]
            b_col = bcum[:, cf:cf + 1]
            m_old = m_sc[bi, hd][:, 0:1]
            n_old = n_sc[bi, hd]
            c_old = c_sc[bi, hd]
            qh = q_ref[bi, :, sl]
            kh = k_ref[bi, :, sl]
            vh = v_ref[bi, :, sl]
            dm = jnp.where(keep, b_col - b_row + li_row, NEG_BIG)
            m_inter = b_col + m_old
            m_t = jnp.maximum(m_inter, jnp.max(dm, axis=1, keepdims=True))
            qk = lax.dot_general(qh, kh, (((1,), (1,)), ((), ())), preferred_element_type=F32)
            w_intra = jnp.where(keep, jnp.exp(dm - m_t), 0.0) * qk
            w_inter = jnp.exp(m_inter - m_t)
            qc = lax.dot_general(qh, c_old.astype(BF16), (((1,), (1,)), ((), ())),
                                 preferred_element_type=F32)
            num = jnp.dot(w_intra.astype(BF16), vh, preferred_element_type=F32) + w_inter * qc
            qn = jnp.sum(qh.astype(F32) * n_old, axis=1, keepdims=True)
            den = jnp.sum(w_intra, axis=1, keepdims=True) + w_inter * qn
            h_ref[bi, :, sl] = num / jnp.maximum(jnp.abs(den), jnp.exp(-m_t))
            m_new = m_t[last:last + 1, :]
            b_last = b_col[last:last + 1, :]
            w_end = jnp.exp(b_last - b_col + li_col - m_new)
            decay = jnp.exp(b_last + m_old - m_new)
            wv = (w_end * vh.astype(F32)).astype(BF16)
            upd = lax.dot_general(wv, kh, (((0,), (0,)), ((), ())), preferred_element_type=F32)
            c_sc[bi, hd] = decay * c_old + upd
            n_sc[bi, hd] = decay * n_old + jnp.sum(w_end * kh.astype(F32), axis=0, keepdims=True)
            m_sc[bi, hd] = jnp.broadcast_to(m_new, (1, LANE))

    @pl.when(step == pl.num_programs(1) - 1)
    def _():
        cf_ref[...] = c_sc[...]
        nf_ref[...] = n_sc[...]
        mf_ref[...] = m_sc[...]


def _mlstm_dir(q, k, v, gates, c0, n0, m0, *, direction, bb):
    b, t, w = q.shape
    nc = t // ML_CHUNK
    reverse = direction == 1
    if reverse:
        cidx = lambda bi, c: (bi, nc - 1 - c, 0)
    else:
        cidx = lambda bi, c: (bi, c, 0)
    st4 = lambda bi, c: (bi, 0, 0, 0)
    seq = pl.BlockSpec((bb, ML_CHUNK, w), cidx)
    c_spec = pl.BlockSpec((bb, ML_HEADS, LANE, LANE), st4)
    n_spec = pl.BlockSpec((bb, ML_HEADS, 1, LANE), st4)
    return pl.pallas_call(
        functools.partial(_mlstm_kernel, bb=bb, direction=direction, reverse=reverse),
        grid=(b // bb, nc),
        in_specs=[seq, seq, seq, pl.BlockSpec((bb, ML_CHUNK, LANE), cidx), c_spec, n_spec, n_spec],
        out_specs=[seq, c_spec, n_spec, n_spec],
        out_shape=[jax.ShapeDtypeStruct((b, t, w), F32),
                   jax.ShapeDtypeStruct(c0.shape, F32),
                   jax.ShapeDtypeStruct(n0.shape, F32),
                   jax.ShapeDtypeStruct(m0.shape, F32)],
        scratch_shapes=[pltpu.VMEM((bb, ML_HEADS, LANE, LANE), F32),
                        pltpu.VMEM((bb, ML_HEADS, 1, LANE), F32),
                        pltpu.VMEM((bb, ML_HEADS, 1, LANE), F32)],
        compiler_params=_cparams("parallel", "arbitrary"),
        name="mlstm_bwd" if reverse else "mlstm_fwd",
    )(q, k, v, gates, c0, n0, m0)


ROUTE_IDX, ROUTE_ROW, ROUTE_GATE = 0, 4, 8
TAB_LEN, TAB_OFF, TAB_BASE = 0, 1, 2
SUBLANES = 8


def _post_tail(x_ref, mix, mod_ref, g2_ref, wr_ref, br_ref, cin_ref,
               x_out_ref, h_ref, rt_ref, tab_ref, cout_ref, cnt_sc):
    @pl.when(pl.program_id(0) == 0)
    def _():
        cnt_sc[...] = cin_ref[...]

    x1 = x_ref[...] + mod_ref[0, 2:3, :] * mix
    x_out_ref[...] = x1
    h2 = _rms(x1, g2_ref[...]) * (1.0 + mod_ref[0, 4:5, :]) + mod_ref[0, 3:4, :]
    h_ref[...] = h2
    logits = jnp.dot(h2, wr_ref[...], precision=HIGHEST, preferred_element_type=F32) + br_ref[...]
    tm = logits.shape[0]
    lane = lax.broadcasted_iota(jnp.int32, (tm, LANE), 1)
    lg = jnp.where(lane < N_EXPERTS, logits, NEG_BIG)
    vals, idxs, hots = [], [], []
    for _ in range(TOP_K):
        mk = jnp.max(lg, axis=1, keepdims=True)
        idx = jnp.min(jnp.where(lg == mk, lane, LANE), axis=1, keepdims=True)
        hot = lane == idx
        vals.append(mk)
        idxs.append(idx)
        hots.append(hot)
        lg = jnp.where(hot, NEG_BIG, lg)
    ex = [jnp.exp(v - vals[0]) for v in vals]
    tot = ex[0] + ex[1] + ex[2] + ex[3]
    sel = jnp.zeros((tm, LANE), F32)
    for hot in hots:
        sel = sel + hot.astype(F32)
    t_idx = lax.broadcasted_iota(jnp.int32, (tm, tm), 0)
    s_idx = lax.broadcasted_iota(jnp.int32, (tm, tm), 1)
    earlier = (s_idx < t_idx).astype(BF16)
    prefix = jnp.dot(earlier, sel.astype(BF16), preferred_element_type=F32)
    run_len = jnp.floor((jnp.sum(sel, axis=0, keepdims=True) + (SUBLANES - 1)) / SUBLANES) * SUBLANES
    e_src = lax.broadcasted_iota(jnp.int32, (LANE, LANE), 0)
    e_dst = lax.broadcasted_iota(jnp.int32, (LANE, LANE), 1)
    before = (e_src < e_dst).astype(F32)
    run_off = jnp.dot(jnp.broadcast_to(run_len, (SUBLANES, LANE)), before, precision=HIGHEST,
                      preferred_element_type=F32)[0:1]
    local = prefix + run_off
    route = jnp.zeros((tm, LANE), F32)
    for k in range(TOP_K):
        row = jnp.sum(jnp.where(hots[k], local, 0.0), axis=1, keepdims=True)
        route = jnp.where(lane == ROUTE_IDX + k, idxs[k].astype(F32), route)
        route = jnp.where(lane == ROUTE_ROW + k, row, route)
        route = jnp.where(lane == ROUTE_GATE + k, ex[k] / tot, route)
    rt_ref[...] = route
    sub = lax.broadcasted_iota(jnp.int32, (SUBLANES, LANE), 0)
    tab = jnp.where(sub == TAB_LEN, run_len, 0.0)
    tab = jnp.where(sub == TAB_OFF, run_off, tab)
    tab = jnp.where(sub == TAB_BASE, cnt_sc[...], tab)
    tab_ref[0] = tab
    cnt_new = cnt_sc[...] + run_len
    cnt_sc[...] = cnt_new
    cout_ref[...] = cnt_new


def _post_even_kernel(x_ref, a_ref, b_ref, wa_ref, wb_ref, *rest):
    mix = (jnp.dot(a_ref[...], wa_ref[...], preferred_element_type=F32)
           + jnp.dot(b_ref[...], wb_ref[...], preferred_element_type=F32))
    _post_tail(x_ref, mix, *rest)


def _post_odd_kernel(x_ref, hf_ref, hb_ref, mo_ref, od_ref, mln_ref, sub_ref, lam_ref,
                     wa_ref, wb_ref, *rest, lam_init):
    lp = lam_ref[...]
    lam = (jnp.exp(jnp.sum(lp[0:1] * lp[1:2], axis=1, keepdims=True))
           - jnp.exp(jnp.sum(lp[2:3] * lp[3:4], axis=1, keepdims=True)) + lam_init)
    mix = None
    for hd in range(ML_HEADS):
        sl = slice(hd * LANE, (hd + 1) * LANE)
        hm = hf_ref[:, sl] + hb_ref[:, sl]
        om = _rms(hm, mln_ref[:, sl]) * _sigmoid(mo_ref[:, sl])
        part = jnp.dot(om.astype(BF16), wa_ref[sl, :], preferred_element_type=F32)
        mix = part if mix is None else mix + part
    for hd in range(DF_HEADS):
        o1 = od_ref[:, (2 * hd) * LANE:(2 * hd + 1) * LANE]
        o2 = od_ref[:, (2 * hd + 1) * LANE:(2 * hd + 2) * LANE]
        od = _rms(o1 - lam * o2, sub_ref[...]) * (1.0 - lam_init)
        mix = mix + jnp.dot(od.astype(BF16), wb_ref[hd * LANE:(hd + 1) * LANE, :],
                            preferred_element_type=F32)
    _post_tail(x_ref, mix, *rest)


def _post_mix(kernel, x, row_inputs, const_inputs, mod, g2, wr, br, cnt_in, group_of_tile, name):
    n, d = x.shape
    tm = ROW_TILE
    row = lambda a: pl.BlockSpec((tm, a.shape[1]), lambda i: (i, 0))
    const = lambda a: pl.BlockSpec(a.shape, lambda i: (0,) * a.ndim)
    in_specs = ([row(x)] + [row(a) for a in row_inputs] + [const(a) for a in const_inputs]
                + [pl.BlockSpec((1, 6, d), lambda i: (group_of_tile(i), 0, 0)),
                   const(g2), const(wr), const(br), const(cnt_in)])
    return pl.pallas_call(
        kernel,
        grid=(n // tm,),
        in_specs=in_specs,
        out_specs=[pl.BlockSpec((tm, d), lambda i: (i, 0)),
                   pl.BlockSpec((tm, d), lambda i: (i, 0)),
                   pl.BlockSpec((tm, LANE), lambda i: (i, 0)),
                   pl.BlockSpec((1, SUBLANES, LANE), lambda i: (i, 0, 0)),
                   pl.BlockSpec((1, LANE), lambda i: (0, 0))],
        out_shape=[jax.ShapeDtypeStruct((n, d), F32), jax.ShapeDtypeStruct((n, d), F32),
                   jax.ShapeDtypeStruct((n, LANE), F32),
                   jax.ShapeDtypeStruct((n // tm, SUBLANES, LANE), F32),
                   jax.ShapeDtypeStruct((1, LANE), F32)],
        scratch_shapes=[pltpu.VMEM((1, LANE), F32)],
        compiler_params=_cparams("arbitrary"),
        name=name,
    )(x, *row_inputs, *const_inputs, mod, g2, wr, br, cnt_in)


LOCAL_ROWS = ROW_TILE * TOP_K + N_EXPERTS * SUBLANES
TAB_WORDS = 128


def _split_bf16(x):
    hi = x.astype(BF16)
    return hi, (x - hi.astype(F32)).astype(BF16)


BIG_CHUNK = 4 * SUBLANES


def _run_copies(tab_smem, slot, make_copy, start):
    def per_expert(e, carry):
        n_big = tab_smem[slot, e]
        n_small = tab_smem[slot, N_EXPERTS + e]
        off = tab_smem[slot, 2 * N_EXPERTS + e]
        base = tab_smem[slot, 3 * N_EXPERTS + e]

        def chunk(rows, shift):
            def one(j, c):
                cp = make_copy(pl.multiple_of(off + shift + j * rows, SUBLANES),
                               pl.multiple_of(base + shift + j * rows, SUBLANES), rows)
                if start:
                    cp.start()
                else:
                    cp.wait()
                return c
            return one

        carry = lax.fori_loop(0, n_big, chunk(BIG_CHUNK, 0), carry)
        return lax.fori_loop(0, n_small, chunk(SUBLANES, n_big * BIG_CHUNK), carry)

    lax.fori_loop(0, N_EXPERTS, per_expert, 0)


def _dispatch_kernel(tail_ref, pend_ref, nu_ref, tab_hbm, hp_ref, hs_ref, rp_ref, rs_ref, xs_ref,
                     tab_smem, local_buf, zero_buf, sem_tab, sem_rows, sem_zero, *, tiles_p, n_blocks):
    i = pl.program_id(0)
    nt = pl.num_programs(0)
    slot = i % 2
    d = hp_ref.shape[1]

    def tab_copy(step, sl):
        return pltpu.make_async_copy(tab_hbm.at[step], tab_smem.at[sl], sem_tab.at[sl])

    def zero_block(start):
        return pltpu.make_async_copy(zero_buf, xs_ref.at[pl.ds(start, MOE_ROWS)], sem_zero)

    def zero_chunk(start):
        return pltpu.make_async_copy(zero_buf.at[pl.ds(0, SUBLANES)], xs_ref.at[pl.ds(start, SUBLANES)],
                                     sem_rows)

    @pl.when(i == 0)
    def _():
        tab_copy(0, 0).start()
        zero_buf[...] = jnp.zeros(zero_buf.shape, F32)

        def tails(start):
            def per_expert(e, carry):
                def one(j, c):
                    cp = zero_chunk(pl.multiple_of(tail_ref[e] + j * SUBLANES, SUBLANES))
                    if start:
                        cp.start()
                    else:
                        cp.wait()
                    return c
                return lax.fori_loop(0, (pend_ref[e] - tail_ref[e]) // SUBLANES, one, carry)
            lax.fori_loop(0, N_EXPERTS, per_expert, 0)

        def blocks(start):
            def one(b, c):
                cp = zero_block(pl.multiple_of(b * MOE_ROWS, MOE_ROWS))
                if start:
                    cp.start()
                else:
                    cp.wait()
                return c
            lax.fori_loop(nu_ref[0], n_blocks, one, 0)

        tails(True)
        blocks(True)
        tails(False)
        blocks(False)

    def sort_tile(h_ref, rt_ref):
        local = local_buf.at[slot]
        rt_t = rt_ref[...].T
        row_id = lax.broadcasted_iota(jnp.int32, (LOCAL_ROWS, ROW_TILE), 0).astype(F32)
        pick = jnp.zeros((LOCAL_ROWS, ROW_TILE), F32)
        gate = jnp.zeros((LOCAL_ROWS, ROW_TILE), F32)
        for k in range(TOP_K):
            hit = row_id == rt_t[ROUTE_ROW + k:ROUTE_ROW + k + 1, :]
            pick = jnp.where(hit, 1.0, pick)
            gate = jnp.where(hit, rt_t[ROUTE_GATE + k:ROUTE_GATE + k + 1, :], gate)
        local[:, 0:d] = jnp.dot(pick.astype(BF16), h_ref[...].astype(BF16), preferred_element_type=F32)
        g_hi, g_lo = _split_bf16(gate)
        ones = jnp.ones((ROW_TILE, LANE), BF16)
        local[:, d:d + LANE] = (jnp.dot(g_hi, ones, preferred_element_type=F32)
                                + jnp.dot(g_lo, ones, preferred_element_type=F32))

    @pl.when(i < tiles_p)
    def _():
        sort_tile(hp_ref, rp_ref)

    @pl.when(i >= tiles_p)
    def _():
        sort_tile(hs_ref, rs_ref)

    def run_copy(buf):
        def make(off, base, rows):
            return pltpu.make_async_copy(local_buf.at[buf, pl.ds(off, rows)], xs_ref.at[pl.ds(base, rows)],
                                         sem_rows)
        return make

    @pl.when(i > 0)
    def _():
        _run_copies(tab_smem, 1 - slot, run_copy(1 - slot), False)

    @pl.when(i + 1 < nt)
    def _():
        tab_copy(i + 1, 1 - slot).start()

    tab_copy(i, slot).wait()
    _run_copies(tab_smem, slot, run_copy(slot), True)

    @pl.when(i == nt - 1)
    def _():
        _run_copies(tab_smem, slot, run_copy(slot), False)


def _dispatch(tail_start, pad_end, n_used, tab, h2p, h2s, rtp, rts, n_blocks):
    d = h2p.shape[1]
    tm = ROW_TILE
    tiles_p = h2p.shape[0] // tm
    nt = tiles_p + h2s.shape[0] // tm
    idx_p = lambda i, *_: (jnp.minimum(i, tiles_p - 1), 0)
    idx_s = lambda i, *_: (jnp.maximum(i - tiles_p, 0), 0)
    grid_spec = pltpu.PrefetchScalarGridSpec(
        num_scalar_prefetch=3,
        grid=(nt,),
        in_specs=[pl.BlockSpec(memory_space=pl.ANY),
                  pl.BlockSpec((tm, d), idx_p), pl.BlockSpec((tm, d), idx_s),
                  pl.BlockSpec((tm, LANE), idx_p), pl.BlockSpec((tm, LANE), idx_s)],
        out_specs=pl.BlockSpec(memory_space=pl.ANY),
        scratch_shapes=[pltpu.SMEM((2, TAB_WORDS), jnp.int32),
                        pltpu.VMEM((2, LOCAL_ROWS, d + LANE), F32),
                        pltpu.VMEM((MOE_ROWS, d + LANE), F32),
                        pltpu.SemaphoreType.DMA((2,)),
                        pltpu.SemaphoreType.DMA(()),
                        pltpu.SemaphoreType.DMA(())],
    )
    return pl.pallas_call(
        functools.partial(_dispatch_kernel, tiles_p=tiles_p, n_blocks=n_blocks),
        grid_spec=grid_spec,
        out_shape=jax.ShapeDtypeStruct((n_blocks * MOE_ROWS, d + LANE), F32),
        compiler_params=_cparams("arbitrary"),
        name="moe_dispatch",
    )(tail_start, pad_end, n_used, tab, h2p, h2s, rtp, rts)


def _expert_kernel(be_ref, nu_ref, x_ref, w1_ref, b1_ref, w2_ref, b2_ref, o_ref, w1b, w2b):
    i = pl.program_id(0)
    d = o_ref.shape[1]
    new_expert = jnp.logical_or(i == 0, be_ref[i] != be_ref[jnp.maximum(i - 1, 0)])

    @pl.when(jnp.logical_and(new_expert, i < nu_ref[0]))
    def _():
        w1b[...] = w1_ref[0].astype(BF16)
        w2b[...] = w2_ref[0].astype(BF16)

    @pl.when(i < nu_ref[0])
    def _():
        x = x_ref[:, 0:d].astype(BF16)
        gate = x_ref[:, d:d + 1]
        acc = None
        fc = FF_CHUNK
        for c in range(D_FF // fc):
            glu = jnp.dot(x, w1b[:, c * fc:(c + 1) * fc], preferred_element_type=F32)
            glu = glu + b1_ref[0, :, c * fc:(c + 1) * fc]
            lin = jnp.dot(x, w1b[:, D_FF + c * fc:D_FF + (c + 1) * fc],
                          preferred_element_type=F32)
            lin = lin + b1_ref[0, :, D_FF + c * fc:D_FF + (c + 1) * fc]
            glu = jnp.minimum(glu, SWIGLU_LIMIT)
            lin = jnp.clip(lin, -SWIGLU_LIMIT, SWIGLU_LIMIT)
            act = glu * _sigmoid(SWIGLU_ALPHA * glu) * (lin + 1.0)
            part = jnp.dot(act.astype(BF16), w2b[c * fc:(c + 1) * fc, :],
                           preferred_element_type=F32)
            acc = part if acc is None else acc + part
        o_ref[...] = (acc + b2_ref[0]) * gate

    @pl.when(i >= nu_ref[0])
    def _():
        o_ref[...] = jnp.zeros(o_ref.shape, F32)


def _expert_ffn(blk_e, n_used, xs, layer, w1, b1, w2, b2):
    d = w2.shape[3]
    nb = blk_e.shape[0]
    wmap = lambda i, be, nu: (layer, be[i], 0, 0)
    sq = pl.Squeezed()
    grid_spec = pltpu.PrefetchScalarGridSpec(
        num_scalar_prefetch=2,
        grid=(nb,),
        in_specs=[pl.BlockSpec((MOE_ROWS, xs.shape[1]), lambda i, be, nu: (i, 0)),
                  pl.BlockSpec((sq, 1, d, 2 * D_FF), wmap),
                  pl.BlockSpec((sq, 1, 1, 2 * D_FF), wmap),
                  pl.BlockSpec((sq, 1, D_FF, d), wmap),
                  pl.BlockSpec((sq, 1, 1, d), wmap)],
        out_specs=pl.BlockSpec((MOE_ROWS, d), lambda i, be, nu: (i, 0)),
        scratch_shapes=[pltpu.VMEM((d, 2 * D_FF), BF16), pltpu.VMEM((D_FF, d), BF16)],
    )
    return pl.pallas_call(
        _expert_kernel,
        grid_spec=grid_spec,
        out_shape=jax.ShapeDtypeStruct((nb * MOE_ROWS, d), F32),
        compiler_params=_cparams("arbitrary"),
        name="expert_ffn",
    )(blk_e, n_used, xs, w1, b1, w2, b2)


def _combine_kernel(*refs, final):
    if final:
        (tab_hbm, x_ref, rt_ref, mod_ref, fg_ref, yb_ref, x_out_ref, yn_ref,
         tab_smem, local_buf, sem_tab, sem_rows) = refs
    else:
        (tab_hbm, x_ref, rt_ref, mod_ref, yb_ref, x_out_ref,
         tab_smem, local_buf, sem_tab, sem_rows) = refs
    i = pl.program_id(0)
    nt = pl.num_programs(0)
    slot = i % 2

    def tab_copy(step):
        return pltpu.make_async_copy(tab_hbm.at[step], tab_smem.at[step % 3], sem_tab.at[step % 3])

    def run_copy(buf):
        def make(off, base, rows):
            return pltpu.make_async_copy(yb_ref.at[pl.ds(base, rows)], local_buf.at[buf, pl.ds(off, rows)],
                                         sem_rows.at[buf])
        return make

    @pl.when(i == 0)
    def _():
        local_buf[...] = jnp.zeros(local_buf.shape, F32)
        tab_copy(0).start()
        tab_copy(0).wait()
        _run_copies(tab_smem, 0, run_copy(0), True)

        @pl.when(nt > 1)
        def _():
            tab_copy(1).start()

    @pl.when(i + 1 < nt)
    def _():
        tab_copy(i + 1).wait()
        _run_copies(tab_smem, (i + 1) % 3, run_copy(1 - slot), True)

    @pl.when(i + 2 < nt)
    def _():
        tab_copy(i + 2).start()

    _run_copies(tab_smem, i % 3, run_copy(slot), False)
    local = local_buf.at[slot]

    rt = rt_ref[...]
    col_id = lax.broadcasted_iota(jnp.int32, (ROW_TILE, LOCAL_ROWS), 1).astype(F32)
    pick = jnp.zeros((ROW_TILE, LOCAL_ROWS), F32)
    for k in range(TOP_K):
        pick = jnp.where(col_id == rt[:, ROUTE_ROW + k:ROUTE_ROW + k + 1], 1.0, pick)
    pick = pick.astype(BF16)
    y_hi, y_lo = _split_bf16(local[...])
    moe = (jnp.dot(pick, y_hi, preferred_element_type=F32)
           + jnp.dot(pick, y_lo, preferred_element_type=F32))
    x2 = x_ref[...] + mod_ref[0, 5:6, :] * moe
    x_out_ref[...] = x2
    if final:
        yn_ref[...] = _rms(x2, fg_ref[...])


def _combine(x, route, tab, yb, mod, group_of_tile, final_g):
    n, d = x.shape
    tm = ROW_TILE
    nt = n // tm
    final = final_g is not None
    in_specs = [pl.BlockSpec(memory_space=pl.ANY),
                pl.BlockSpec((tm, d), lambda i: (i, 0)),
                pl.BlockSpec((tm, LANE), lambda i: (i, 0)),
                pl.BlockSpec((1, 6, d), lambda i: (group_of_tile(i), 0, 0))]
    args = [tab, x, route, mod]
    out_shape = [jax.ShapeDtypeStruct((n, d), F32)]
    if final:
        in_specs.append(pl.BlockSpec((1, d), lambda i: (0, 0)))
        args.append(final_g)
        out_shape.append(jax.ShapeDtypeStruct((n, d), F32))
    in_specs.append(pl.BlockSpec(memory_space=pl.ANY))
    args.append(yb)
    out_specs = [pl.BlockSpec((tm, d), lambda i: (i, 0)) for _ in out_shape]
    return pl.pallas_call(
        functools.partial(_combine_kernel, final=final),
        grid=(nt,),
        in_specs=in_specs,
        out_specs=out_specs,
        out_shape=out_shape,
        scratch_shapes=[pltpu.SMEM((3, TAB_WORDS), jnp.int32),
                        pltpu.VMEM((2, LOCAL_ROWS, d), F32),
                        pltpu.SemaphoreType.DMA((3,)),
                        pltpu.SemaphoreType.DMA((2,))],
        compiler_params=_cparams("arbitrary"),
        name="moe_combine_final" if final else "moe_combine",
    )(*args)


def _block_plan(counts_f, n_tiles):
    counts = counts_f[0, :N_EXPERTS].astype(jnp.int32)
    padded = (counts + MOE_ROWS - 1) // MOE_ROWS * MOE_ROWS
    pad_end = jnp.cumsum(padded)
    pad_start = pad_end - padded
    max_rows = n_tiles * LOCAL_ROWS
    nb = -(-max_rows // MOE_ROWS) + N_EXPERTS
    first_row = jnp.arange(nb, dtype=jnp.int32) * MOE_ROWS
    blk_e = jnp.minimum(jnp.sum((pad_end[None, :] <= first_row[:, None]).astype(jnp.int32), axis=1),
                        N_EXPERTS - 1)
    n_used = (pad_end[-1] // MOE_ROWS).reshape(1)
    return pad_start, pad_start + counts, pad_end, blk_e, n_used, nb


def _run_table(tab_f, pad_start):
    t = tab_f[:, :, :N_EXPERTS].astype(jnp.int32)
    n_big = t[:, TAB_LEN] // BIG_CHUNK
    n_small = (t[:, TAB_LEN] - n_big * BIG_CHUNK) // SUBLANES
    base = t[:, TAB_BASE] + pad_start[None, :]
    return jnp.concatenate([n_big, n_small, t[:, TAB_OFF], base], axis=1)


def _half_split_perm(n):
    return np.concatenate([np.arange(0, n, 2), np.arange(1, n, 2)])


def _pad_heads(w, heads, hd):
    k = w.shape[0]
    w = w.reshape(k, heads, hd)
    return jnp.pad(w, ((0, 0), (0, 0), (0, LANE - hd))).reshape(k, heads * LANE)


def _pad_rows(w, heads, hd):
    n = w.shape[1]
    w = w.reshape(heads, hd, n)
    return jnp.pad(w, ((0, 0), (0, LANE - hd), (0, 0))).reshape(heads * LANE, n)


def _rope_tables(t, rot_dim, lane_off):
    tok = jnp.arange(t)
    row = (tok // GRID_W).astype(F32)
    col = (tok % GRID_W).astype(F32)
    nf = rot_dim // 4
    inv = ROPE_THETA ** (-jnp.arange(nf, dtype=F32) / nf)
    ang = jnp.concatenate([row[:, None] * inv, col[:, None] * inv], axis=-1)
    cos, sin = jnp.cos(ang), jnp.sin(ang)
    half = rot_dim // 2
    cos_t = jnp.ones((t, LANE), F32)
    cos_t = cos_t.at[:, lane_off:lane_off + half].set(cos).at[:, lane_off + half:lane_off + rot_dim].set(cos)
    sin_a = jnp.zeros((t, LANE), F32).at[:, lane_off:lane_off + half].set(-sin)
    sin_b = jnp.zeros((t, LANE), F32).at[:, lane_off + half:lane_off + rot_dim].set(sin)
    return jnp.stack([cos_t, sin_a, sin_b])


def _na_bias_table(rpb):
    c = np.arange(GRID_W)[:, None]
    kc = np.arange(GRID_W)[None, :]
    wstart = np.clip(c - NA_KW // 2, 0, GRID_W - NA_KW)
    mask = (kc >= wstart) & (kc < wstart + NA_KW)
    dcol = np.clip(kc - c + NA_KW - 1, 0, 2 * NA_KW - 2)
    d0 = np.arange(NA_KR)[:, None] + np.arange(NA_KR)[None, :]
    tab = rpb[:, d0][:, :, :, dcol]
    tab = jnp.where(mask[None, None, None], tab.astype(F32), NEG_BIG)
    tab = jnp.transpose(tab, (0, 1, 3, 2, 4))
    return tab.reshape(NA_HEADS, NA_KR, GRID_W, NA_KR * GRID_W)


def kernel(x_prompt, x_sample, c, cache_mla_ckv, cache_mla_krope, cache_na_k, cache_na_v, state_mlstm_C, state_mlstm_n, state_mlstm_m, cache_diff_k, cache_diff_v, c_ctx, w_mod, b_mod, norm_g, final_g, a_w_in, a_q_norm, a_w_uq, a_kv_norm, a_w_ukv, a_rpb, a_w_out, b_w_in, b_gate_bias, b_ml_norm, b_df_lambda, b_df_subln, b_w_out, moe_w_router, moe_b_router, moe_w1, moe_b1, moe_w2, moe_b2):
    bp, lp, d = x_prompt.shape
    bs, ts, _ = x_sample.shape
    n_p, n_s = bp * lp, bs * ts
    past = cache_mla_ckv.shape[2]
    tiles_per_batch = ts // ROW_TILE
    grp_p = lambda i: 0
    grp_s = lambda i: 1 + i // tiles_per_batch

    xp = x_prompt.reshape(n_p, d)
    xs = x_sample.reshape(n_s, d)
    cond8 = jnp.concatenate([c_ctx[None, :], c, jnp.zeros((8 - 1 - bs, d), F32)], axis=0)
    mod_all = _modulation(cond8, w_mod, b_mod)

    perm_a = _half_split_perm(MLA_ROPE)
    inv_a = np.argsort(perm_a)
    perm_d = _half_split_perm(DF_QK)
    inv_d = np.argsort(perm_d)

    def moe_layer(l, x1p, x1s, h2p, h2s, rtp, rts, tabp, tabs_, counts, final):
        n_tiles = (n_p + n_s) // ROW_TILE
        pad_start, tail_start, pad_end, blk_e, n_used, nb = _block_plan(counts, n_tiles)
        tab_p = _run_table(tabp, pad_start)
        tab_s = _run_table(tabs_, pad_start)
        xs_rows = _dispatch(tail_start, pad_end, n_used, jnp.concatenate([tab_p, tab_s], axis=0),
                            h2p, h2s, rtp, rts, nb)
        yb = _expert_ffn(blk_e, n_used, xs_rows, l, moe_w1, moe_b1[:, :, None, :],
                         moe_w2, moe_b2[:, :, None, :])
        fg = final_g.reshape(1, d) if final else None
        outp = _combine(x1p, rtp, tab_p, yb, mod_all[l], grp_p, fg)
        outs = _combine(x1s, rts, tab_s, yb, mod_all[l], grp_s, fg)
        return outp, outs

    def router_w(l):
        wr = jnp.pad(moe_w_router[l], ((0, 0), (0, LANE - N_EXPERTS)))
        br = jnp.pad(moe_b_router[l], (0, LANE - N_EXPERTS)).reshape(1, LANE)
        return wr, br

    l, e = 0, 0
    w_in = a_w_in[e]
    kr_cols = w_in[:, 384:416][:, perm_a]
    w0 = jnp.concatenate([
        w_in[:, :384], jnp.pad(kr_cols, ((0, 0), (0, LANE - MLA_ROPE))),
        _pad_heads(w_in[:, 416:928], NA_HEADS, NA_HD),
        _pad_heads(w_in[:, 928:1440], NA_HEADS, NA_HD),
        _pad_heads(w_in[:, 1440:1952], NA_HEADS, NA_HD)], axis=1).astype(BF16)
    wuq = a_w_uq[e].reshape(MLA_Q_LORA, MLA_HEADS, MLA_NOPE + MLA_ROPE)
    wuq = jnp.concatenate([wuq[:, :, :MLA_NOPE], wuq[:, :, MLA_NOPE:][:, :, perm_a]], axis=2)
    wuq = _pad_heads(wuq.reshape(MLA_Q_LORA, -1), MLA_HEADS, MLA_NOPE + MLA_ROPE).astype(BF16)
    wukv = a_w_ukv[e].reshape(MLA_KV_LORA, MLA_HEADS, MLA_NOPE + MLA_V)
    wk = _pad_heads(wukv[:, :, :MLA_NOPE].reshape(MLA_KV_LORA, -1), MLA_HEADS, MLA_NOPE).astype(BF16)
    wv = _pad_heads(wukv[:, :, MLA_NOPE:].reshape(MLA_KV_LORA, -1), MLA_HEADS, MLA_V).astype(BF16)
    sel_np = np.zeros((LANE, MLA_HEADS * LANE), np.float32)
    for hd in range(MLA_HEADS):
        sel_np[np.arange(MLA_ROPE), hd * LANE + MLA_NOPE + np.arange(MLA_ROPE)] = 1.0
    sel = jnp.asarray(sel_np, BF16)
    qn_g = a_q_norm[e].reshape(1, -1)
    kvn_g = a_kv_norm[e].reshape(1, -1)
    g1 = norm_g[l, 0].reshape(1, d)
    g2 = norm_g[l, 1].reshape(1, d)
    tabs_a = (_rope_tables(ts, MLA_ROPE, MLA_NOPE), _rope_tables(ts, MLA_ROPE, 0))
    wide = MLA_HEADS * LANE

    q_p, ckv_p, kr_p, nq_p, nk_p, nv_p = _even_proj(xp, g1, mod_all[l], w0, qn_g, wuq, kvn_g, grp_p, None, F32)
    new_mla_ckv = ckv_p.reshape(bp, 1, lp, MLA_KV_LORA)
    new_mla_krope = kr_p[:, :MLA_ROPE][:, inv_a].reshape(bp, 1, lp, MLA_ROPE)
    new_na_k = nk_p.reshape(bp, 1, lp, NA_HEADS, LANE)[..., :NA_HD]
    new_na_v = nv_p.reshape(bp, 1, lp, NA_HEADS, LANE)[..., :NA_HD]
    k_p, v_p = _kv_expand(ckv_p, kr_p, wk, sel, wv)
    ident = lambda hd: hd
    o_mla_p = _flash(q_p.reshape(bp, lp, wide), k_p.reshape(bp, lp, wide), v_p.reshape(bp, lp, wide),
                     v_of_head=ident, out_dtype=BF16)
    o_na_p = _flash(nq_p.reshape(bp, lp, wide), nk_p.astype(BF16).reshape(bp, lp, wide),
                    nv_p.astype(BF16).reshape(bp, lp, wide),
                    v_of_head=ident, out_dtype=BF16)
    q_s, ckv_s, kr_s, nq_s, nk_s, nv_s = _even_proj(xs, g1, mod_all[l], w0, qn_g, wuq, kvn_g, grp_s, tabs_a, BF16)
    ckv_all = jnp.concatenate([cache_mla_ckv[:, e], ckv_s.reshape(bs, ts, LANE)], axis=1)
    kr_cache = jnp.pad(cache_mla_krope[:, e][..., perm_a], ((0, 0), (0, 0), (0, LANE - MLA_ROPE)))
    kr_all = jnp.concatenate([kr_cache, kr_s.reshape(bs, ts, LANE)], axis=1)
    n_all = past + ts
    k_s, v_s = _kv_expand(ckv_all.reshape(bs * n_all, LANE), kr_all.reshape(bs * n_all, LANE), wk, sel, wv)
    o_mla_s = _flash(q_s.reshape(bs, ts, wide), k_s.reshape(bs, n_all, wide), v_s.reshape(bs, n_all, wide),
                     v_of_head=ident, out_dtype=BF16)
    pad_hd = ((0, 0), (0, 0), (0, 0), (0, LANE - NA_HD))
    kc = jnp.pad(cache_na_k[:, e], pad_hd).astype(BF16).reshape(bs, past, wide)
    vc = jnp.pad(cache_na_v[:, e], pad_hd).astype(BF16).reshape(bs, past, wide)
    o_na_s = _na_latent(nq_s.reshape(bs, ts, wide), nk_s.reshape(bs, ts, wide), nv_s.reshape(bs, ts, wide),
                        kc, vc, _na_bias_table(a_rpb[e]))
    w_out = a_w_out[e]
    wa = _pad_rows(w_out[:MLA_HEADS * MLA_V], MLA_HEADS, MLA_V).astype(BF16)
    wb = _pad_rows(w_out[MLA_HEADS * MLA_V:], NA_HEADS, NA_HD).astype(BF16)
    wr, br = router_w(l)
    cnt0 = jnp.zeros((1, LANE), F32)
    x1p, h2p, rtp, tbp, cnt_p = _post_mix(_post_even_kernel, xp,
                                     [o_mla_p.reshape(n_p, wide), o_na_p.reshape(n_p, wide)],
                                     [wa, wb], mod_all[l], g2, wr, br, cnt0, grp_p, "post_even")
    x1s, h2s, rts, tbs, cnt_s = _post_mix(_post_even_kernel, xs,
                                     [o_mla_s.reshape(n_s, wide), o_na_s.reshape(n_s, wide)],
                                     [wa, wb], mod_all[l], g2, wr, br, cnt_p, grp_s, "post_even")
    (xp,), (xs,) = moe_layer(l, x1p, x1s, h2p, h2s, rtp, rts, tbp, tbs, cnt_s, False)

    l, o = 1, 0
    lam_init = 0.8 - 0.6 * math.exp(-0.3 * l)
    w_in = b_w_in[o]
    cuts = np.cumsum([0, 256, 256, 512, 512, 16, 512, 512, 512])
    seg = lambda i: w_in[:, cuts[i]:cuts[i + 1]]

    def diff_cols(w):
        w = w.reshape(d, 2 * DF_HEADS, DF_QK)[:, :, perm_d]
        return _pad_heads(w.reshape(d, -1), 2 * DF_HEADS, DF_QK)

    w1p = jnp.concatenate([
        _pad_heads(seg(0), ML_HEADS, ML_DQK), _pad_heads(seg(1), ML_HEADS, ML_DQK), seg(2), seg(3),
        diff_cols(seg(5)), diff_cols(seg(6)), seg(7),
        jnp.pad(seg(4), ((0, 0), (0, LANE - 16)))], axis=1).astype(BF16)
    bg = jnp.pad(b_gate_bias[o].reshape(1, 16), ((0, 0), (0, LANE - 16)))
    g1 = norm_g[l, 0].reshape(1, d)
    g2 = norm_g[l, 1].reshape(1, d)
    tab_d = _rope_tables(ts, DF_QK, 0)
    df_v_of = lambda hd: hd // 2

    mq_p, mk_p, mv_p, mo_p, gt_p, dq_p, dk_p, dv_p = _odd_proj(xp, g1, mod_all[l], w1p, bg, grp_p, None, F32)
    new_diff_k = dk_p.reshape(bp, 1, lp, DF_HEADS, 2, LANE)[..., :DF_QK][..., inv_d]
    new_diff_v = dv_p.reshape(bp, 1, lp, DF_HEADS, DF_V)
    mw = ML_HEADS * LANE
    zc = jnp.zeros((bp, ML_HEADS, LANE, LANE), F32)
    zn = jnp.zeros((bp, ML_HEADS, 1, LANE), F32)
    seq_p = lambda a: a.reshape(bp, lp, -1)
    hf_p, cf_f, nf_f, mf_f = _mlstm_dir(seq_p(mq_p), seq_p(mk_p), seq_p(mv_p), seq_p(gt_p), zc, zn, zn,
                                        direction=0, bb=4)
    hb_p, cf_b, nf_b, mf_b = _mlstm_dir(seq_p(mq_p), seq_p(mk_p), seq_p(mv_p), seq_p(gt_p), zc, zn, zn,
                                        direction=1, bb=4)
    new_mlstm_C = jnp.stack([cf_f, cf_b], axis=1)[..., :ML_DQK][:, None]
    new_mlstm_n = jnp.stack([nf_f, nf_b], axis=1)[:, :, :, 0, :ML_DQK][:, None]
    new_mlstm_m = jnp.stack([mf_f, mf_b], axis=1)[:, :, :, 0, 0][:, None]
    od_p = _flash(seq_p(dq_p), seq_p(dk_p.astype(BF16)), seq_p(dv_p.astype(BF16)),
                  v_of_head=df_v_of, out_dtype=F32)

    mq_s, mk_s, mv_s, mo_s, gt_s, dq_s, dk_s, dv_s = _odd_proj(xs, g1, mod_all[l], w1p, bg, grp_s, tab_d, BF16)
    seq_s = lambda a: a.reshape(bs, ts, -1)
    c0 = jnp.pad(state_mlstm_C[:, o], ((0, 0), (0, 0), (0, 0), (0, 0), (0, LANE - ML_DQK)))
    n0 = jnp.pad(state_mlstm_n[:, o], ((0, 0), (0, 0), (0, 0), (0, LANE - ML_DQK)))[:, :, :, None, :]
    m0 = jnp.broadcast_to(state_mlstm_m[:, o][:, :, :, None, None], (bs, 2, ML_HEADS, 1, LANE))
    hf_s = _mlstm_dir(seq_s(mq_s), seq_s(mk_s), seq_s(mv_s), seq_s(gt_s), c0[:, 0], n0[:, 0], m0[:, 0],
                      direction=0, bb=4)[0]
    hb_s = _mlstm_dir(seq_s(mq_s), seq_s(mk_s), seq_s(mv_s), seq_s(gt_s), c0[:, 1], n0[:, 1], m0[:, 1],
                      direction=1, bb=4)[0]
    dk_cache = cache_diff_k[:, o][..., perm_d]
    dk_cache = jnp.pad(dk_cache, ((0, 0),) * 4 + ((0, LANE - DF_QK),)).astype(BF16).reshape(bs, past, 1024)
    dk_all = jnp.concatenate([dk_cache, seq_s(dk_s)], axis=1)
    dv_all = jnp.concatenate([cache_diff_v[:, o].astype(BF16).reshape(bs, past, 512), seq_s(dv_s)], axis=1)
    od_s = _flash(seq_s(dq_s), dk_all, dv_all, v_of_head=df_v_of, out_dtype=F32)

    w_out = b_w_out[o]
    wa = w_out[:ML_HEADS * ML_DV].astype(BF16)
    wb = w_out[ML_HEADS * ML_DV:].astype(BF16)
    mln = b_ml_norm[o].reshape(1, ML_HEADS * ML_DV)
    sub = b_df_subln[o].reshape(1, DF_V)
    lam_p = jnp.pad(b_df_lambda[o], ((0, 4), (0, LANE - DF_QK)))
    wr, br = router_w(l)
    post_odd = functools.partial(_post_odd_kernel, lam_init=lam_init)
    x1p, h2p, rtp, tbp, cnt_p = _post_mix(post_odd, xp, [hf_p.reshape(n_p, mw), hb_p.reshape(n_p, mw), mo_p,
                                                    od_p.reshape(n_p, 1024)],
                                     [mln, sub, lam_p, wa, wb], mod_all[l], g2, wr, br, cnt0, grp_p, "post_odd")
    x1s, h2s, rts, tbs, cnt_s = _post_mix(post_odd, xs, [hf_s.reshape(n_s, mw), hb_s.reshape(n_s, mw), mo_s,
                                                    od_s.reshape(n_s, 1024)],
                                     [mln, sub, lam_p, wa, wb], mod_all[l], g2, wr, br, cnt_p, grp_s, "post_odd")
    (_, y_p), (_, y_s) = moe_layer(l, x1p, x1s, h2p, h2s, rtp, rts, tbp, tbs, cnt_s, True)

    return (y_p.reshape(bp, lp, d), y_s.reshape(bs, ts, d), new_mla_ckv, new_mla_krope, new_na_k, new_na_v,
            new_mlstm_C, new_mlstm_n, new_mlstm_m, new_diff_k, new_diff_v)
```

```python
import functools
import math

import numpy as np
import jax
import jax.numpy as jnp
from jax import lax
from jax.experimental import pallas as pl
from jax.experimental.pallas import tpu as pltpu

F32 = jnp.float32
BF16 = jnp.bfloat16
HIGHEST = lax.Precision.HIGHEST

D_MODEL = 1024
GRID_W = 64
LANE = 128
EPS = 1e-6
ROPE_THETA = 10000.0

MLA_HEADS, MLA_Q_LORA, MLA_KV_LORA, MLA_NOPE, MLA_ROPE, MLA_V = 8, 256, 128, 64, 32, 64
NA_HEADS, NA_HD, NA_KR, NA_KW = 8, 64, 8, 16
ML_HEADS, ML_DQK, ML_DV, ML_CHUNK = 4, 64, 128, 64
DF_HEADS, DF_QK, DF_V = 4, 64, 128
N_EXPERTS, TOP_K, D_FF = 32, 4, 1024
SWIGLU_LIMIT, SWIGLU_ALPHA = 7.0, 1.702
MLA_SCALE = (MLA_NOPE + MLA_ROPE) ** -0.5

ROW_TILE = 256
POST_TILES = 2
MOE_ROWS = 512
ATT_TQ, ATT_TK = 2048, 256
MXU_AHEAD = 2
FF_CHUNK = 512
NEG_BIG = -1e30
VMEM_LIMIT = 56 * 1024 * 1024


def _cparams(*sem):
    return pltpu.CompilerParams(dimension_semantics=sem, vmem_limit_bytes=VMEM_LIMIT)


def _rms(x, g):
    return x * lax.rsqrt(jnp.mean(x * x, axis=-1, keepdims=True) + EPS) * g


def _sigmoid(x):
    return 1.0 / (1.0 + jnp.exp(-x))


def _mod_kernel(c_ref, w_ref, b_ref, o_ref):
    c = c_ref[...]
    s = c * _sigmoid(c)
    o_ref[0] = jnp.dot(s, w_ref[0], precision=HIGHEST, preferred_element_type=F32) + b_ref[0]


def _modulation(cond8, w_mod, b_mod):
    depth, d, n6 = w_mod.shape
    tn = 1024
    out = pl.pallas_call(
        _mod_kernel,
        grid=(depth, n6 // tn),
        in_specs=[
            pl.BlockSpec((8, d), lambda l, j: (0, 0)),
            pl.BlockSpec((1, d, tn), lambda l, j: (l, 0, j)),
            pl.BlockSpec((1, 1, tn), lambda l, j: (l, 0, j)),
        ],
        out_specs=pl.BlockSpec((1, 8, tn), lambda l, j: (l, 0, j)),
        out_shape=jax.ShapeDtypeStruct((depth, 8, n6), F32),
        compiler_params=_cparams("parallel", "parallel"),
        name="modulation",
    )(cond8, w_mod, b_mod.reshape(depth, 1, n6))
    return out.reshape(depth, 8, 6, d)


def _rope_block(x, cos, sin_a, sin_b, half):
    up = pltpu.roll(x, LANE - half, axis=1)
    dn = pltpu.roll(x, half, axis=1)
    return x * cos + up * sin_a + dn * sin_b


def _even_proj_kernel(*refs, rope):
    if rope:
        (x_ref, g_ref, mod_ref, w_ref, qn_ref, wuq_ref, kvn_ref, tq_ref, tk_ref,
         q_ref, ckv_ref, kr_ref, nq_ref, nk_ref, nv_ref) = refs
    else:
        (x_ref, g_ref, mod_ref, w_ref, qn_ref, wuq_ref, kvn_ref,
         q_ref, ckv_ref, kr_ref, nq_ref, nk_ref, nv_ref) = refs
    x = x_ref[...]
    h = _rms(x, g_ref[...]) * (1.0 + mod_ref[0, 1:2, :]) + mod_ref[0, 0:1, :]
    hb = h.astype(BF16)

    def proj(a, b):
        return jnp.dot(hb, w_ref[:, a:b], preferred_element_type=F32)

    q_lat = proj(0, 256)
    qn = _rms(q_lat, qn_ref[...]).astype(BF16)
    for hd in range(MLA_HEADS):
        qh = jnp.dot(qn, wuq_ref[:, hd * LANE:(hd + 1) * LANE], preferred_element_type=F32)
        if rope:
            qh = _rope_block(qh, tq_ref[0], tq_ref[1], tq_ref[2], MLA_ROPE // 2)
        q_ref[:, hd * LANE:(hd + 1) * LANE] = (qh * _exp2_scale(MLA_SCALE)).astype(q_ref.dtype)
    ckv_ref[...] = _rms(proj(256, 384), kvn_ref[...])
    kr = proj(384, 512)
    if rope:
        kr = _rope_block(kr, tk_ref[0], tk_ref[1], tk_ref[2], MLA_ROPE // 2)
    kr_ref[...] = kr
    nq = proj(512, 1536) if rope else proj(512, 1536) * _exp2_scale(NA_HD ** -0.5)
    nq_ref[...] = nq.astype(nq_ref.dtype)
    nk_ref[...] = proj(1536, 2560).astype(nk_ref.dtype)
    nv_ref[...] = proj(2560, 3584).astype(nv_ref.dtype)


def _even_proj(x, g, mod, w, qn, wuq, kvn, group_of_tile, rope_tabs, kv_dtype):
    n, d = x.shape
    tm = ROW_TILE
    rope = rope_tabs is not None
    in_specs = [
        pl.BlockSpec((tm, d), lambda i: (i, 0)),
        pl.BlockSpec((1, d), lambda i: (0, 0)),
        pl.BlockSpec((1, 6, d), lambda i: (group_of_tile(i), 0, 0)),
        pl.BlockSpec(w.shape, lambda i: (0, 0)),
        pl.BlockSpec(qn.shape, lambda i: (0, 0)),
        pl.BlockSpec(wuq.shape, lambda i: (0, 0)),
        pl.BlockSpec(kvn.shape, lambda i: (0, 0)),
    ]
    args = [x, g, mod, w, qn, wuq, kvn]
    if rope:
        tq, tk = rope_tabs
        nblk = tq.shape[1] // tm
        in_specs += [pl.BlockSpec((3, tm, LANE), lambda i: (0, i % nblk, 0)),
                     pl.BlockSpec((3, tm, LANE), lambda i: (0, i % nblk, 0))]
        args += [tq, tk]
    wide = MLA_HEADS * LANE
    out_shape = [
        jax.ShapeDtypeStruct((n, wide), BF16),
        jax.ShapeDtypeStruct((n, LANE), F32),
        jax.ShapeDtypeStruct((n, LANE), F32),
        jax.ShapeDtypeStruct((n, wide), BF16),
        jax.ShapeDtypeStruct((n, wide), kv_dtype),
        jax.ShapeDtypeStruct((n, wide), kv_dtype),
    ]
    out_specs = [pl.BlockSpec((tm, s.shape[1]), lambda i: (i, 0)) for s in out_shape]
    return pl.pallas_call(
        functools.partial(_even_proj_kernel, rope=rope),
        grid=(n // tm,),
        in_specs=in_specs,
        out_specs=out_specs,
        out_shape=out_shape,
        compiler_params=_cparams("parallel"),
        name="even_proj_rope" if rope else "even_proj",
    )(*args)


def _kv_expand_kernel(ckv_ref, kr_ref, wk_ref, sel_ref, wv_ref, k_ref, v_ref):
    c = ckv_ref[...].astype(BF16)
    r = kr_ref[...].astype(BF16)
    k = jnp.dot(c, wk_ref[...], preferred_element_type=F32)
    k = k + jnp.dot(r, sel_ref[...], preferred_element_type=F32)
    k_ref[...] = k.astype(BF16)
    v_ref[...] = jnp.dot(c, wv_ref[...], preferred_element_type=F32).astype(BF16)


def _kv_expand(ckv, kr, wk, sel, wv):
    n = ckv.shape[0]
    tm = ROW_TILE
    wide = MLA_HEADS * LANE
    const = lambda i: (0, 0)
    return pl.pallas_call(
        _kv_expand_kernel,
        grid=(n // tm,),
        in_specs=[pl.BlockSpec((tm, LANE), lambda i: (i, 0)),
                  pl.BlockSpec((tm, LANE), lambda i: (i, 0)),
                  pl.BlockSpec(wk.shape, const), pl.BlockSpec(sel.shape, const),
                  pl.BlockSpec(wv.shape, const)],
        out_specs=[pl.BlockSpec((tm, wide), lambda i: (i, 0))] * 2,
        out_shape=[jax.ShapeDtypeStruct((n, wide), BF16)] * 2,
        compiler_params=_cparams("parallel"),
        name="kv_expand",
    )(ckv, kr, wk, sel, wv)


def _exp2_scale(scale):
    return scale * math.log2(math.e)


def _flash_kernel(q_ref, k_ref, v_ref, o_ref, m_sc, l_sc, acc_sc, *, heads, v_of_head):
    kj = pl.program_id(2)

    @pl.when(kj == 0)
    def _():
        m_sc[...] = jnp.full(m_sc.shape, NEG_BIG, F32)
        l_sc[...] = jnp.zeros(l_sc.shape, F32)
        acc_sc[...] = jnp.zeros(acc_sc.shape, F32)

    def scores(hd):
        sl = slice(hd * LANE, (hd + 1) * LANE)
        return lax.dot_general(k_ref[0, :, sl], q_ref[0, :, sl], (((1,), (1,)), ((), ())),
                               preferred_element_type=F32)

    pending = [scores(hd) for hd in range(min(MXU_AHEAD, heads))]
    for hd in range(heads):
        sl = slice(hd * LANE, (hd + 1) * LANE)
        vh = v_of_head(hd)
        s_t = pending.pop(0)
        if hd + MXU_AHEAD < heads:
            pending.append(scores(hd + MXU_AHEAD))
        m_prev = m_sc[hd]
        m_new = jnp.maximum(m_prev, jnp.max(s_t, axis=0, keepdims=True))
        alpha = jnp.exp2(m_prev - m_new)
        p_t = jnp.exp2(s_t - m_new)
        l_sc[hd] = alpha * l_sc[hd] + jnp.sum(p_t, axis=0, keepdims=True)
        pv_t = lax.dot_general(v_ref[0, :, vh * LANE:(vh + 1) * LANE], p_t.astype(BF16),
                               (((0,), (0,)), ((), ())), preferred_element_type=F32)
        acc_sc[sl, :] = alpha * acc_sc[sl, :] + pv_t
        m_sc[hd] = m_new

    @pl.when(kj == pl.num_programs(2) - 1)
    def _():
        for hd in range(heads):
            sl = slice(hd * LANE, (hd + 1) * LANE)
            o_ref[0, :, sl] = (acc_sc[sl, :] / l_sc[hd]).T.astype(o_ref.dtype)


def _flash(q, k, v, *, v_of_head, out_dtype):
    b, s, wq = q.shape
    nk = k.shape[1]
    wv = v.shape[2]
    heads = wq // LANE
    tq = min(ATT_TQ, s)
    tk = min(ATT_TK, nk)
    return pl.pallas_call(
        functools.partial(_flash_kernel, heads=heads, v_of_head=v_of_head),
        grid=(b, s // tq, nk // tk),
        in_specs=[pl.BlockSpec((1, tq, wq), lambda bi, qi, ki: (bi, qi, 0)),
                  pl.BlockSpec((1, tk, wq), lambda bi, qi, ki: (bi, ki, 0)),
                  pl.BlockSpec((1, tk, wv), lambda bi, qi, ki: (bi, ki, 0))],
        out_specs=pl.BlockSpec((1, tq, wq), lambda bi, qi, ki: (bi, qi, 0)),
        out_shape=jax.ShapeDtypeStruct((b, s, wq), out_dtype),
        scratch_shapes=[pltpu.VMEM((heads, 1, tq), F32), pltpu.VMEM((heads, 1, tq), F32),
                        pltpu.VMEM((wq, tq), F32)],
        compiler_params=_cparams("parallel", "parallel", "arbitrary"),
        name="flash_attention",
    )(q, k, v)


def _na_kernel(q_ref, k_ref, v_ref, kc_ref, vc_ref, bias_ref, o_ref, *, rows, scale):
    r = pl.program_id(1)
    r0 = jnp.clip(r - NA_KR // 2, 0, rows - NA_KR)
    start = pl.multiple_of(r0 * GRID_W, GRID_W)
    nloc = NA_KR * GRID_W
    dn = (((1,), (1,)), ((), ()))

    def scores(hd):
        sl = slice(hd * LANE, (hd + 1) * LANE)
        qh = q_ref[0, :, sl]
        kh = k_ref[0, pl.ds(start, nloc), sl]
        return (lax.dot_general(qh, kh, dn, preferred_element_type=F32),
                lax.dot_general(qh, kc_ref[0, :, sl], dn, preferred_element_type=F32))

    ahead = 2 * MXU_AHEAD
    pending = [scores(hd) for hd in range(min(ahead, NA_HEADS))]
    for hd in range(NA_HEADS):
        sl = slice(hd * LANE, (hd + 1) * LANE)
        s_loc, s_ctx = pending.pop(0)
        if hd + ahead < NA_HEADS:
            pending.append(scores(hd + ahead))
        vh = v_ref[0, pl.ds(start, nloc), sl]
        s_loc = s_loc * scale + bias_ref[hd, 0]
        s_ctx = s_ctx * scale
        m = jnp.maximum(jnp.max(s_loc, axis=1, keepdims=True), jnp.max(s_ctx, axis=1, keepdims=True))
        p_loc = jnp.exp(s_loc - m)
        p_ctx = jnp.exp(s_ctx - m)
        l = jnp.sum(p_loc, axis=1, keepdims=True) + jnp.sum(p_ctx, axis=1, keepdims=True)
        o = (jnp.dot(p_loc.astype(BF16), vh, preferred_element_type=F32)
             + jnp.dot(p_ctx.astype(BF16), vc_ref[0, :, sl], preferred_element_type=F32))
        o_ref[0, :, sl] = (o / l).astype(o_ref.dtype)


def _na_latent(q, k, v, kc, vc, bias_tab):
    b, s, wide = q.shape
    rows = s // GRID_W
    lc = kc.shape[1]

    def bias_idx(bi, r):
        return (0, jnp.clip(r - NA_KR // 2, 0, rows - NA_KR) - r + NA_KR - 1, 0, 0)

    return pl.pallas_call(
        functools.partial(_na_kernel, rows=rows, scale=NA_HD ** -0.5),
        grid=(b, rows),
        in_specs=[pl.BlockSpec((1, GRID_W, wide), lambda bi, r: (bi, r, 0)),
                  pl.BlockSpec((1, s, wide), lambda bi, r: (bi, 0, 0)),
                  pl.BlockSpec((1, s, wide), lambda bi, r: (bi, 0, 0)),
                  pl.BlockSpec((1, lc, wide), lambda bi, r: (bi, 0, 0)),
                  pl.BlockSpec((1, lc, wide), lambda bi, r: (bi, 0, 0)),
                  pl.BlockSpec((NA_HEADS, 1, GRID_W, NA_KR * GRID_W), bias_idx)],
        out_specs=pl.BlockSpec((1, GRID_W, wide), lambda bi, r: (bi, r, 0)),
        out_shape=jax.ShapeDtypeStruct((b, s, wide), BF16),
        compiler_params=_cparams("parallel", "arbitrary"),
        name="na_latent",
    )(q, k, v, kc, vc, bias_tab)


def _odd_proj_kernel(*refs, rope):
    if rope:
        (x_ref, g_ref, mod_ref, w_ref, bg_ref, td_ref,
         mq_ref, mk_ref, mv_ref, mo_ref, gt_ref, dq_ref, dk_ref, dv_ref) = refs
    else:
        (x_ref, g_ref, mod_ref, w_ref, bg_ref,
         mq_ref, mk_ref, mv_ref, mo_ref, gt_ref, dq_ref, dk_ref, dv_ref) = refs
    x = x_ref[...]
    h = _rms(x, g_ref[...]) * (1.0 + mod_ref[0, 1:2, :]) + mod_ref[0, 0:1, :]
    hb = h.astype(BF16)

    def proj(a, b):
        return jnp.dot(hb, w_ref[:, a:b], preferred_element_type=F32)

    mq_ref[...] = proj(0, 512).astype(mq_ref.dtype)
    mk_ref[...] = (proj(512, 1024) * (ML_DQK ** -0.5)).astype(mk_ref.dtype)
    mv_ref[...] = proj(1024, 1536).astype(mv_ref.dtype)
    mo_ref[...] = proj(1536, 2048)
    for blk in range(8):
        qd = proj(2048 + blk * LANE, 2048 + (blk + 1) * LANE)
        kd = proj(3072 + blk * LANE, 3072 + (blk + 1) * LANE)
        if rope:
            qd = _rope_block(qd, td_ref[0], td_ref[1], td_ref[2], DF_QK // 2)
            kd = _rope_block(kd, td_ref[0], td_ref[1], td_ref[2], DF_QK // 2)
        dq_ref[:, blk * LANE:(blk + 1) * LANE] = (qd * _exp2_scale(DF_QK ** -0.5)).astype(dq_ref.dtype)
        dk_ref[:, blk * LANE:(blk + 1) * LANE] = kd.astype(dk_ref.dtype)
    dv_ref[...] = proj(4096, 4608).astype(dv_ref.dtype)
    gt = proj(4608, 4736) + bg_ref[...]
    lane = lax.broadcasted_iota(jnp.int32, gt.shape, 1)
    log_f = jnp.minimum(gt, 0.0) - jnp.log(1.0 + jnp.exp(-jnp.abs(gt)))
    gt_ref[...] = jnp.where((lane % 8) >= 4, log_f, gt)


def _odd_proj(x, g, mod, w, bg, group_of_tile, rope_tab, kv_dtype):
    n, d = x.shape
    tm = ROW_TILE
    rope = rope_tab is not None
    in_specs = [
        pl.BlockSpec((tm, d), lambda i: (i, 0)),
        pl.BlockSpec((1, d), lambda i: (0, 0)),
        pl.BlockSpec((1, 6, d), lambda i: (group_of_tile(i), 0, 0)),
        pl.BlockSpec(w.shape, lambda i: (0, 0)),
        pl.BlockSpec(bg.shape, lambda i: (0, 0)),
    ]
    args = [x, g, mod, w, bg]
    if rope:
        nblk = rope_tab.shape[1] // tm
        in_specs.append(pl.BlockSpec((3, tm, LANE), lambda i: (0, i % nblk, 0)))
        args.append(rope_tab)
    out_shape = [
        jax.ShapeDtypeStruct((n, 512), BF16),
        jax.ShapeDtypeStruct((n, 512), BF16),
        jax.ShapeDtypeStruct((n, 512), BF16),
        jax.ShapeDtypeStruct((n, 512), F32),
        jax.ShapeDtypeStruct((n, LANE), F32),
        jax.ShapeDtypeStruct((n, 1024), BF16),
        jax.ShapeDtypeStruct((n, 1024), kv_dtype),
        jax.ShapeDtypeStruct((n, 512), kv_dtype),
    ]
    out_specs = [pl.BlockSpec((tm, s.shape[1]), lambda i: (i, 0)) for s in out_shape]
    return pl.pallas_call(
        functools.partial(_odd_proj_kernel, rope=rope),
        grid=(n // tm,),
        in_specs=in_specs,
        out_specs=out_specs,
        out_shape=out_shape,
        compiler_params=_cparams("parallel"),
        name="odd_proj_rope" if rope else "odd_proj",
    )(*args)


def _mlstm_kernel(q_ref, k_ref, v_ref, g_ref, c0_ref, n0_ref, m0_ref,
                  h_ref, cf_ref, nf_ref, mf_ref, c_sc, n_sc, m_sc, *, bb, direction, reverse):
    step = pl.program_id(1)
    L = ML_CHUNK

    @pl.when(step == 0)
    def _():
        c_sc[...] = c0_ref[...]
        n_sc[...] = n0_ref[...]
        m_sc[...] = m0_ref[...]

    t_idx = lax.broadcasted_iota(jnp.int32, (L, L), 0)
    s_idx = lax.broadcasted_iota(jnp.int32, (L, L), 1)
    keep = (s_idx >= t_idx) if reverse else (s_idx <= t_idx)
    tri = keep.astype(F32)
    last = 0 if reverse else L - 1
    for bi in range(bb):
        g = g_ref[bi]
        bcum = jnp.dot(tri, g, precision=HIGHEST, preferred_element_type=F32)
        g_t = g.T
        b_t = bcum.T
        for hd in range(ML_HEADS):
            sl = slice(hd * LANE, (hd + 1) * LANE)
            ci = direction * 8 + hd
            cf = direction * 8 + 4 + hd
            li_row = g_t[ci:ci + 1, :]
            b_row = b_t[cf:cf + 1, :]
            li_col = g[:, ci:ci + 1]
            b_col = bcum[:, cf:cf + 1]
            m_old = m_sc[bi, hd][:, 0:1]
            n_old = n_sc[bi, hd]
            c_old = c_sc[bi, hd]
            qh = q_ref[bi, :, sl]
            kh = k_ref[bi, :, sl]
            vh = v_ref[bi, :, sl]
            dm = jnp.where(keep, b_col - b_row + li_row, NEG_BIG)
            m_inter = b_col + m_old
            m_t = jnp.maximum(m_inter, jnp.max(dm, axis=1, keepdims=True))
            qk = lax.dot_general(qh, kh, (((1,), (1,)), ((), ())), preferred_element_type=F32)
            w_intra = jnp.where(keep, jnp.exp(dm - m_t), 0.0) * qk
            w_inter = jnp.exp(m_inter - m_t)
            qc = lax.dot_general(qh, c_old.astype(BF16), (((1,), (1,)), ((), ())),
                                 preferred_element_type=F32)
            num = jnp.dot(w_intra.astype(BF16), vh, preferred_element_type=F32) + w_inter * qc
            qn = jnp.sum(qh.astype(F32) * n_old, axis=1, keepdims=True)
            den = jnp.sum(w_intra, axis=1, keepdims=True) + w_inter * qn
            h_ref[bi, :, sl] = num / jnp.maximum(jnp.abs(den), jnp.exp(-m_t))
            m_new = m_t[last:last + 1, :]
            b_last = b_col[last:last + 1, :]
            w_end = jnp.exp(b_last - b_col + li_col - m_new)
            decay = jnp.exp(b_last + m_old - m_new)
            wv = (w_end * vh.astype(F32)).astype(BF16)
            upd = lax.dot_general(wv, kh, (((0,), (0,)), ((), ())), preferred_element_type=F32)
            c_sc[bi, hd] = decay * c_old + upd
            n_sc[bi, hd] = decay * n_old + jnp.sum(w_end * kh.astype(F32), axis=0, keepdims=True)
            m_sc[bi, hd] = jnp.broadcast_to(m_new, (1, LANE))

    @pl.when(step == pl.num_programs(1) - 1)
    def _():
        cf_ref[...] = c_sc[...]
        nf_ref[...] = n_sc[...]
        mf_ref[...] = m_sc[...]


def _mlstm_dir(q, k, v, gates, c0, n0, m0, *, direction, bb):
    b, t, w = q.shape
    nc = t // ML_CHUNK
    reverse = direction == 1
    if reverse:
        cidx = lambda bi, c: (bi, nc - 1 - c, 0)
    else:
        cidx = lambda bi, c: (bi, c, 0)
    st4 = lambda bi, c: (bi, 0, 0, 0)
    seq = pl.BlockSpec((bb, ML_CHUNK, w), cidx)
    c_spec = pl.BlockSpec((bb, ML_HEADS, LANE, LANE), st4)
    n_spec = pl.BlockSpec((bb, ML_HEADS, 1, LANE), st4)
    return pl.pallas_call(
        functools.partial(_mlstm_kernel, bb=bb, direction=direction, reverse=reverse),
        grid=(b // bb, nc),
        in_specs=[seq, seq, seq, pl.BlockSpec((bb, ML_CHUNK, LANE), cidx), c_spec, n_spec, n_spec],
        out_specs=[seq, c_spec, n_spec, n_spec],
        out_shape=[jax.ShapeDtypeStruct((b, t, w), F32),
                   jax.ShapeDtypeStruct(c0.shape, F32),
                   jax.ShapeDtypeStruct(n0.shape, F32),
                   jax.ShapeDtypeStruct(m0.shape, F32)],
        scratch_shapes=[pltpu.VMEM((bb, ML_HEADS, LANE, LANE), F32),
                        pltpu.VMEM((bb, ML_HEADS, 1, LANE), F32),
                        pltpu.VMEM((bb, ML_HEADS, 1, LANE), F32)],
        compiler_params=_cparams("parallel", "arbitrary"),
        name="mlstm_bwd" if reverse else "mlstm_fwd",
    )(q, k, v, gates, c0, n0, m0)


ROUTE_IDX, ROUTE_ROW, ROUTE_GATE = 0, 4, 8
TAB_LEN, TAB_OFF, TAB_BASE = 0, 1, 2
SUBLANES = 8


def _post_tail(x_ref, mix, mod_ref, g2_ref, wr_ref, br_ref, cin_ref,
               x_out_ref, h_ref, rt_ref, tab_ref, cout_ref, cnt_sc):
    @pl.when(pl.program_id(0) == 0)
    def _():
        cnt_sc[...] = cin_ref[...]

    x1 = x_ref[...] + mod_ref[0, 2:3, :] * mix
    x_out_ref[...] = x1
    h2 = _rms(x1, g2_ref[...]) * (1.0 + mod_ref[0, 4:5, :]) + mod_ref[0, 3:4, :]
    h_ref[...] = h2
    logits = jnp.dot(h2, wr_ref[...], precision=HIGHEST, preferred_element_type=F32) + br_ref[...]
    tm = logits.shape[0]
    lane = lax.broadcasted_iota(jnp.int32, (tm, LANE), 1)
    lg = jnp.where(lane < N_EXPERTS, logits, NEG_BIG)
    vals, idxs, hots = [], [], []
    for _ in range(TOP_K):
        mk = jnp.max(lg, axis=1, keepdims=True)
        idx = jnp.min(jnp.where(lg == mk, lane, LANE), axis=1, keepdims=True)
        hot = lane == idx
        vals.append(mk)
        idxs.append(idx)
        hots.append(hot)
        lg = jnp.where(hot, NEG_BIG, lg)
    ex = [jnp.exp(v - vals[0]) for v in vals]
    tot = ex[0] + ex[1] + ex[2] + ex[3]
    sel = jnp.zeros((tm, LANE), F32)
    for hot in hots:
        sel = sel + hot.astype(F32)
    t_idx = lax.broadcasted_iota(jnp.int32, (ROW_TILE, ROW_TILE), 0)
    s_idx = lax.broadcasted_iota(jnp.int32, (ROW_TILE, ROW_TILE), 1)
    earlier = (s_idx < t_idx).astype(BF16)
    e_src = lax.broadcasted_iota(jnp.int32, (LANE, LANE), 0)
    e_dst = lax.broadcasted_iota(jnp.int32, (LANE, LANE), 1)
    before = (e_src < e_dst).astype(F32)
    sub = lax.broadcasted_iota(jnp.int32, (SUBLANES, LANE), 0)
    base = cnt_sc[...]
    local_parts = []
    for part in range(tm // ROW_TILE):
        sel_p = sel[part * ROW_TILE:(part + 1) * ROW_TILE]
        prefix = jnp.dot(earlier, sel_p.astype(BF16), preferred_element_type=F32)
        run_len = jnp.floor((jnp.sum(sel_p, axis=0, keepdims=True) + (SUBLANES - 1)) / SUBLANES) * SUBLANES
        run_off = jnp.dot(jnp.broadcast_to(run_len, (SUBLANES, LANE)), before, precision=HIGHEST,
                          preferred_element_type=F32)[0:1]
        local_parts.append(prefix + run_off)
        tab = jnp.where(sub == TAB_LEN, run_len, 0.0)
        tab = jnp.where(sub == TAB_OFF, run_off, tab)
        tab = jnp.where(sub == TAB_BASE, base, tab)
        tab_ref[part] = tab
        base = base + run_len
    cnt_sc[...] = base
    cout_ref[...] = base
    local = jnp.concatenate(local_parts, axis=0)
    route = jnp.zeros((tm, LANE), F32)
    for k in range(TOP_K):
        row = jnp.sum(jnp.where(hots[k], local, 0.0), axis=1, keepdims=True)
        route = jnp.where(lane == ROUTE_IDX + k, idxs[k].astype(F32), route)
        route = jnp.where(lane == ROUTE_ROW + k, row, route)
        route = jnp.where(lane == ROUTE_GATE + k, ex[k] / tot, route)
    rt_ref[...] = route


def _post_even_kernel(x_ref, a_ref, b_ref, wa_ref, wb_ref, *rest):
    mix = (jnp.dot(a_ref[...], wa_ref[...], preferred_element_type=F32)
           + jnp.dot(b_ref[...], wb_ref[...], preferred_element_type=F32))
    _post_tail(x_ref, mix, *rest)


def _post_odd_kernel(x_ref, hf_ref, hb_ref, mo_ref, od_ref, mln_ref, sub_ref, lam_ref,
                     wa_ref, wb_ref, *rest, lam_init):
    lp = lam_ref[...]
    lam = (jnp.exp(jnp.sum(lp[0:1] * lp[1:2], axis=1, keepdims=True))
           - jnp.exp(jnp.sum(lp[2:3] * lp[3:4], axis=1, keepdims=True)) + lam_init)
    mix = None
    for hd in range(ML_HEADS):
        sl = slice(hd * LANE, (hd + 1) * LANE)
        hm = hf_ref[:, sl] + hb_ref[:, sl]
        om = _rms(hm, mln_ref[:, sl]) * _sigmoid(mo_ref[:, sl])
        part = jnp.dot(om.astype(BF16), wa_ref[sl, :], preferred_element_type=F32)
        mix = part if mix is None else mix + part
    for hd in range(DF_HEADS):
        o1 = od_ref[:, (2 * hd) * LANE:(2 * hd + 1) * LANE]
        o2 = od_ref[:, (2 * hd + 1) * LANE:(2 * hd + 2) * LANE]
        od = _rms(o1 - lam * o2, sub_ref[...]) * (1.0 - lam_init)
        mix = mix + jnp.dot(od.astype(BF16), wb_ref[hd * LANE:(hd + 1) * LANE, :],
                            preferred_element_type=F32)
    _post_tail(x_ref, mix, *rest)


def _post_mix(kernel, x, row_inputs, const_inputs, mod, g2, wr, br, cnt_in, group_of_tile, name):
    n, d = x.shape
    parts = POST_TILES
    tm = parts * ROW_TILE
    row = lambda a: pl.BlockSpec((tm, a.shape[1]), lambda i: (i, 0))
    const = lambda a: pl.BlockSpec(a.shape, lambda i: (0,) * a.ndim)
    in_specs = ([row(x)] + [row(a) for a in row_inputs] + [const(a) for a in const_inputs]
                + [pl.BlockSpec((1, 6, d), lambda i: (group_of_tile(i * parts), 0, 0)),
                   const(g2), const(wr), const(br), const(cnt_in)])
    return pl.pallas_call(
        kernel,
        grid=(n // tm,),
        in_specs=in_specs,
        out_specs=[pl.BlockSpec((tm, d), lambda i: (i, 0)),
                   pl.BlockSpec((tm, d), lambda i: (i, 0)),
                   pl.BlockSpec((tm, LANE), lambda i: (i, 0)),
                   pl.BlockSpec((parts, SUBLANES, LANE), lambda i: (i, 0, 0)),
                   pl.BlockSpec((1, LANE), lambda i: (0, 0))],
        out_shape=[jax.ShapeDtypeStruct((n, d), F32), jax.ShapeDtypeStruct((n, d), F32),
                   jax.ShapeDtypeStruct((n, LANE), F32),
                   jax.ShapeDtypeStruct((n // ROW_TILE, SUBLANES, LANE), F32),
                   jax.ShapeDtypeStruct((1, LANE), F32)],
        scratch_shapes=[pltpu.VMEM((1, LANE), F32)],
        compiler_params=_cparams("arbitrary"),
        name=name,
    )(x, *row_inputs, *const_inputs, mod, g2, wr, br, cnt_in)


LOCAL_ROWS = ROW_TILE * TOP_K + N_EXPERTS * SUBLANES
TAB_WORDS = 128


def _split_bf16(x):
    hi = x.astype(BF16)
    return hi, (x - hi.astype(F32)).astype(BF16)


BIG_CHUNK = 4 * SUBLANES


def _run_copies(tab_smem, slot, make_copy, start):
    def per_expert(e, carry):
        n_big = tab_smem[slot, e]
        n_small = tab_smem[slot, N_EXPERTS + e]
        off = tab_smem[slot, 2 * N_EXPERTS + e]
        base = tab_smem[slot, 3 * N_EXPERTS + e]

        def chunk(rows, shift):
            def one(j, c):
                cp = make_copy(pl.multiple_of(off + shift + j * rows, SUBLANES),
                               pl.multiple_of(base + shift + j * rows, SUBLANES), rows)
                if start:
                    cp.start()
                else:
                    cp.wait()
                return c
            return one

        carry = lax.fori_loop(0, n_big, chunk(BIG_CHUNK, 0), carry)
        return lax.fori_loop(0, n_small, chunk(SUBLANES, n_big * BIG_CHUNK), carry)

    lax.fori_loop(0, N_EXPERTS, per_expert, 0)


def _dispatch_kernel(tail_ref, pend_ref, nu_ref, tab_hbm, hp_ref, hs_ref, rp_ref, rs_ref, xs_ref,
                     tab_smem, local_buf, zero_buf, sem_tab, sem_rows, sem_zero, *, tiles_p, n_blocks):
    i = pl.program_id(0)
    nt = pl.num_programs(0)
    slot = i % 2
    d = hp_ref.shape[1]

    def tab_copy(step, sl):
        return pltpu.make_async_copy(tab_hbm.at[step], tab_smem.at[sl], sem_tab.at[sl])

    def zero_block(start):
        return pltpu.make_async_copy(zero_buf, xs_ref.at[pl.ds(start, MOE_ROWS)], sem_zero)

    def zero_chunk(start):
        return pltpu.make_async_copy(zero_buf.at[pl.ds(0, SUBLANES)], xs_ref.at[pl.ds(start, SUBLANES)],
                                     sem_rows)

    @pl.when(i == 0)
    def _():
        tab_copy(0, 0).start()
        zero_buf[...] = jnp.zeros(zero_buf.shape, F32)

        def tails(start):
            def per_expert(e, carry):
                def one(j, c):
                    cp = zero_chunk(pl.multiple_of(tail_ref[e] + j * SUBLANES, SUBLANES))
                    if start:
                        cp.start()
                    else:
                        cp.wait()
                    return c
                return lax.fori_loop(0, (pend_ref[e] - tail_ref[e]) // SUBLANES, one, carry)
            lax.fori_loop(0, N_EXPERTS, per_expert, 0)

        def blocks(start):
            def one(b, c):
                cp = zero_block(pl.multiple_of(b * MOE_ROWS, MOE_ROWS))
                if start:
                    cp.start()
                else:
                    cp.wait()
                return c
            lax.fori_loop(nu_ref[0], n_blocks, one, 0)

        tails(True)
        blocks(True)
        tails(False)
        blocks(False)

    def sort_tile(h_ref, rt_ref):
        local = local_buf.at[slot]
        rt_t = rt_ref[...].T
        row_id = lax.broadcasted_iota(jnp.int32, (LOCAL_ROWS, ROW_TILE), 0).astype(F32)
        pick = jnp.zeros((LOCAL_ROWS, ROW_TILE), F32)
        gate = jnp.zeros((LOCAL_ROWS, ROW_TILE), F32)
        for k in range(TOP_K):
            hit = row_id == rt_t[ROUTE_ROW + k:ROUTE_ROW + k + 1, :]
            pick = jnp.where(hit, 1.0, pick)
            gate = jnp.where(hit, rt_t[ROUTE_GATE + k:ROUTE_GATE + k + 1, :], gate)
        local[:, 0:d] = jnp.dot(pick.astype(BF16), h_ref[...].astype(BF16), preferred_element_type=F32)
        g_hi, g_lo = _split_bf16(gate)
        ones = jnp.ones((ROW_TILE, LANE), BF16)
        local[:, d:d + LANE] = (jnp.dot(g_hi, ones, preferred_element_type=F32)
                                + jnp.dot(g_lo, ones, preferred_element_type=F32))

    @pl.when(i < tiles_p)
    def _():
        sort_tile(hp_ref, rp_ref)

    @pl.when(i >= tiles_p)
    def _():
        sort_tile(hs_ref, rs_ref)

    def run_copy(buf):
        def make(off, base, rows):
            return pltpu.make_async_copy(local_buf.at[buf, pl.ds(off, rows)], xs_ref.at[pl.ds(base, rows)],
                                         sem_rows)
        return make

    @pl.when(i > 0)
    def _():
        _run_copies(tab_smem, 1 - slot, run_copy(1 - slot), False)

    @pl.when(i + 1 < nt)
    def _():
        tab_copy(i + 1, 1 - slot).start()

    tab_copy(i, slot).wait()
    _run_copies(tab_smem, slot, run_copy(slot), True)

    @pl.when(i == nt - 1)
    def _():
        _run_copies(tab_smem, slot, run_copy(slot), False)


def _dispatch(tail_start, pad_end, n_used, tab, h2p, h2s, rtp, rts, n_blocks):
    d = h2p.shape[1]
    tm = ROW_TILE
    tiles_p = h2p.shape[0] // tm
    nt = tiles_p + h2s.shape[0] // tm
    idx_p = lambda i, *_: (jnp.minimum(i, tiles_p - 1), 0)
    idx_s = lambda i, *_: (jnp.maximum(i - tiles_p, 0), 0)
    grid_spec = pltpu.PrefetchScalarGridSpec(
        num_scalar_prefetch=3,
        grid=(nt,),
        in_specs=[pl.BlockSpec(memory_space=pl.ANY),
                  pl.BlockSpec((tm, d), idx_p), pl.BlockSpec((tm, d), idx_s),
                  pl.BlockSpec((tm, LANE), idx_p), pl.BlockSpec((tm, LANE), idx_s)],
        out_specs=pl.BlockSpec(memory_space=pl.ANY),
        scratch_shapes=[pltpu.SMEM((2, TAB_WORDS), jnp.int32),
                        pltpu.VMEM((2, LOCAL_ROWS, d + LANE), F32),
                        pltpu.VMEM((MOE_ROWS, d + LANE), F32),
                        pltpu.SemaphoreType.DMA((2,)),
                        pltpu.SemaphoreType.DMA(()),
                        pltpu.SemaphoreType.DMA(())],
    )
    return pl.pallas_call(
        functools.partial(_dispatch_kernel, tiles_p=tiles_p, n_blocks=n_blocks),
        grid_spec=grid_spec,
        out_shape=jax.ShapeDtypeStruct((n_blocks * MOE_ROWS, d + LANE), F32),
        compiler_params=_cparams("arbitrary"),
        name="moe_dispatch",
    )(tail_start, pad_end, n_used, tab, h2p, h2s, rtp, rts)


def _expert_kernel(be_ref, nu_ref, x_ref, w1_ref, b1_ref, w2_ref, b2_ref, o_ref, w1b, w2b):
    i = pl.program_id(0)
    d = o_ref.shape[1]
    new_expert = jnp.logical_or(i == 0, be_ref[i] != be_ref[jnp.maximum(i - 1, 0)])

    @pl.when(jnp.logical_and(new_expert, i < nu_ref[0]))
    def _():
        w1b[...] = w1_ref[0].astype(BF16)
        w2b[...] = w2_ref[0].astype(BF16)

    @pl.when(i < nu_ref[0])
    def _():
        x = x_ref[:, 0:d].astype(BF16)
        gate = x_ref[:, d:d + 1]
        acc = None
        fc = FF_CHUNK
        for c in range(D_FF // fc):
            glu = jnp.dot(x, w1b[:, c * fc:(c + 1) * fc], preferred_element_type=F32)
            glu = glu + b1_ref[0, :, c * fc:(c + 1) * fc]
            lin = jnp.dot(x, w1b[:, D_FF + c * fc:D_FF + (c + 1) * fc],
                          preferred_element_type=F32)
            lin = lin + b1_ref[0, :, D_FF + c * fc:D_FF + (c + 1) * fc]
            glu = jnp.minimum(glu, SWIGLU_LIMIT)
            lin = jnp.clip(lin, -SWIGLU_LIMIT, SWIGLU_LIMIT)
            act = glu * _sigmoid(SWIGLU_ALPHA * glu) * (lin + 1.0)
            part = jnp.dot(act.astype(BF16), w2b[c * fc:(c + 1) * fc, :],
                           preferred_element_type=F32)
            acc = part if acc is None else acc + part
        o_ref[...] = (acc + b2_ref[0]) * gate

    @pl.when(i >= nu_ref[0])
    def _():
        o_ref[...] = jnp.zeros(o_ref.shape, F32)


def _expert_ffn(blk_e, n_used, xs, layer, w1, b1, w2, b2):
    d = w2.shape[3]
    nb = blk_e.shape[0]
    wmap = lambda i, be, nu: (layer, be[i], 0, 0)
    sq = pl.Squeezed()
    grid_spec = pltpu.PrefetchScalarGridSpec(
        num_scalar_prefetch=2,
        grid=(nb,),
        in_specs=[pl.BlockSpec((MOE_ROWS, xs.shape[1]), lambda i, be, nu: (i, 0)),
                  pl.BlockSpec((sq, 1, d, 2 * D_FF), wmap),
                  pl.BlockSpec((sq, 1, 1, 2 * D_FF), wmap),
                  pl.BlockSpec((sq, 1, D_FF, d), wmap),
                  pl.BlockSpec((sq, 1, 1, d), wmap)],
        out_specs=pl.BlockSpec((MOE_ROWS, d), lambda i, be, nu: (i, 0)),
        scratch_shapes=[pltpu.VMEM((d, 2 * D_FF), BF16), pltpu.VMEM((D_FF, d), BF16)],
    )
    return pl.pallas_call(
        _expert_kernel,
        grid_spec=grid_spec,
        out_shape=jax.ShapeDtypeStruct((nb * MOE_ROWS, d), F32),
        compiler_params=_cparams("arbitrary"),
        name="expert_ffn",
    )(blk_e, n_used, xs, w1, b1, w2, b2)


def _combine_kernel(*refs, final):
    if final:
        (tab_hbm, x_ref, rt_ref, mod_ref, fg_ref, yb_ref, x_out_ref, yn_ref,
         tab_smem, local_buf, sem_tab, sem_rows) = refs
    else:
        (tab_hbm, x_ref, rt_ref, mod_ref, yb_ref, x_out_ref,
         tab_smem, local_buf, sem_tab, sem_rows) = refs
    i = pl.program_id(0)
    nt = pl.num_programs(0)
    slot = i % 2

    def tab_copy(step):
        return pltpu.make_async_copy(tab_hbm.at[step], tab_smem.at[step % 3], sem_tab.at[step % 3])

    def run_copy(buf):
        def make(off, base, rows):
            return pltpu.make_async_copy(yb_ref.at[pl.ds(base, rows)], local_buf.at[buf, pl.ds(off, rows)],
                                         sem_rows.at[buf])
        return make

    @pl.when(i == 0)
    def _():
        local_buf[...] = jnp.zeros(local_buf.shape, F32)
        tab_copy(0).start()
        tab_copy(0).wait()
        _run_copies(tab_smem, 0, run_copy(0), True)

        @pl.when(nt > 1)
        def _():
            tab_copy(1).start()

    @pl.when(i + 1 < nt)
    def _():
        tab_copy(i + 1).wait()
        _run_copies(tab_smem, (i + 1) % 3, run_copy(1 - slot), True)

    @pl.when(i + 2 < nt)
    def _():
        tab_copy(i + 2).start()

    _run_copies(tab_smem, i % 3, run_copy(slot), False)
    local = local_buf.at[slot]

    rt = rt_ref[...]
    col_id = lax.broadcasted_iota(jnp.int32, (ROW_TILE, LOCAL_ROWS), 1).astype(F32)
    pick = jnp.zeros((ROW_TILE, LOCAL_ROWS), F32)
    for k in range(TOP_K):
        pick = jnp.where(col_id == rt[:, ROUTE_ROW + k:ROUTE_ROW + k + 1], 1.0, pick)
    pick = pick.astype(BF16)
    y_hi, y_lo = _split_bf16(local[...])
    moe = (jnp.dot(pick, y_hi, preferred_element_type=F32)
           + jnp.dot(pick, y_lo, preferred_element_type=F32))
    x2 = x_ref[...] + mod_ref[0, 5:6, :] * moe
    x_out_ref[...] = x2
    if final:
        yn_ref[...] = _rms(x2, fg_ref[...])


def _combine(x, route, tab, yb, mod, group_of_tile, final_g):
    n, d = x.shape
    tm = ROW_TILE
    nt = n // tm
    final = final_g is not None
    in_specs = [pl.BlockSpec(memory_space=pl.ANY),
                pl.BlockSpec((tm, d), lambda i: (i, 0)),
                pl.BlockSpec((tm, LANE), lambda i: (i, 0)),
                pl.BlockSpec((1, 6, d), lambda i: (group_of_tile(i), 0, 0))]
    args = [tab, x, route, mod]
    out_shape = [jax.ShapeDtypeStruct((n, d), F32)]
    if final:
        in_specs.append(pl.BlockSpec((1, d), lambda i: (0, 0)))
        args.append(final_g)
        out_shape.append(jax.ShapeDtypeStruct((n, d), F32))
    in_specs.append(pl.BlockSpec(memory_space=pl.ANY))
    args.append(yb)
    out_specs = [pl.BlockSpec((tm, d), lambda i: (i, 0)) for _ in out_shape]
    return pl.pallas_call(
        functools.partial(_combine_kernel, final=final),
        grid=(nt,),
        in_specs=in_specs,
        out_specs=out_specs,
        out_shape=out_shape,
        scratch_shapes=[pltpu.SMEM((3, TAB_WORDS), jnp.int32),
                        pltpu.VMEM((2, LOCAL_ROWS, d), F32),
                        pltpu.SemaphoreType.DMA((3,)),
                        pltpu.SemaphoreType.DMA((2,))],
        compiler_params=_cparams("arbitrary"),
        name="moe_combine_final" if final else "moe_combine",
    )(*args)


def _block_plan(counts_f, n_tiles):
    counts = counts_f[0, :N_EXPERTS].astype(jnp.int32)
    padded = (counts + MOE_ROWS - 1) // MOE_ROWS * MOE_ROWS
    pad_end = jnp.cumsum(padded)
    pad_start = pad_end - padded
    max_rows = n_tiles * LOCAL_ROWS
    nb = -(-max_rows // MOE_ROWS) + N_EXPERTS
    first_row = jnp.arange(nb, dtype=jnp.int32) * MOE_ROWS
    blk_e = jnp.minimum(jnp.sum((pad_end[None, :] <= first_row[:, None]).astype(jnp.int32), axis=1),
                        N_EXPERTS - 1)
    n_used = (pad_end[-1] // MOE_ROWS).reshape(1)
    return pad_start, pad_start + counts, pad_end, blk_e, n_used, nb


def _run_table(tab_f, pad_start):
    t = tab_f[:, :, :N_EXPERTS].astype(jnp.int32)
    n_big = t[:, TAB_LEN] // BIG_CHUNK
    n_small = (t[:, TAB_LEN] - n_big * BIG_CHUNK) // SUBLANES
    base = t[:, TAB_BASE] + pad_start[None, :]
    return jnp.concatenate([n_big, n_small, t[:, TAB_OFF], base], axis=1)


def _to_half_split(x):
    return jnp.concatenate([x[..., 0::2], x[..., 1::2]], axis=-1)


def _from_half_split(x):
    n = x.shape[-1] // 2
    return jnp.stack([x[..., :n], x[..., n:]], axis=-1).reshape(x.shape)


def _pad_heads(w, heads, hd):
    k = w.shape[0]
    w = w.reshape(k, heads, hd)
    return jnp.pad(w, ((0, 0), (0, 0), (0, LANE - hd))).reshape(k, heads * LANE)


def _pad_rows(w, heads, hd):
    n = w.shape[1]
    w = w.reshape(heads, hd, n)
    return jnp.pad(w, ((0, 0), (0, LANE - hd), (0, 0))).reshape(heads * LANE, n)


def _rope_tables(t, rot_dim, lane_off):
    tok = jnp.arange(t)
    row = (tok // GRID_W).astype(F32)
    col = (tok % GRID_W).astype(F32)
    nf = rot_dim // 4
    inv = ROPE_THETA ** (-jnp.arange(nf, dtype=F32) / nf)
    ang = jnp.concatenate([row[:, None] * inv, col[:, None] * inv], axis=-1)
    cos, sin = jnp.cos(ang), jnp.sin(ang)
    half = rot_dim // 2
    fill = lambda v, w: jnp.full((t, w), v, F32)
    rest = LANE - lane_off - rot_dim
    cos_t = jnp.concatenate([fill(1.0, lane_off), cos, cos, fill(1.0, rest)], axis=1)
    sin_a = jnp.concatenate([fill(0.0, lane_off), -sin, fill(0.0, half + rest)], axis=1)
    sin_b = jnp.concatenate([fill(0.0, lane_off + half), sin, fill(0.0, rest)], axis=1)
    return jnp.stack([cos_t, sin_a, sin_b])


def _na_bias_table(rpb):
    c = np.arange(GRID_W)[:, None]
    kc = np.arange(GRID_W)[None, :]
    wstart = np.clip(c - NA_KW // 2, 0, GRID_W - NA_KW)
    mask = (kc >= wstart) & (kc < wstart + NA_KW)
    dcol = np.clip(kc - c + NA_KW - 1, 0, 2 * NA_KW - 2)
    d0 = np.arange(NA_KR)[:, None] + np.arange(NA_KR)[None, :]
    tab = rpb[:, d0][:, :, :, dcol]
    tab = jnp.where(mask[None, None, None], tab.astype(F32), NEG_BIG)
    tab = jnp.transpose(tab, (0, 1, 3, 2, 4))
    return tab.reshape(NA_HEADS, NA_KR, GRID_W, NA_KR * GRID_W)


def kernel(x_prompt, x_sample, c, cache_mla_ckv, cache_mla_krope, cache_na_k, cache_na_v, state_mlstm_C, state_mlstm_n, state_mlstm_m, cache_diff_k, cache_diff_v, c_ctx, w_mod, b_mod, norm_g, final_g, a_w_in, a_q_norm, a_w_uq, a_kv_norm, a_w_ukv, a_rpb, a_w_out, b_w_in, b_gate_bias, b_ml_norm, b_df_lambda, b_df_subln, b_w_out, moe_w_router, moe_b_router, moe_w1, moe_b1, moe_w2, moe_b2):
    bp, lp, d = x_prompt.shape
    bs, ts, _ = x_sample.shape
    n_p, n_s = bp * lp, bs * ts
    past = cache_mla_ckv.shape[2]
    tiles_per_batch = ts // ROW_TILE
    grp_p = lambda i: 0
    grp_s = lambda i: 1 + i // tiles_per_batch

    xp = x_prompt.reshape(n_p, d)
    xs = x_sample.reshape(n_s, d)
    cond8 = jnp.concatenate([c_ctx[None, :], c, jnp.zeros((8 - 1 - bs, d), F32)], axis=0)
    mod_all = _modulation(cond8, w_mod, b_mod)


    def moe_layer(l, x1p, x1s, h2p, h2s, rtp, rts, tabp, tabs_, counts, final):
        n_tiles = (n_p + n_s) // ROW_TILE
        pad_start, tail_start, pad_end, blk_e, n_used, nb = _block_plan(counts, n_tiles)
        tab_p = _run_table(tabp, pad_start)
        tab_s = _run_table(tabs_, pad_start)
        xs_rows = _dispatch(tail_start, pad_end, n_used, jnp.concatenate([tab_p, tab_s], axis=0),
                            h2p, h2s, rtp, rts, nb)
        yb = _expert_ffn(blk_e, n_used, xs_rows, l, moe_w1, moe_b1[:, :, None, :],
                         moe_w2, moe_b2[:, :, None, :])
        fg = final_g.reshape(1, d) if final else None
        outp = _combine(x1p, rtp, tab_p, yb, mod_all[l], grp_p, fg)
        outs = _combine(x1s, rts, tab_s, yb, mod_all[l], grp_s, fg)
        return outp, outs

    def router_w(l):
        wr = jnp.pad(moe_w_router[l], ((0, 0), (0, LANE - N_EXPERTS)))
        br = jnp.pad(moe_b_router[l], (0, LANE - N_EXPERTS)).reshape(1, LANE)
        return wr, br

    l, e = 0, 0
    w_in = a_w_in[e]
    kr_cols = _to_half_split(w_in[:, 384:416])
    w0 = jnp.concatenate([
        w_in[:, :384], jnp.pad(kr_cols, ((0, 0), (0, LANE - MLA_ROPE))),
        _pad_heads(w_in[:, 416:928], NA_HEADS, NA_HD),
        _pad_heads(w_in[:, 928:1440], NA_HEADS, NA_HD),
        _pad_heads(w_in[:, 1440:1952], NA_HEADS, NA_HD)], axis=1).astype(BF16)
    wuq = a_w_uq[e].reshape(MLA_Q_LORA, MLA_HEADS, MLA_NOPE + MLA_ROPE)
    wuq = jnp.concatenate([wuq[:, :, :MLA_NOPE], _to_half_split(wuq[:, :, MLA_NOPE:])], axis=2)
    wuq = _pad_heads(wuq.reshape(MLA_Q_LORA, -1), MLA_HEADS, MLA_NOPE + MLA_ROPE).astype(BF16)
    wukv = a_w_ukv[e].reshape(MLA_KV_LORA, MLA_HEADS, MLA_NOPE + MLA_V)
    wk = _pad_heads(wukv[:, :, :MLA_NOPE].reshape(MLA_KV_LORA, -1), MLA_HEADS, MLA_NOPE).astype(BF16)
    wv = _pad_heads(wukv[:, :, MLA_NOPE:].reshape(MLA_KV_LORA, -1), MLA_HEADS, MLA_V).astype(BF16)
    sel_np = np.zeros((LANE, MLA_HEADS * LANE), np.float32)
    for hd in range(MLA_HEADS):
        sel_np[np.arange(MLA_ROPE), hd * LANE + MLA_NOPE + np.arange(MLA_ROPE)] = 1.0
    sel = jnp.asarray(sel_np, BF16)
    qn_g = a_q_norm[e].reshape(1, -1)
    kvn_g = a_kv_norm[e].reshape(1, -1)
    g1 = norm_g[l, 0].reshape(1, d)
    g2 = norm_g[l, 1].reshape(1, d)
    tabs_a = (_rope_tables(ts, MLA_ROPE, MLA_NOPE), _rope_tables(ts, MLA_ROPE, 0))
    wide = MLA_HEADS * LANE

    q_p, ckv_p, kr_p, nq_p, nk_p, nv_p = _even_proj(xp, g1, mod_all[l], w0, qn_g, wuq, kvn_g, grp_p, None, F32)
    new_mla_ckv = ckv_p.reshape(bp, 1, lp, MLA_KV_LORA)
    new_mla_krope = _from_half_split(kr_p[:, :MLA_ROPE]).reshape(bp, 1, lp, MLA_ROPE)
    new_na_k = nk_p.reshape(bp, 1, lp, NA_HEADS, LANE)[..., :NA_HD]
    new_na_v = nv_p.reshape(bp, 1, lp, NA_HEADS, LANE)[..., :NA_HD]
    k_p, v_p = _kv_expand(ckv_p, kr_p, wk, sel, wv)
    ident = lambda hd: hd
    o_mla_p = _flash(q_p.reshape(bp, lp, wide), k_p.reshape(bp, lp, wide), v_p.reshape(bp, lp, wide),
                     v_of_head=ident, out_dtype=BF16)
    o_na_p = _flash(nq_p.reshape(bp, lp, wide), nk_p.astype(BF16).reshape(bp, lp, wide),
                    nv_p.astype(BF16).reshape(bp, lp, wide),
                    v_of_head=ident, out_dtype=BF16)
    q_s, ckv_s, kr_s, nq_s, nk_s, nv_s = _even_proj(xs, g1, mod_all[l], w0, qn_g, wuq, kvn_g, grp_s, tabs_a, BF16)
    ckv_all = jnp.concatenate([cache_mla_ckv[:, e], ckv_s.reshape(bs, ts, LANE)], axis=1)
    kr_cache = jnp.pad(_to_half_split(cache_mla_krope[:, e]), ((0, 0), (0, 0), (0, LANE - MLA_ROPE)))
    kr_all = jnp.concatenate([kr_cache, kr_s.reshape(bs, ts, LANE)], axis=1)
    n_all = past + ts
    k_s, v_s = _kv_expand(ckv_all.reshape(bs * n_all, LANE), kr_all.reshape(bs * n_all, LANE), wk, sel, wv)
    o_mla_s = _flash(q_s.reshape(bs, ts, wide), k_s.reshape(bs, n_all, wide), v_s.reshape(bs, n_all, wide),
                     v_of_head=ident, out_dtype=BF16)
    pad_hd = ((0, 0), (0, 0), (0, 0), (0, LANE - NA_HD))
    kc = jnp.pad(cache_na_k[:, e], pad_hd).astype(BF16).reshape(bs, past, wide)
    vc = jnp.pad(cache_na_v[:, e], pad_hd).astype(BF16).reshape(bs, past, wide)
    o_na_s = _na_latent(nq_s.reshape(bs, ts, wide), nk_s.reshape(bs, ts, wide), nv_s.reshape(bs, ts, wide),
                        kc, vc, _na_bias_table(a_rpb[e]))
    w_out = a_w_out[e]
    wa = _pad_rows(w_out[:MLA_HEADS * MLA_V], MLA_HEADS, MLA_V).astype(BF16)
    wb = _pad_rows(w_out[MLA_HEADS * MLA_V:], NA_HEADS, NA_HD).astype(BF16)
    wr, br = router_w(l)
    cnt0 = jnp.zeros((1, LANE), F32)
    x1p, h2p, rtp, tbp, cnt_p = _post_mix(_post_even_kernel, xp,
                                     [o_mla_p.reshape(n_p, wide), o_na_p.reshape(n_p, wide)],
                                     [wa, wb], mod_all[l], g2, wr, br, cnt0, grp_p, "post_even")
    x1s, h2s, rts, tbs, cnt_s = _post_mix(_post_even_kernel, xs,
                                     [o_mla_s.reshape(n_s, wide), o_na_s.reshape(n_s, wide)],
                                     [wa, wb], mod_all[l], g2, wr, br, cnt_p, grp_s, "post_even")
    (xp,), (xs,) = moe_layer(l, x1p, x1s, h2p, h2s, rtp, rts, tbp, tbs, cnt_s, False)

    l, o = 1, 0
    lam_init = 0.8 - 0.6 * math.exp(-0.3 * l)
    w_in = b_w_in[o]
    cuts = np.cumsum([0, 256, 256, 512, 512, 16, 512, 512, 512])
    seg = lambda i: w_in[:, cuts[i]:cuts[i + 1]]

    def diff_cols(w):
        w = _to_half_split(w.reshape(d, 2 * DF_HEADS, DF_QK))
        return _pad_heads(w.reshape(d, -1), 2 * DF_HEADS, DF_QK)

    w1p = jnp.concatenate([
        _pad_heads(seg(0), ML_HEADS, ML_DQK), _pad_heads(seg(1), ML_HEADS, ML_DQK), seg(2), seg(3),
        diff_cols(seg(5)), diff_cols(seg(6)), seg(7),
        jnp.pad(seg(4), ((0, 0), (0, LANE - 16)))], axis=1).astype(BF16)
    bg = jnp.pad(b_gate_bias[o].reshape(1, 16), ((0, 0), (0, LANE - 16)))
    g1 = norm_g[l, 0].reshape(1, d)
    g2 = norm_g[l, 1].reshape(1, d)
    tab_d = _rope_tables(ts, DF_QK, 0)
    df_v_of = lambda hd: hd // 2

    mq_p, mk_p, mv_p, mo_p, gt_p, dq_p, dk_p, dv_p = _odd_proj(xp, g1, mod_all[l], w1p, bg, grp_p, None, F32)
    new_diff_k = _from_half_split(dk_p.reshape(bp, 1, lp, DF_HEADS, 2, LANE)[..., :DF_QK])
    new_diff_v = dv_p.reshape(bp, 1, lp, DF_HEADS, DF_V)
    mw = ML_HEADS * LANE
    zc = jnp.zeros((bp, ML_HEADS, LANE, LANE), F32)
    zn = jnp.zeros((bp, ML_HEADS, 1, LANE), F32)
    seq_p = lambda a: a.reshape(bp, lp, -1)
    hf_p, cf_f, nf_f, mf_f = _mlstm_dir(seq_p(mq_p), seq_p(mk_p), seq_p(mv_p), seq_p(gt_p), zc, zn, zn,
                                        direction=0, bb=4)
    hb_p, cf_b, nf_b, mf_b = _mlstm_dir(seq_p(mq_p), seq_p(mk_p), seq_p(mv_p), seq_p(gt_p), zc, zn, zn,
                                        direction=1, bb=4)
    new_mlstm_C = jnp.stack([cf_f, cf_b], axis=1)[..., :ML_DQK][:, None]
    new_mlstm_n = jnp.stack([nf_f, nf_b], axis=1)[:, :, :, 0, :ML_DQK][:, None]
    new_mlstm_m = jnp.stack([mf_f, mf_b], axis=1)[:, :, :, 0, 0][:, None]
    od_p = _flash(seq_p(dq_p), seq_p(dk_p.astype(BF16)), seq_p(dv_p.astype(BF16)),
                  v_of_head=df_v_of, out_dtype=F32)

    mq_s, mk_s, mv_s, mo_s, gt_s, dq_s, dk_s, dv_s = _odd_proj(xs, g1, mod_all[l], w1p, bg, grp_s, tab_d, BF16)
    seq_s = lambda a: a.reshape(bs, ts, -1)
    c0 = jnp.pad(state_mlstm_C[:, o], ((0, 0), (0, 0), (0, 0), (0, 0), (0, LANE - ML_DQK)))
    n0 = jnp.pad(state_mlstm_n[:, o], ((0, 0), (0, 0), (0, 0), (0, LANE - ML_DQK)))[:, :, :, None, :]
    m0 = jnp.broadcast_to(state_mlstm_m[:, o][:, :, :, None, None], (bs, 2, ML_HEADS, 1, LANE))
    hf_s = _mlstm_dir(seq_s(mq_s), seq_s(mk_s), seq_s(mv_s), seq_s(gt_s), c0[:, 0], n0[:, 0], m0[:, 0],
                      direction=0, bb=4)[0]
    hb_s = _mlstm_dir(seq_s(mq_s), seq_s(mk_s), seq_s(mv_s), seq_s(gt_s), c0[:, 1], n0[:, 1], m0[:, 1],
                      direction=1, bb=4)[0]
    dk_cache = _to_half_split(cache_diff_k[:, o])
    dk_cache = jnp.pad(dk_cache, ((0, 0),) * 4 + ((0, LANE - DF_QK),)).astype(BF16).reshape(bs, past, 1024)
    dk_all = jnp.concatenate([dk_cache, seq_s(dk_s)], axis=1)
    dv_all = jnp.concatenate([cache_diff_v[:, o].astype(BF16).reshape(bs, past, 512), seq_s(dv_s)], axis=1)
    od_s = _flash(seq_s(dq_s), dk_all, dv_all, v_of_head=df_v_of, out_dtype=F32)

    w_out = b_w_out[o]
    wa = w_out[:ML_HEADS * ML_DV].astype(BF16)
    wb = w_out[ML_HEADS * ML_DV:].astype(BF16)
    mln = b_ml_norm[o].reshape(1, ML_HEADS * ML_DV)
    sub = b_df_subln[o].reshape(1, DF_V)
    lam_p = jnp.pad(b_df_lambda[o], ((0, 4), (0, LANE - DF_QK)))
    wr, br = router_w(l)
    post_odd = functools.partial(_post_odd_kernel, lam_init=lam_init)
    x1p, h2p, rtp, tbp, cnt_p = _post_mix(post_odd, xp, [hf_p.reshape(n_p, mw), hb_p.reshape(n_p, mw), mo_p,
                                                    od_p.reshape(n_p, 1024)],
                                     [mln, sub, lam_p, wa, wb], mod_all[l], g2, wr, br, cnt0, grp_p, "post_odd")
    x1s, h2s, rts, tbs, cnt_s = _post_mix(post_odd, xs, [hf_s.reshape(n_s, mw), hb_s.reshape(n_s, mw), mo_s,
                                                    od_s.reshape(n_s, 1024)],
                                     [mln, sub, lam_p, wa, wb], mod_all[l], g2, wr, br, cnt_p, grp_s, "post_odd")
    (_, y_p), (_, y_s) = moe_layer(l, x1p, x1s, h2p, h2s, rtp, rts, tbp, tbs, cnt_s, True)

    return (y_p.reshape(bp, lp, d), y_s.reshape(bs, ts, d), new_mla_ckv, new_mla_krope, new_na_k, new_na_v,
            new_mlstm_C, new_mlstm_n, new_mlstm_m, new_diff_k, new_diff_v)
```

```python
import functools
import math

import numpy as np
import jax
import jax.numpy as jnp
from jax import lax
from jax.experimental import pallas as pl
from jax.experimental.pallas import tpu as pltpu

F32 = jnp.float32
BF16 = jnp.bfloat16
HIGHEST = lax.Precision.HIGHEST

D_MODEL = 1024
GRID_W = 64
LANE = 128
EPS = 1e-6
ROPE_THETA = 10000.0

MLA_HEADS, MLA_Q_LORA, MLA_KV_LORA, MLA_NOPE, MLA_ROPE, MLA_V = 8, 256, 128, 64, 32, 64
NA_HEADS, NA_HD, NA_KR, NA_KW = 8, 64, 8, 16
ML_HEADS, ML_DQK, ML_DV, ML_CHUNK = 4, 64, 128, 64
DF_HEADS, DF_QK, DF_V = 4, 64, 128
N_EXPERTS, TOP_K, D_FF = 32, 4, 1024
SWIGLU_LIMIT, SWIGLU_ALPHA = 7.0, 1.702
MLA_SCALE = (MLA_NOPE + MLA_ROPE) ** -0.5

ROW_TILE = 256
POST_TILES = 2
MOE_ROWS = 512
ATT_TQ, ATT_TK = 2048, 256
MXU_AHEAD = 2
FF_CHUNK = 512
NEG_BIG = -1e30
VMEM_LIMIT = 56 * 1024 * 1024


def _cparams(*sem):
    return pltpu.CompilerParams(dimension_semantics=sem, vmem_limit_bytes=VMEM_LIMIT)


def _rms(x, g):
    return x * lax.rsqrt(jnp.mean(x * x, axis=-1, keepdims=True) + EPS) * g


def _sigmoid(x):
    return 1.0 / (1.0 + jnp.exp(-x))


def _mod_kernel(c_ref, w_ref, b_ref, o_ref):
    c = c_ref[...]
    s = c * _sigmoid(c)
    o_ref[0] = jnp.dot(s, w_ref[0], precision=HIGHEST, preferred_element_type=F32) + b_ref[0]


def _modulation(cond8, w_mod, b_mod):
    depth, d, n6 = w_mod.shape
    tn = 1024
    out = pl.pallas_call(
        _mod_kernel,
        grid=(depth, n6 // tn),
        in_specs=[
            pl.BlockSpec((8, d), lambda l, j: (0, 0)),
            pl.BlockSpec((1, d, tn), lambda l, j: (l, 0, j)),
            pl.BlockSpec((1, 1, tn), lambda l, j: (l, 0, j)),
        ],
        out_specs=pl.BlockSpec((1, 8, tn), lambda l, j: (l, 0, j)),
        out_shape=jax.ShapeDtypeStruct((depth, 8, n6), F32),
        compiler_params=_cparams("parallel", "parallel"),
        name="modulation",
    )(cond8, w_mod, b_mod.reshape(depth, 1, n6))
    return out.reshape(depth, 8, 6, d)


def _rope_block(x, cos, sin_a, sin_b, half):
    up = pltpu.roll(x, LANE - half, axis=1)
    dn = pltpu.roll(x, half, axis=1)
    return x * cos + up * sin_a + dn * sin_b


def _even_proj_kernel(*refs, rope):
    if rope:
        (x_ref, g_ref, mod_ref, w_ref, qn_ref, wuq_ref, kvn_ref, tq_ref, tk_ref,
         q_ref, ckv_ref, kr_ref, nq_ref, nk_ref, nv_ref) = refs
    else:
        (x_ref, g_ref, mod_ref, w_ref, qn_ref, wuq_ref, kvn_ref,
         q_ref, ckv_ref, kr_ref, nq_ref, nk_ref, nv_ref) = refs
    x = x_ref[...]
    h = _rms(x, g_ref[...]) * (1.0 + mod_ref[0, 1:2, :]) + mod_ref[0, 0:1, :]
    hb = h.astype(BF16)

    def proj(a, b):
        return jnp.dot(hb, w_ref[:, a:b], preferred_element_type=F32)

    q_lat = proj(0, 256)
    qn = _rms(q_lat, qn_ref[...]).astype(BF16)
    for hd in range(MLA_HEADS):
        qh = jnp.dot(qn, wuq_ref[:, hd * LANE:(hd + 1) * LANE], preferred_element_type=F32)
        if rope:
            qh = _rope_block(qh, tq_ref[0], tq_ref[1], tq_ref[2], MLA_ROPE // 2)
        q_ref[:, hd * LANE:(hd + 1) * LANE] = (qh * _exp2_scale(MLA_SCALE)).astype(q_ref.dtype)
    ckv_ref[...] = _rms(proj(256, 384), kvn_ref[...])
    kr = proj(384, 512)
    if rope:
        kr = _rope_block(kr, tk_ref[0], tk_ref[1], tk_ref[2], MLA_ROPE // 2)
    kr_ref[...] = kr
    nq = proj(512, 1536) if rope else proj(512, 1536) * _exp2_scale(NA_HD ** -0.5)
    nq_ref[...] = nq.astype(nq_ref.dtype)
    nk_ref[...] = proj(1536, 2560).astype(nk_ref.dtype)
    nv_ref[...] = proj(2560, 3584).astype(nv_ref.dtype)


def _even_proj(x, g, mod, w, qn, wuq, kvn, group_of_tile, rope_tabs, kv_dtype):
    n, d = x.shape
    tm = ROW_TILE
    rope = rope_tabs is not None
    in_specs = [
        pl.BlockSpec((tm, d), lambda i: (i, 0)),
        pl.BlockSpec((1, d), lambda i: (0, 0)),
        pl.BlockSpec((1, 6, d), lambda i: (group_of_tile(i), 0, 0)),
        pl.BlockSpec(w.shape, lambda i: (0, 0)),
        pl.BlockSpec(qn.shape, lambda i: (0, 0)),
        pl.BlockSpec(wuq.shape, lambda i: (0, 0)),
        pl.BlockSpec(kvn.shape, lambda i: (0, 0)),
    ]
    args = [x, g, mod, w, qn, wuq, kvn]
    if rope:
        tq, tk = rope_tabs
        nblk = tq.shape[1] // tm
        in_specs += [pl.BlockSpec((3, tm, LANE), lambda i: (0, i % nblk, 0)),
                     pl.BlockSpec((3, tm, LANE), lambda i: (0, i % nblk, 0))]
        args += [tq, tk]
    wide = MLA_HEADS * LANE
    out_shape = [
        jax.ShapeDtypeStruct((n, wide), BF16),
        jax.ShapeDtypeStruct((n, LANE), F32),
        jax.ShapeDtypeStruct((n, LANE), F32),
        jax.ShapeDtypeStruct((n, wide), BF16),
        jax.ShapeDtypeStruct((n, wide), kv_dtype),
        jax.ShapeDtypeStruct((n, wide), kv_dtype),
    ]
    out_specs = [pl.BlockSpec((tm, s.shape[1]), lambda i: (i, 0)) for s in out_shape]
    return pl.pallas_call(
        functools.partial(_even_proj_kernel, rope=rope),
        grid=(n // tm,),
        in_specs=in_specs,
        out_specs=out_specs,
        out_shape=out_shape,
        compiler_params=_cparams("parallel"),
        name="even_proj_rope" if rope else "even_proj",
    )(*args)


def _kv_expand_kernel(ckv_ref, kr_ref, wk_ref, sel_ref, wv_ref, k_ref, v_ref):
    c = ckv_ref[...].astype(BF16)
    r = kr_ref[...].astype(BF16)
    k = jnp.dot(c, wk_ref[...], preferred_element_type=F32)
    k = k + jnp.dot(r, sel_ref[...], preferred_element_type=F32)
    k_ref[...] = k.astype(BF16)
    v_ref[...] = jnp.dot(c, wv_ref[...], preferred_element_type=F32).astype(BF16)


def _kv_expand(ckv, kr, wk, sel, wv):
    n = ckv.shape[0]
    tm = ROW_TILE
    wide = MLA_HEADS * LANE
    const = lambda i: (0, 0)
    return pl.pallas_call(
        _kv_expand_kernel,
        grid=(n // tm,),
        in_specs=[pl.BlockSpec((tm, LANE), lambda i: (i, 0)),
                  pl.BlockSpec((tm, LANE), lambda i: (i, 0)),
                  pl.BlockSpec(wk.shape, const), pl.BlockSpec(sel.shape, const),
                  pl.BlockSpec(wv.shape, const)],
        out_specs=[pl.BlockSpec((tm, wide), lambda i: (i, 0))] * 2,
        out_shape=[jax.ShapeDtypeStruct((n, wide), BF16)] * 2,
        compiler_params=_cparams("parallel"),
        name="kv_expand",
    )(ckv, kr, wk, sel, wv)


def _exp2_scale(scale):
    return scale * math.log2(math.e)


def _flash_kernel(q_ref, k_ref, v_ref, o_ref, m_sc, l_sc, acc_sc, *, heads, v_of_head):
    kj = pl.program_id(2)

    @pl.when(kj == 0)
    def _():
        m_sc[...] = jnp.full(m_sc.shape, NEG_BIG, F32)
        l_sc[...] = jnp.zeros(l_sc.shape, F32)
        acc_sc[...] = jnp.zeros(acc_sc.shape, F32)

    def scores(hd):
        sl = slice(hd * LANE, (hd + 1) * LANE)
        return lax.dot_general(k_ref[0, :, sl], q_ref[0, :, sl], (((1,), (1,)), ((), ())),
                               preferred_element_type=F32)

    pending = [scores(hd) for hd in range(min(MXU_AHEAD, heads))]
    for hd in range(heads):
        sl = slice(hd * LANE, (hd + 1) * LANE)
        vh = v_of_head(hd)
        s_t = pending.pop(0)
        if hd + MXU_AHEAD < heads:
            pending.append(scores(hd + MXU_AHEAD))
        m_prev = m_sc[hd]
        m_new = jnp.maximum(m_prev, jnp.max(s_t, axis=0, keepdims=True))
        alpha = jnp.exp2(m_prev - m_new)
        p_t = jnp.exp2(s_t - m_new)
        l_sc[hd] = alpha * l_sc[hd] + jnp.sum(p_t, axis=0, keepdims=True)
        pv_t = lax.dot_general(v_ref[0, :, vh * LANE:(vh + 1) * LANE], p_t.astype(BF16),
                               (((0,), (0,)), ((), ())), preferred_element_type=F32)
        acc_sc[sl, :] = alpha * acc_sc[sl, :] + pv_t
        m_sc[hd] = m_new

    @pl.when(kj == pl.num_programs(2) - 1)
    def _():
        for hd in range(heads):
            sl = slice(hd * LANE, (hd + 1) * LANE)
            o_ref[0, :, sl] = (acc_sc[sl, :] / l_sc[hd]).T.astype(o_ref.dtype)


def _flash(q, k, v, *, v_of_head, out_dtype):
    b, s, wq = q.shape
    nk = k.shape[1]
    wv = v.shape[2]
    heads = wq // LANE
    tq = min(ATT_TQ, s)
    tk = min(ATT_TK, nk)
    return pl.pallas_call(
        functools.partial(_flash_kernel, heads=heads, v_of_head=v_of_head),
        grid=(b, s // tq, nk // tk),
        in_specs=[pl.BlockSpec((1, tq, wq), lambda bi, qi, ki: (bi, qi, 0)),
                  pl.BlockSpec((1, tk, wq), lambda bi, qi, ki: (bi, ki, 0)),
                  pl.BlockSpec((1, tk, wv), lambda bi, qi, ki: (bi, ki, 0))],
        out_specs=pl.BlockSpec((1, tq, wq), lambda bi, qi, ki: (bi, qi, 0)),
        out_shape=jax.ShapeDtypeStruct((b, s, wq), out_dtype),
        scratch_shapes=[pltpu.VMEM((heads, 1, tq), F32), pltpu.VMEM((heads, 1, tq), F32),
                        pltpu.VMEM((wq, tq), F32)],
        compiler_params=_cparams("parallel", "parallel", "arbitrary"),
        name="flash_attention",
    )(q, k, v)


NA_ROWS_PER_STEP = 2


def _na_kernel(q_ref, k_ref, v_ref, kc_ref, vc_ref, *rest, rows, scale):
    bias_refs, o_ref = rest[:-1], rest[-1]
    nloc = NA_KR * GRID_W
    dn = (((1,), (1,)), ((), ()))
    starts = []
    for t in range(NA_ROWS_PER_STEP):
        r = pl.program_id(1) * NA_ROWS_PER_STEP + t
        r0 = jnp.clip(r - NA_KR // 2, 0, rows - NA_KR)
        starts.append(pl.multiple_of(r0 * GRID_W, GRID_W))

    def scores(idx):
        t, hd = divmod(idx, NA_HEADS)
        sl = slice(hd * LANE, (hd + 1) * LANE)
        qh = q_ref[0, t * GRID_W:(t + 1) * GRID_W, sl]
        kh = k_ref[0, pl.ds(starts[t], nloc), sl]
        return (lax.dot_general(qh, kh, dn, preferred_element_type=F32),
                lax.dot_general(qh, kc_ref[0, :, sl], dn, preferred_element_type=F32))

    n_chain = NA_ROWS_PER_STEP * NA_HEADS
    ahead = 2 * MXU_AHEAD
    pending = [scores(idx) for idx in range(min(ahead, n_chain))]
    for idx in range(n_chain):
        t, hd = divmod(idx, NA_HEADS)
        sl = slice(hd * LANE, (hd + 1) * LANE)
        s_loc, s_ctx = pending.pop(0)
        if idx + ahead < n_chain:
            pending.append(scores(idx + ahead))
        vh = v_ref[0, pl.ds(starts[t], nloc), sl]
        s_loc = s_loc * scale + bias_refs[t][hd, 0]
        s_ctx = s_ctx * scale
        m = jnp.maximum(jnp.max(s_loc, axis=1, keepdims=True), jnp.max(s_ctx, axis=1, keepdims=True))
        p_loc = jnp.exp(s_loc - m)
        p_ctx = jnp.exp(s_ctx - m)
        l = jnp.sum(p_loc, axis=1, keepdims=True) + jnp.sum(p_ctx, axis=1, keepdims=True)
        o = (jnp.dot(p_loc.astype(BF16), vh, preferred_element_type=F32)
             + jnp.dot(p_ctx.astype(BF16), vc_ref[0, :, sl], preferred_element_type=F32))
        o_ref[0, t * GRID_W:(t + 1) * GRID_W, sl] = (o / l).astype(o_ref.dtype)


def _na_latent(q, k, v, kc, vc, bias_tab):
    b, s, wide = q.shape
    rows = s // GRID_W
    lc = kc.shape[1]
    per = NA_ROWS_PER_STEP

    def bias_spec(t):
        def idx(bi, j):
            r = j * per + t
            return (0, jnp.clip(r - NA_KR // 2, 0, rows - NA_KR) - r + NA_KR - 1, 0, 0)
        return pl.BlockSpec((NA_HEADS, 1, GRID_W, NA_KR * GRID_W), idx)

    return pl.pallas_call(
        functools.partial(_na_kernel, rows=rows, scale=NA_HD ** -0.5),
        grid=(b, rows // per),
        in_specs=[pl.BlockSpec((1, per * GRID_W, wide), lambda bi, j: (bi, j, 0)),
                  pl.BlockSpec((1, s, wide), lambda bi, j: (bi, 0, 0)),
                  pl.BlockSpec((1, s, wide), lambda bi, j: (bi, 0, 0)),
                  pl.BlockSpec((1, lc, wide), lambda bi, j: (bi, 0, 0)),
                  pl.BlockSpec((1, lc, wide), lambda bi, j: (bi, 0, 0))]
                 + [bias_spec(t) for t in range(per)],
        out_specs=pl.BlockSpec((1, per * GRID_W, wide), lambda bi, j: (bi, j, 0)),
        out_shape=jax.ShapeDtypeStruct((b, s, wide), BF16),
        compiler_params=_cparams("parallel", "arbitrary"),
        name="na_latent",
    )(q, k, v, kc, vc, *([bias_tab] * per))


def _odd_proj_kernel(*refs, rope):
    if rope:
        (x_ref, g_ref, mod_ref, w_ref, bg_ref, td_ref,
         mq_ref, mk_ref, mv_ref, mo_ref, gt_ref, dq_ref, dk_ref, dv_ref) = refs
    else:
        (x_ref, g_ref, mod_ref, w_ref, bg_ref,
         mq_ref, mk_ref, mv_ref, mo_ref, gt_ref, dq_ref, dk_ref, dv_ref) = refs
    x = x_ref[...]
    h = _rms(x, g_ref[...]) * (1.0 + mod_ref[0, 1:2, :]) + mod_ref[0, 0:1, :]
    hb = h.astype(BF16)

    def proj(a, b):
        return jnp.dot(hb, w_ref[:, a:b], preferred_element_type=F32)

    mq_ref[...] = proj(0, 512).astype(mq_ref.dtype)
    mk_ref[...] = (proj(512, 1024) * (ML_DQK ** -0.5)).astype(mk_ref.dtype)
    mv_ref[...] = proj(1024, 1536).astype(mv_ref.dtype)
    mo_ref[...] = proj(1536, 2048)
    for blk in range(8):
        qd = proj(2048 + blk * LANE, 2048 + (blk + 1) * LANE)
        kd = proj(3072 + blk * LANE, 3072 + (blk + 1) * LANE)
        if rope:
            qd = _rope_block(qd, td_ref[0], td_ref[1], td_ref[2], DF_QK // 2)
            kd = _rope_block(kd, td_ref[0], td_ref[1], td_ref[2], DF_QK // 2)
        dq_ref[:, blk * LANE:(blk + 1) * LANE] = (qd * _exp2_scale(DF_QK ** -0.5)).astype(dq_ref.dtype)
        dk_ref[:, blk * LANE:(blk + 1) * LANE] = kd.astype(dk_ref.dtype)
    dv_ref[...] = proj(4096, 4608).astype(dv_ref.dtype)
    gt = proj(4608, 4736) + bg_ref[...]
    lane = lax.broadcasted_iota(jnp.int32, gt.shape, 1)
    log_f = jnp.minimum(gt, 0.0) - jnp.log(1.0 + jnp.exp(-jnp.abs(gt)))
    gt_ref[...] = jnp.where((lane % 8) >= 4, log_f, gt)


def _odd_proj(x, g, mod, w, bg, group_of_tile, rope_tab, kv_dtype):
    n, d = x.shape
    tm = ROW_TILE
    rope = rope_tab is not None
    in_specs = [
        pl.BlockSpec((tm, d), lambda i: (i, 0)),
        pl.BlockSpec((1, d), lambda i: (0, 0)),
        pl.BlockSpec((1, 6, d), lambda i: (group_of_tile(i), 0, 0)),
        pl.BlockSpec(w.shape, lambda i: (0, 0)),
        pl.BlockSpec(bg.shape, lambda i: (0, 0)),
    ]
    args = [x, g, mod, w, bg]
    if rope:
        nblk = rope_tab.shape[1] // tm
        in_specs.append(pl.BlockSpec((3, tm, LANE), lambda i: (0, i % nblk, 0)))
        args.append(rope_tab)
    out_shape = [
        jax.ShapeDtypeStruct((n, 512), BF16),
        jax.ShapeDtypeStruct((n, 512), BF16),
        jax.ShapeDtypeStruct((n, 512), BF16),
        jax.ShapeDtypeStruct((n, 512), F32),
        jax.ShapeDtypeStruct((n, LANE), F32),
        jax.ShapeDtypeStruct((n, 1024), BF16),
        jax.ShapeDtypeStruct((n, 1024), kv_dtype),
        jax.ShapeDtypeStruct((n, 512), kv_dtype),
    ]
    out_specs = [pl.BlockSpec((tm, s.shape[1]), lambda i: (i, 0)) for s in out_shape]
    return pl.pallas_call(
        functools.partial(_odd_proj_kernel, rope=rope),
        grid=(n // tm,),
        in_specs=in_specs,
        out_specs=out_specs,
        out_shape=out_shape,
        compiler_params=_cparams("parallel"),
        name="odd_proj_rope" if rope else "odd_proj",
    )(*args)


def _mlstm_kernel(q_ref, k_ref, v_ref, g_ref, c0_ref, n0_ref, m0_ref,
                  h_ref, cf_ref, nf_ref, mf_ref, c_sc, n_sc, m_sc, *, bb, direction, reverse):
    step = pl.program_id(1)
    L = ML_CHUNK

    @pl.when(step == 0)
    def _():
        c_sc[...] = c0_ref[...]
        n_sc[...] = n0_ref[...]
        m_sc[...] = m0_ref[...]

    t_idx = lax.broadcasted_iota(jnp.int32, (L, L), 0)
    s_idx = lax.broadcasted_iota(jnp.int32, (L, L), 1)
    keep = (s_idx >= t_idx) if reverse else (s_idx <= t_idx)
    tri = keep.astype(F32)
    last = 0 if reverse else L - 1
    for bi in range(bb):
        g = g_ref[bi]
        bcum = jnp.dot(tri, g, precision=HIGHEST, preferred_element_type=F32)
        g_t = g.T
        b_t = bcum.T
        for hd in range(ML_HEADS):
            sl = slice(hd * LANE, (hd + 1) * LANE)
            ci = direction * 8 + hd
            cf = direction * 8 + 4 + hd
            li_row = g_t[ci:ci + 1, :]
            b_row = b_t[cf:cf + 1, :]
            li_col = g[:, ci:ci + 1]
            b_col = bcum[:, cf:cf + 1]
            m_old = m_sc[bi, hd][:, 0:1]
            n_old = n_sc[bi, hd]
            c_old = c_sc[bi, hd]
            qh = q_ref[bi, :, sl]
            kh = k_ref[bi, :, sl]
            vh = v_ref[bi, :, sl]
            dm = jnp.where(keep, b_col - b_row + li_row, NEG_BIG)
            m_inter = b_col + m_old
            m_t = jnp.maximum(m_inter, jnp.max(dm, axis=1, keepdims=True))
            qk = lax.dot_general(qh, kh, (((1,), (1,)), ((), ())), preferred_element_type=F32)
            w_intra = jnp.where(keep, jnp.exp(dm - m_t), 0.0) * qk
            w_inter = jnp.exp(m_inter - m_t)
            qc = lax.dot_general(qh, c_old.astype(BF16), (((1,), (1,)), ((), ())),
                                 preferred_element_type=F32)
            num = jnp.dot(w_intra.astype(BF16), vh, preferred_element_type=F32) + w_inter * qc
            qn = jnp.sum(qh.astype(F32) * n_old, axis=1, keepdims=True)
            den = jnp.sum(w_intra, axis=1, keepdims=True) + w_inter * qn
            h_ref[bi, :, sl] = num / jnp.maximum(jnp.abs(den), jnp.exp(-m_t))
            m_new = m_t[last:last + 1, :]
            b_last = b_col[last:last + 1, :]
            w_end = jnp.exp(b_last - b_col + li_col - m_new)
            decay = jnp.exp(b_last + m_old - m_new)
            wv = (w_end * vh.astype(F32)).astype(BF16)
            upd = lax.dot_general(wv, kh, (((0,), (0,)), ((), ())), preferred_element_type=F32)
            c_sc[bi, hd] = decay * c_old + upd
            n_sc[bi, hd] = decay * n_old + jnp.sum(w_end * kh.astype(F32), axis=0, keepdims=True)
            m_sc[bi, hd] = jnp.broadcast_to(m_new, (1, LANE))

    @pl.when(step == pl.num_programs(1) - 1)
    def _():
        cf_ref[...] = c_sc[...]
        nf_ref[...] = n_sc[...]
        mf_ref[...] = m_sc[...]


def _mlstm_dir(q, k, v, gates, c0, n0, m0, *, direction, bb):
    b, t, w = q.shape
    nc = t // ML_CHUNK
    reverse = direction == 1
    if reverse:
        cidx = lambda bi, c: (bi, nc - 1 - c, 0)
    else:
        cidx = lambda bi, c: (bi, c, 0)
    st4 = lambda bi, c: (bi, 0, 0, 0)
    seq = pl.BlockSpec((bb, ML_CHUNK, w), cidx)
    c_spec = pl.BlockSpec((bb, ML_HEADS, LANE, LANE), st4)
    n_spec = pl.BlockSpec((bb, ML_HEADS, 1, LANE), st4)
    return pl.pallas_call(
        functools.partial(_mlstm_kernel, bb=bb, direction=direction, reverse=reverse),
        grid=(b // bb, nc),
        in_specs=[seq, seq, seq, pl.BlockSpec((bb, ML_CHUNK, LANE), cidx), c_spec, n_spec, n_spec],
        out_specs=[seq, c_spec, n_spec, n_spec],
        out_shape=[jax.ShapeDtypeStruct((b, t, w), F32),
                   jax.ShapeDtypeStruct(c0.shape, F32),
                   jax.ShapeDtypeStruct(n0.shape, F32),
                   jax.ShapeDtypeStruct(m0.shape, F32)],
        scratch_shapes=[pltpu.VMEM((bb, ML_HEADS, LANE, LANE), F32),
                        pltpu.VMEM((bb, ML_HEADS, 1, LANE), F32),
                        pltpu.VMEM((bb, ML_HEADS, 1, LANE), F32)],
        compiler_params=_cparams("parallel", "arbitrary"),
        name="mlstm_bwd" if reverse else "mlstm_fwd",
    )(q, k, v, gates, c0, n0, m0)


ROUTE_IDX, ROUTE_ROW, ROUTE_GATE = 0, 4, 8
TAB_LEN, TAB_OFF, TAB_BASE = 0, 1, 2
SUBLANES = 8


def _post_tail(x_ref, mix, mod_ref, g2_ref, wr_ref, br_ref, cin_ref,
               x_out_ref, h_ref, rt_ref, tab_ref, cout_ref, cnt_sc):
    @pl.when(pl.program_id(0) == 0)
    def _():
        cnt_sc[...] = cin_ref[...]

    x1 = x_ref[...] + mod_ref[0, 2:3, :] * mix
    x_out_ref[...] = x1
    h2 = _rms(x1, g2_ref[...]) * (1.0 + mod_ref[0, 4:5, :]) + mod_ref[0, 3:4, :]
    h_ref[...] = h2
    logits = jnp.dot(h2, wr_ref[...], precision=HIGHEST, preferred_element_type=F32) + br_ref[...]
    tm = logits.shape[0]
    lane = lax.broadcasted_iota(jnp.int32, (tm, LANE), 1)
    lg = jnp.where(lane < N_EXPERTS, logits, NEG_BIG)
    vals, idxs, hots = [], [], []
    for _ in range(TOP_K):
        mk = jnp.max(lg, axis=1, keepdims=True)
        idx = jnp.min(jnp.where(lg == mk, lane, LANE), axis=1, keepdims=True)
        hot = lane == idx
        vals.append(mk)
        idxs.append(idx)
        hots.append(hot)
        lg = jnp.where(hot, NEG_BIG, lg)
    ex = [jnp.exp(v - vals[0]) for v in vals]
    tot = ex[0] + ex[1] + ex[2] + ex[3]
    sel = jnp.zeros((tm, LANE), F32)
    for hot in hots:
        sel = sel + hot.astype(F32)
    t_idx = lax.broadcasted_iota(jnp.int32, (ROW_TILE, ROW_TILE), 0)
    s_idx = lax.broadcasted_iota(jnp.int32, (ROW_TILE, ROW_TILE), 1)
    earlier = (s_idx < t_idx).astype(BF16)
    e_src = lax.broadcasted_iota(jnp.int32, (LANE, LANE), 0)
    e_dst = lax.broadcasted_iota(jnp.int32, (LANE, LANE), 1)
    before = (e_src < e_dst).astype(F32)
    sub = lax.broadcasted_iota(jnp.int32, (SUBLANES, LANE), 0)
    base = cnt_sc[...]
    local_parts = []
    for part in range(tm // ROW_TILE):
        sel_p = sel[part * ROW_TILE:(part + 1) * ROW_TILE]
        prefix = jnp.dot(earlier, sel_p.astype(BF16), preferred_element_type=F32)
        run_len = jnp.floor((jnp.sum(sel_p, axis=0, keepdims=True) + (SUBLANES - 1)) / SUBLANES) * SUBLANES
        run_off = jnp.dot(jnp.broadcast_to(run_len, (SUBLANES, LANE)), before, precision=HIGHEST,
                          preferred_element_type=F32)[0:1]
        local_parts.append(prefix + run_off)
        tab = jnp.where(sub == TAB_LEN, run_len, 0.0)
        tab = jnp.where(sub == TAB_OFF, run_off, tab)
        tab = jnp.where(sub == TAB_BASE, base, tab)
        tab_ref[part] = tab
        base = base + run_len
    cnt_sc[...] = base
    cout_ref[...] = base
    local = jnp.concatenate(local_parts, axis=0)
    route = jnp.zeros((tm, LANE), F32)
    for k in range(TOP_K):
        row = jnp.sum(jnp.where(hots[k], local, 0.0), axis=1, keepdims=True)
        route = jnp.where(lane == ROUTE_IDX + k, idxs[k].astype(F32), route)
        route = jnp.where(lane == ROUTE_ROW + k, row, route)
        route = jnp.where(lane == ROUTE_GATE + k, ex[k] / tot, route)
    rt_ref[...] = route


def _post_even_kernel(x_ref, a_ref, b_ref, wa_ref, wb_ref, *rest):
    mix = (jnp.dot(a_ref[...], wa_ref[...], preferred_element_type=F32)
           + jnp.dot(b_ref[...], wb_ref[...], preferred_element_type=F32))
    _post_tail(x_ref, mix, *rest)


def _post_odd_kernel(x_ref, hf_ref, hb_ref, mo_ref, od_ref, mln_ref, sub_ref, lam_ref,
                     wa_ref, wb_ref, *rest, lam_init):
    lp = lam_ref[...]
    lam = (jnp.exp(jnp.sum(lp[0:1] * lp[1:2], axis=1, keepdims=True))
           - jnp.exp(jnp.sum(lp[2:3] * lp[3:4], axis=1, keepdims=True)) + lam_init)
    mix = None
    for hd in range(ML_HEADS):
        sl = slice(hd * LANE, (hd + 1) * LANE)
        hm = hf_ref[:, sl] + hb_ref[:, sl]
        om = _rms(hm, mln_ref[:, sl]) * _sigmoid(mo_ref[:, sl])
        part = jnp.dot(om.astype(BF16), wa_ref[sl, :], preferred_element_type=F32)
        mix = part if mix is None else mix + part
    for hd in range(DF_HEADS):
        o1 = od_ref[:, (2 * hd) * LANE:(2 * hd + 1) * LANE]
        o2 = od_ref[:, (2 * hd + 1) * LANE:(2 * hd + 2) * LANE]
        od = _rms(o1 - lam * o2, sub_ref[...]) * (1.0 - lam_init)
        mix = mix + jnp.dot(od.astype(BF16), wb_ref[hd * LANE:(hd + 1) * LANE, :],
                            preferred_element_type=F32)
    _post_tail(x_ref, mix, *rest)


def _post_mix(kernel, x, row_inputs, const_inputs, mod, g2, wr, br, cnt_in, group_of_tile, name):
    n, d = x.shape
    parts = POST_TILES
    tm = parts * ROW_TILE
    row = lambda a: pl.BlockSpec((tm, a.shape[1]), lambda i: (i, 0))
    const = lambda a: pl.BlockSpec(a.shape, lambda i: (0,) * a.ndim)
    in_specs = ([row(x)] + [row(a) for a in row_inputs] + [const(a) for a in const_inputs]
                + [pl.BlockSpec((1, 6, d), lambda i: (group_of_tile(i * parts), 0, 0)),
                   const(g2), const(wr), const(br), const(cnt_in)])
    return pl.pallas_call(
        kernel,
        grid=(n // tm,),
        in_specs=in_specs,
        out_specs=[pl.BlockSpec((tm, d), lambda i: (i, 0)),
                   pl.BlockSpec((tm, d), lambda i: (i, 0)),
                   pl.BlockSpec((tm, LANE), lambda i: (i, 0)),
                   pl.BlockSpec((parts, SUBLANES, LANE), lambda i: (i, 0, 0)),
                   pl.BlockSpec((1, LANE), lambda i: (0, 0))],
        out_shape=[jax.ShapeDtypeStruct((n, d), F32), jax.ShapeDtypeStruct((n, d), F32),
                   jax.ShapeDtypeStruct((n, LANE), F32),
                   jax.ShapeDtypeStruct((n // ROW_TILE, SUBLANES, LANE), F32),
                   jax.ShapeDtypeStruct((1, LANE), F32)],
        scratch_shapes=[pltpu.VMEM((1, LANE), F32)],
        compiler_params=_cparams("arbitrary"),
        name=name,
    )(x, *row_inputs, *const_inputs, mod, g2, wr, br, cnt_in)


LOCAL_ROWS = ROW_TILE * TOP_K + N_EXPERTS * SUBLANES
TAB_WORDS = 128


def _split_bf16(x):
    hi = x.astype(BF16)
    return hi, (x - hi.astype(F32)).astype(BF16)


BIG_CHUNK = 4 * SUBLANES


def _run_copies(tab_smem, slot, make_copy, start):
    def per_expert(e, carry):
        n_big = tab_smem[slot, e]
        n_small = tab_smem[slot, N_EXPERTS + e]
        off = tab_smem[slot, 2 * N_EXPERTS + e]
        base = tab_smem[slot, 3 * N_EXPERTS + e]

        def chunk(rows, shift):
            def one(j, c):
                cp = make_copy(pl.multiple_of(off + shift + j * rows, SUBLANES),
                               pl.multiple_of(base + shift + j * rows, SUBLANES), rows)
                if start:
                    cp.start()
                else:
                    cp.wait()
                return c
            return one

        carry = lax.fori_loop(0, n_big, chunk(BIG_CHUNK, 0), carry)
        return lax.fori_loop(0, n_small, chunk(SUBLANES, n_big * BIG_CHUNK), carry)

    lax.fori_loop(0, N_EXPERTS, per_expert, 0)


def _dispatch_kernel(tail_ref, pend_ref, nu_ref, tab_hbm, hp_ref, hs_ref, rp_ref, rs_ref, xs_ref,
                     tab_smem, local_buf, zero_buf, sem_tab, sem_rows, sem_zero, *, tiles_p, n_blocks):
    i = pl.program_id(0)
    nt = pl.num_programs(0)
    slot = i % 2
    d = hp_ref.shape[1]

    def tab_copy(step, sl):
        return pltpu.make_async_copy(tab_hbm.at[step], tab_smem.at[sl], sem_tab.at[sl])

    def zero_block(start):
        return pltpu.make_async_copy(zero_buf, xs_ref.at[pl.ds(start, MOE_ROWS)], sem_zero)

    def zero_chunk(start):
        return pltpu.make_async_copy(zero_buf.at[pl.ds(0, SUBLANES)], xs_ref.at[pl.ds(start, SUBLANES)],
                                     sem_rows)

    @pl.when(i == 0)
    def _():
        tab_copy(0, 0).start()
        zero_buf[...] = jnp.zeros(zero_buf.shape, F32)

        def tails(start):
            def per_expert(e, carry):
                def one(j, c):
                    cp = zero_chunk(pl.multiple_of(tail_ref[e] + j * SUBLANES, SUBLANES))
                    if start:
                        cp.start()
                    else:
                        cp.wait()
                    return c
                return lax.fori_loop(0, (pend_ref[e] - tail_ref[e]) // SUBLANES, one, carry)
            lax.fori_loop(0, N_EXPERTS, per_expert, 0)

        def blocks(start):
            def one(b, c):
                cp = zero_block(pl.multiple_of(b * MOE_ROWS, MOE_ROWS))
                if start:
                    cp.start()
                else:
                    cp.wait()
                return c
            lax.fori_loop(nu_ref[0], n_blocks, one, 0)

        tails(True)
        blocks(True)
        tails(False)
        blocks(False)

    def sort_tile(h_ref, rt_ref):
        local = local_buf.at[slot]
        rt_t = rt_ref[...].T
        row_id = lax.broadcasted_iota(jnp.int32, (LOCAL_ROWS, ROW_TILE), 0).astype(F32)
        pick = jnp.zeros((LOCAL_ROWS, ROW_TILE), F32)
        gate = jnp.zeros((LOCAL_ROWS, ROW_TILE), F32)
        for k in range(TOP_K):
            hit = row_id == rt_t[ROUTE_ROW + k:ROUTE_ROW + k + 1, :]
            pick = jnp.where(hit, 1.0, pick)
            gate = jnp.where(hit, rt_t[ROUTE_GATE + k:ROUTE_GATE + k + 1, :], gate)
        local[:, 0:d] = jnp.dot(pick.astype(BF16), h_ref[...].astype(BF16), preferred_element_type=F32)
        g_hi, g_lo = _split_bf16(gate)
        ones = jnp.ones((ROW_TILE, LANE), BF16)
        local[:, d:d + LANE] = (jnp.dot(g_hi, ones, preferred_element_type=F32)
                                + jnp.dot(g_lo, ones, preferred_element_type=F32))

    @pl.when(i < tiles_p)
    def _():
        sort_tile(hp_ref, rp_ref)

    @pl.when(i >= tiles_p)
    def _():
        sort_tile(hs_ref, rs_ref)

    def run_copy(buf):
        def make(off, base, rows):
            return pltpu.make_async_copy(local_buf.at[buf, pl.ds(off, rows)], xs_ref.at[pl.ds(base, rows)],
                                         sem_rows)
        return make

    @pl.when(i > 0)
    def _():
        _run_copies(tab_smem, 1 - slot, run_copy(1 - slot), False)

    @pl.when(i + 1 < nt)
    def _():
        tab_copy(i + 1, 1 - slot).start()

    tab_copy(i, slot).wait()
    _run_copies(tab_smem, slot, run_copy(slot), True)

    @pl.when(i == nt - 1)
    def _():
        _run_copies(tab_smem, slot, run_copy(slot), False)


def _dispatch(tail_start, pad_end, n_used, tab, h2p, h2s, rtp, rts, n_blocks):
    d = h2p.shape[1]
    tm = ROW_TILE
    tiles_p = h2p.shape[0] // tm
    nt = tiles_p + h2s.shape[0] // tm
    idx_p = lambda i, *_: (jnp.minimum(i, tiles_p - 1), 0)
    idx_s = lambda i, *_: (jnp.maximum(i - tiles_p, 0), 0)
    grid_spec = pltpu.PrefetchScalarGridSpec(
        num_scalar_prefetch=3,
        grid=(nt,),
        in_specs=[pl.BlockSpec(memory_space=pl.ANY),
                  pl.BlockSpec((tm, d), idx_p), pl.BlockSpec((tm, d), idx_s),
                  pl.BlockSpec((tm, LANE), idx_p), pl.BlockSpec((tm, LANE), idx_s)],
        out_specs=pl.BlockSpec(memory_space=pl.ANY),
        scratch_shapes=[pltpu.SMEM((2, TAB_WORDS), jnp.int32),
                        pltpu.VMEM((2, LOCAL_ROWS, d + LANE), F32),
                        pltpu.VMEM((MOE_ROWS, d + LANE), F32),
                        pltpu.SemaphoreType.DMA((2,)),
                        pltpu.SemaphoreType.DMA(()),
                        pltpu.SemaphoreType.DMA(())],
    )
    return pl.pallas_call(
        functools.partial(_dispatch_kernel, tiles_p=tiles_p, n_blocks=n_blocks),
        grid_spec=grid_spec,
        out_shape=jax.ShapeDtypeStruct((n_blocks * MOE_ROWS, d + LANE), F32),
        compiler_params=_cparams("arbitrary"),
        name="moe_dispatch",
    )(tail_start, pad_end, n_used, tab, h2p, h2s, rtp, rts)


def _expert_kernel(be_ref, nu_ref, x_ref, w1_ref, b1_ref, w2_ref, b2_ref, o_ref, w1b, w2b):
    i = pl.program_id(0)
    d = o_ref.shape[1]
    new_expert = jnp.logical_or(i == 0, be_ref[i] != be_ref[jnp.maximum(i - 1, 0)])

    @pl.when(jnp.logical_and(new_expert, i < nu_ref[0]))
    def _():
        w1b[...] = w1_ref[0].astype(BF16)
        w2b[...] = w2_ref[0].astype(BF16)

    @pl.when(i < nu_ref[0])
    def _():
        x = x_ref[:, 0:d].astype(BF16)
        gate = x_ref[:, d:d + 1]
        acc = None
        fc = FF_CHUNK
        for c in range(D_FF // fc):
            glu = jnp.dot(x, w1b[:, c * fc:(c + 1) * fc], preferred_element_type=F32)
            glu = glu + b1_ref[0, :, c * fc:(c + 1) * fc]
            lin = jnp.dot(x, w1b[:, D_FF + c * fc:D_FF + (c + 1) * fc],
                          preferred_element_type=F32)
            lin = lin + b1_ref[0, :, D_FF + c * fc:D_FF + (c + 1) * fc]
            glu = jnp.minimum(glu, SWIGLU_LIMIT)
            lin = jnp.clip(lin, -SWIGLU_LIMIT, SWIGLU_LIMIT)
            act = glu * _sigmoid(SWIGLU_ALPHA * glu) * (lin + 1.0)
            part = jnp.dot(act.astype(BF16), w2b[c * fc:(c + 1) * fc, :],
                           preferred_element_type=F32)
            acc = part if acc is None else acc + part
        o_ref[...] = (acc + b2_ref[0]) * gate

    @pl.when(i >= nu_ref[0])
    def _():
        o_ref[...] = jnp.zeros(o_ref.shape, F32)


def _expert_ffn(blk_e, n_used, xs, layer, w1, b1, w2, b2):
    d = w2.shape[3]
    nb = blk_e.shape[0]
    wmap = lambda i, be, nu: (layer, be[i], 0, 0)
    sq = pl.Squeezed()
    grid_spec = pltpu.PrefetchScalarGridSpec(
        num_scalar_prefetch=2,
        grid=(nb,),
        in_specs=[pl.BlockSpec((MOE_ROWS, xs.shape[1]), lambda i, be, nu: (i, 0)),
                  pl.BlockSpec((sq, 1, d, 2 * D_FF), wmap),
                  pl.BlockSpec((sq, 1, 1, 2 * D_FF), wmap),
                  pl.BlockSpec((sq, 1, D_FF, d), wmap),
                  pl.BlockSpec((sq, 1, 1, d), wmap)],
        out_specs=pl.BlockSpec((MOE_ROWS, d), lambda i, be, nu: (i, 0)),
        scratch_shapes=[pltpu.VMEM((d, 2 * D_FF), BF16), pltpu.VMEM((D_FF, d), BF16)],
    )
    return pl.pallas_call(
        _expert_kernel,
        grid_spec=grid_spec,
        out_shape=jax.ShapeDtypeStruct((nb * MOE_ROWS, d), F32),
        compiler_params=_cparams("arbitrary"),
        name="expert_ffn",
    )(blk_e, n_used, xs, w1, b1, w2, b2)


def _combine_kernel(*refs, final):
    if final:
        (tab_hbm, x_ref, rt_ref, mod_ref, fg_ref, yb_ref, x_out_ref, yn_ref,
         tab_smem, local_buf, sem_tab, sem_rows) = refs
    else:
        (tab_hbm, x_ref, rt_ref, mod_ref, yb_ref, x_out_ref,
         tab_smem, local_buf, sem_tab, sem_rows) = refs
    i = pl.program_id(0)
    nt = pl.num_programs(0)
    slot = i % 2

    def tab_copy(step):
        return pltpu.make_async_copy(tab_hbm.at[step], tab_smem.at[step % 3], sem_tab.at[step % 3])

    def run_copy(buf):
        def make(off, base, rows):
            return pltpu.make_async_copy(yb_ref.at[pl.ds(base, rows)], local_buf.at[buf, pl.ds(off, rows)],
                                         sem_rows.at[buf])
        return make

    @pl.when(i == 0)
    def _():
        local_buf[...] = jnp.zeros(local_buf.shape, F32)
        tab_copy(0).start()
        tab_copy(0).wait()
        _run_copies(tab_smem, 0, run_copy(0), True)

        @pl.when(nt > 1)
        def _():
            tab_copy(1).start()

    @pl.when(i + 1 < nt)
    def _():
        tab_copy(i + 1).wait()
        _run_copies(tab_smem, (i + 1) % 3, run_copy(1 - slot), True)

    @pl.when(i + 2 < nt)
    def _():
        tab_copy(i + 2).start()

    _run_copies(tab_smem, i % 3, run_copy(slot), False)
    local = local_buf.at[slot]

    rt = rt_ref[...]
    col_id = lax.broadcasted_iota(jnp.int32, (ROW_TILE, LOCAL_ROWS), 1).astype(F32)
    pick = jnp.zeros((ROW_TILE, LOCAL_ROWS), F32)
    for k in range(TOP_K):
        pick = jnp.where(col_id == rt[:, ROUTE_ROW + k:ROUTE_ROW + k + 1], 1.0, pick)
    pick = pick.astype(BF16)
    y_hi, y_lo = _split_bf16(local[...])
    moe = (jnp.dot(pick, y_hi, preferred_element_type=F32)
           + jnp.dot(pick, y_lo, preferred_element_type=F32))
    x2 = x_ref[...] + mod_ref[0, 5:6, :] * moe
    x_out_ref[...] = x2
    if final:
        yn_ref[...] = _rms(x2, fg_ref[...])


def _combine(x, route, tab, yb, mod, group_of_tile, final_g):
    n, d = x.shape
    tm = ROW_TILE
    nt = n // tm
    final = final_g is not None
    in_specs = [pl.BlockSpec(memory_space=pl.ANY),
                pl.BlockSpec((tm, d), lambda i: (i, 0)),
                pl.BlockSpec((tm, LANE), lambda i: (i, 0)),
                pl.BlockSpec((1, 6, d), lambda i: (group_of_tile(i), 0, 0))]
    args = [tab, x, route, mod]
    out_shape = [jax.ShapeDtypeStruct((n, d), F32)]
    if final:
        in_specs.append(pl.BlockSpec((1, d), lambda i: (0, 0)))
        args.append(final_g)
        out_shape.append(jax.ShapeDtypeStruct((n, d), F32))
    in_specs.append(pl.BlockSpec(memory_space=pl.ANY))
    args.append(yb)
    out_specs = [pl.BlockSpec((tm, d), lambda i: (i, 0)) for _ in out_shape]
    return pl.pallas_call(
        functools.partial(_combine_kernel, final=final),
        grid=(nt,),
        in_specs=in_specs,
        out_specs=out_specs,
        out_shape=out_shape,
        scratch_shapes=[pltpu.SMEM((3, TAB_WORDS), jnp.int32),
                        pltpu.VMEM((2, LOCAL_ROWS, d), F32),
                        pltpu.SemaphoreType.DMA((3,)),
                        pltpu.SemaphoreType.DMA((2,))],
        compiler_params=_cparams("arbitrary"),
        name="moe_combine_final" if final else "moe_combine",
    )(*args)


def _block_plan(counts_f, n_tiles):
    counts = counts_f[0, :N_EXPERTS].astype(jnp.int32)
    padded = (counts + MOE_ROWS - 1) // MOE_ROWS * MOE_ROWS
    pad_end = jnp.cumsum(padded)
    pad_start = pad_end - padded
    max_rows = n_tiles * LOCAL_ROWS
    nb = -(-max_rows // MOE_ROWS) + N_EXPERTS
    first_row = jnp.arange(nb, dtype=jnp.int32) * MOE_ROWS
    blk_e = jnp.minimum(jnp.sum((pad_end[None, :] <= first_row[:, None]).astype(jnp.int32), axis=1),
                        N_EXPERTS - 1)
    n_used = (pad_end[-1] // MOE_ROWS).reshape(1)
    return pad_start, pad_start + counts, pad_end, blk_e, n_used, nb


def _run_table(tab_f, pad_start):
    t = tab_f[:, :, :N_EXPERTS].astype(jnp.int32)
    n_big = t[:, TAB_LEN] // BIG_CHUNK
    n_small = (t[:, TAB_LEN] - n_big * BIG_CHUNK) // SUBLANES
    base = t[:, TAB_BASE] + pad_start[None, :]
    return jnp.concatenate([n_big, n_small, t[:, TAB_OFF], base], axis=1)


def _to_half_split(x):
    return jnp.concatenate([x[..., 0::2], x[..., 1::2]], axis=-1)


def _from_half_split(x):
    n = x.shape[-1] // 2
    return jnp.stack([x[..., :n], x[..., n:]], axis=-1).reshape(x.shape)


def _pad_heads(w, heads, hd):
    k = w.shape[0]
    w = w.reshape(k, heads, hd)
    return jnp.pad(w, ((0, 0), (0, 0), (0, LANE - hd))).reshape(k, heads * LANE)


def _pad_rows(w, heads, hd):
    n = w.shape[1]
    w = w.reshape(heads, hd, n)
    return jnp.pad(w, ((0, 0), (0, LANE - hd), (0, 0))).reshape(heads * LANE, n)


def _rope_tables(t, rot_dim, lane_off):
    tok = jnp.arange(t)
    row = (tok // GRID_W).astype(F32)
    col = (tok % GRID_W).astype(F32)
    nf = rot_dim // 4
    inv = ROPE_THETA ** (-jnp.arange(nf, dtype=F32) / nf)
    ang = jnp.concatenate([row[:, None] * inv, col[:, None] * inv], axis=-1)
    cos, sin = jnp.cos(ang), jnp.sin(ang)
    half = rot_dim // 2
    fill = lambda v, w: jnp.full((t, w), v, F32)
    rest = LANE - lane_off - rot_dim
    cos_t = jnp.concatenate([fill(1.0, lane_off), cos, cos, fill(1.0, rest)], axis=1)
    sin_a = jnp.concatenate([fill(0.0, lane_off), -sin, fill(0.0, half + rest)], axis=1)
    sin_b = jnp.concatenate([fill(0.0, lane_off + half), sin, fill(0.0, rest)], axis=1)
    return jnp.stack([cos_t, sin_a, sin_b])


def _na_bias_table(rpb):
    c = np.arange(GRID_W)[:, None]
    kc = np.arange(GRID_W)[None, :]
    wstart = np.clip(c - NA_KW // 2, 0, GRID_W - NA_KW)
    mask = (kc >= wstart) & (kc < wstart + NA_KW)
    n_dc = 2 * NA_KW - 1
    dcol = np.clip(kc - c + NA_KW - 1, 0, n_dc - 1)
    pick = (dcol[None] == np.arange(n_dc)[:, None, None]).astype(np.float32)
    toep = jnp.einsum('hrd,dck->hrck', rpb.astype(F32), pick, precision=HIGHEST)
    toep = jnp.where(mask[None, None], toep, NEG_BIG)
    tab = jnp.stack([toep[:, d0:d0 + NA_KR] for d0 in range(NA_KR)], axis=1)
    tab = jnp.transpose(tab, (0, 1, 3, 2, 4))
    return tab.reshape(NA_HEADS, NA_KR, GRID_W, NA_KR * GRID_W)


def kernel(x_prompt, x_sample, c, cache_mla_ckv, cache_mla_krope, cache_na_k, cache_na_v, state_mlstm_C, state_mlstm_n, state_mlstm_m, cache_diff_k, cache_diff_v, c_ctx, w_mod, b_mod, norm_g, final_g, a_w_in, a_q_norm, a_w_uq, a_kv_norm, a_w_ukv, a_rpb, a_w_out, b_w_in, b_gate_bias, b_ml_norm, b_df_lambda, b_df_subln, b_w_out, moe_w_router, moe_b_router, moe_w1, moe_b1, moe_w2, moe_b2):
    bp, lp, d = x_prompt.shape
    bs, ts, _ = x_sample.shape
    n_p, n_s = bp * lp, bs * ts
    past = cache_mla_ckv.shape[2]
    tiles_per_batch = ts // ROW_TILE
    grp_p = lambda i: 0
    grp_s = lambda i: 1 + i // tiles_per_batch

    xp = x_prompt.reshape(n_p, d)
    xs = x_sample.reshape(n_s, d)
    cond8 = jnp.concatenate([c_ctx[None, :], c, jnp.zeros((8 - 1 - bs, d), F32)], axis=0)
    mod_all = _modulation(cond8, w_mod, b_mod)


    def moe_layer(l, x1p, x1s, h2p, h2s, rtp, rts, tabp, tabs_, counts, final):
        n_tiles = (n_p + n_s) // ROW_TILE
        pad_start, tail_start, pad_end, blk_e, n_used, nb = _block_plan(counts, n_tiles)
        tab_p = _run_table(tabp, pad_start)
        tab_s = _run_table(tabs_, pad_start)
        xs_rows = _dispatch(tail_start, pad_end, n_used, jnp.concatenate([tab_p, tab_s], axis=0),
                            h2p, h2s, rtp, rts, nb)
        yb = _expert_ffn(blk_e, n_used, xs_rows, l, moe_w1, moe_b1[:, :, None, :],
                         moe_w2, moe_b2[:, :, None, :])
        fg = final_g.reshape(1, d) if final else None
        outp = _combine(x1p, rtp, tab_p, yb, mod_all[l], grp_p, fg)
        outs = _combine(x1s, rts, tab_s, yb, mod_all[l], grp_s, fg)
        return outp, outs

    def router_w(l):
        wr = jnp.pad(moe_w_router[l], ((0, 0), (0, LANE - N_EXPERTS)))
        br = jnp.pad(moe_b_router[l], (0, LANE - N_EXPERTS)).reshape(1, LANE)
        return wr, br

    l, e = 0, 0
    w_in = a_w_in[e]
    kr_cols = _to_half_split(w_in[:, 384:416])
    w0 = jnp.concatenate([
        w_in[:, :384], jnp.pad(kr_cols, ((0, 0), (0, LANE - MLA_ROPE))),
        _pad_heads(w_in[:, 416:928], NA_HEADS, NA_HD),
        _pad_heads(w_in[:, 928:1440], NA_HEADS, NA_HD),
        _pad_heads(w_in[:, 1440:1952], NA_HEADS, NA_HD)], axis=1).astype(BF16)
    wuq = a_w_uq[e].reshape(MLA_Q_LORA, MLA_HEADS, MLA_NOPE + MLA_ROPE)
    wuq = jnp.concatenate([wuq[:, :, :MLA_NOPE], _to_half_split(wuq[:, :, MLA_NOPE:])], axis=2)
    wuq = _pad_heads(wuq.reshape(MLA_Q_LORA, -1), MLA_HEADS, MLA_NOPE + MLA_ROPE).astype(BF16)
    wukv = a_w_ukv[e].reshape(MLA_KV_LORA, MLA_HEADS, MLA_NOPE + MLA_V)
    wk = _pad_heads(wukv[:, :, :MLA_NOPE].reshape(MLA_KV_LORA, -1), MLA_HEADS, MLA_NOPE).astype(BF16)
    wv = _pad_heads(wukv[:, :, MLA_NOPE:].reshape(MLA_KV_LORA, -1), MLA_HEADS, MLA_V).astype(BF16)
    sel_np = np.zeros((LANE, MLA_HEADS * LANE), np.float32)
    for hd in range(MLA_HEADS):
        sel_np[np.arange(MLA_ROPE), hd * LANE + MLA_NOPE + np.arange(MLA_ROPE)] = 1.0
    sel = jnp.asarray(sel_np, BF16)
    qn_g = a_q_norm[e].reshape(1, -1)
    kvn_g = a_kv_norm[e].reshape(1, -1)
    g1 = norm_g[l, 0].reshape(1, d)
    g2 = norm_g[l, 1].reshape(1, d)
    tabs_a = (_rope_tables(ts, MLA_ROPE, MLA_NOPE), _rope_tables(ts, MLA_ROPE, 0))
    wide = MLA_HEADS * LANE

    q_p, ckv_p, kr_p, nq_p, nk_p, nv_p = _even_proj(xp, g1, mod_all[l], w0, qn_g, wuq, kvn_g, grp_p, None, F32)
    new_mla_ckv = ckv_p.reshape(bp, 1, lp, MLA_KV_LORA)
    new_mla_krope = _from_half_split(kr_p[:, :MLA_ROPE]).reshape(bp, 1, lp, MLA_ROPE)
    new_na_k = nk_p.reshape(bp, 1, lp, NA_HEADS, LANE)[..., :NA_HD]
    new_na_v = nv_p.reshape(bp, 1, lp, NA_HEADS, LANE)[..., :NA_HD]
    k_p, v_p = _kv_expand(ckv_p, kr_p, wk, sel, wv)
    ident = lambda hd: hd
    o_mla_p = _flash(q_p.reshape(bp, lp, wide), k_p.reshape(bp, lp, wide), v_p.reshape(bp, lp, wide),
                     v_of_head=ident, out_dtype=BF16)
    o_na_p = _flash(nq_p.reshape(bp, lp, wide), nk_p.astype(BF16).reshape(bp, lp, wide),
                    nv_p.astype(BF16).reshape(bp, lp, wide),
                    v_of_head=ident, out_dtype=BF16)
    q_s, ckv_s, kr_s, nq_s, nk_s, nv_s = _even_proj(xs, g1, mod_all[l], w0, qn_g, wuq, kvn_g, grp_s, tabs_a, BF16)
    ckv_all = jnp.concatenate([cache_mla_ckv[:, e], ckv_s.reshape(bs, ts, LANE)], axis=1)
    kr_cache = jnp.pad(_to_half_split(cache_mla_krope[:, e]), ((0, 0), (0, 0), (0, LANE - MLA_ROPE)))
    kr_all = jnp.concatenate([kr_cache, kr_s.reshape(bs, ts, LANE)], axis=1)
    n_all = past + ts
    k_s, v_s = _kv_expand(ckv_all.reshape(bs * n_all, LANE), kr_all.reshape(bs * n_all, LANE), wk, sel, wv)
    o_mla_s = _flash(q_s.reshape(bs, ts, wide), k_s.reshape(bs, n_all, wide), v_s.reshape(bs, n_all, wide),
                     v_of_head=ident, out_dtype=BF16)
    pad_hd = ((0, 0), (0, 0), (0, 0), (0, LANE - NA_HD))
    kc = jnp.pad(cache_na_k[:, e], pad_hd).astype(BF16).reshape(bs, past, wide)
    vc = jnp.pad(cache_na_v[:, e], pad_hd).astype(BF16).reshape(bs, past, wide)
    o_na_s = _na_latent(nq_s.reshape(bs, ts, wide), nk_s.reshape(bs, ts, wide), nv_s.reshape(bs, ts, wide),
                        kc, vc, _na_bias_table(a_rpb[e]))
    w_out = a_w_out[e]
    wa = _pad_rows(w_out[:MLA_HEADS * MLA_V], MLA_HEADS, MLA_V).astype(BF16)
    wb = _pad_rows(w_out[MLA_HEADS * MLA_V:], NA_HEADS, NA_HD).astype(BF16)
    wr, br = router_w(l)
    cnt0 = jnp.zeros((1, LANE), F32)
    x1p, h2p, rtp, tbp, cnt_p = _post_mix(_post_even_kernel, xp,
                                     [o_mla_p.reshape(n_p, wide), o_na_p.reshape(n_p, wide)],
                                     [wa, wb], mod_all[l], g2, wr, br, cnt0, grp_p, "post_even")
    x1s, h2s, rts, tbs, cnt_s = _post_mix(_post_even_kernel, xs,
                                     [o_mla_s.reshape(n_s, wide), o_na_s.reshape(n_s, wide)],
                                     [wa, wb], mod_all[l], g2, wr, br, cnt_p, grp_s, "post_even")
    (xp,), (xs,) = moe_layer(l, x1p, x1s, h2p, h2s, rtp, rts, tbp, tbs, cnt_s, False)

    l, o = 1, 0
    lam_init = 0.8 - 0.6 * math.exp(-0.3 * l)
    w_in = b_w_in[o]
    cuts = np.cumsum([0, 256, 256, 512, 512, 16, 512, 512, 512])
    seg = lambda i: w_in[:, cuts[i]:cuts[i + 1]]

    def diff_cols(w):
        w = _to_half_split(w.reshape(d, 2 * DF_HEADS, DF_QK))
        return _pad_heads(w.reshape(d, -1), 2 * DF_HEADS, DF_QK)

    w1p = jnp.concatenate([
        _pad_heads(seg(0), ML_HEADS, ML_DQK), _pad_heads(seg(1), ML_HEADS, ML_DQK), seg(2), seg(3),
        diff_cols(seg(5)), diff_cols(seg(6)), seg(7),
        jnp.pad(seg(4), ((0, 0), (0, LANE - 16)))], axis=1).astype(BF16)
    bg = jnp.pad(b_gate_bias[o].reshape(1, 16), ((0, 0), (0, LANE - 16)))
    g1 = norm_g[l, 0].reshape(1, d)
    g2 = norm_g[l, 1].reshape(1, d)
    tab_d = _rope_tables(ts, DF_QK, 0)
    df_v_of = lambda hd: hd // 2

    mq_p, mk_p, mv_p, mo_p, gt_p, dq_p, dk_p, dv_p = _odd_proj(xp, g1, mod_all[l], w1p, bg, grp_p, None, F32)
    new_diff_k = _from_half_split(dk_p.reshape(bp, 1, lp, DF_HEADS, 2, LANE)[..., :DF_QK])
    new_diff_v = dv_p.reshape(bp, 1, lp, DF_HEADS, DF_V)
    mw = ML_HEADS * LANE
    zc = jnp.zeros((bp, ML_HEADS, LANE, LANE), F32)
    zn = jnp.zeros((bp, ML_HEADS, 1, LANE), F32)
    seq_p = lambda a: a.reshape(bp, lp, -1)
    hf_p, cf_f, nf_f, mf_f = _mlstm_dir(seq_p(mq_p), seq_p(mk_p), seq_p(mv_p), seq_p(gt_p), zc, zn, zn,
                                        direction=0, bb=4)
    hb_p, cf_b, nf_b, mf_b = _mlstm_dir(seq_p(mq_p), seq_p(mk_p), seq_p(mv_p), seq_p(gt_p), zc, zn, zn,
                                        direction=1, bb=4)
    new_mlstm_C = jnp.stack([cf_f, cf_b], axis=1)[..., :ML_DQK][:, None]
    new_mlstm_n = jnp.stack([nf_f, nf_b], axis=1)[:, :, :, 0, :ML_DQK][:, None]
    new_mlstm_m = jnp.stack([mf_f, mf_b], axis=1)[:, :, :, 0, 0][:, None]
    od_p = _flash(seq_p(dq_p), seq_p(dk_p.astype(BF16)), seq_p(dv_p.astype(BF16)),
                  v_of_head=df_v_of, out_dtype=F32)

    mq_s, mk_s, mv_s, mo_s, gt_s, dq_s, dk_s, dv_s = _odd_proj(xs, g1, mod_all[l], w1p, bg, grp_s, tab_d, BF16)
    seq_s = lambda a: a.reshape(bs, ts, -1)
    c0 = jnp.pad(state_mlstm_C[:, o], ((0, 0), (0, 0), (0, 0), (0, 0), (0, LANE - ML_DQK)))
    n0 = jnp.pad(state_mlstm_n[:, o], ((0, 0), (0, 0), (0, 0), (0, LANE - ML_DQK)))[:, :, :, None, :]
    m0 = jnp.broadcast_to(state_mlstm_m[:, o][:, :, :, None, None], (bs, 2, ML_HEADS, 1, LANE))
    hf_s = _mlstm_dir(seq_s(mq_s), seq_s(mk_s), seq_s(mv_s), seq_s(gt_s), c0[:, 0], n0[:, 0], m0[:, 0],
                      direction=0, bb=4)[0]
    hb_s = _mlstm_dir(seq_s(mq_s), seq_s(mk_s), seq_s(mv_s), seq_s(gt_s), c0[:, 1], n0[:, 1], m0[:, 1],
                      direction=1, bb=4)[0]
    dk_cache = _to_half_split(cache_diff_k[:, o])
    dk_cache = jnp.pad(dk_cache, ((0, 0),) * 4 + ((0, LANE - DF_QK),)).astype(BF16).reshape(bs, past, 1024)
    dk_all = jnp.concatenate([dk_cache, seq_s(dk_s)], axis=1)
    dv_all = jnp.concatenate([cache_diff_v[:, o].astype(BF16).reshape(bs, past, 512), seq_s(dv_s)], axis=1)
    od_s = _flash(seq_s(dq_s), dk_all, dv_all, v_of_head=df_v_of, out_dtype=F32)

    w_out = b_w_out[o]
    wa = w_out[:ML_HEADS * ML_DV].astype(BF16)
    wb = w_out[ML_HEADS * ML_DV:].astype(BF16)
    mln = b_ml_norm[o].reshape(1, ML_HEADS * ML_DV)
    sub = b_df_subln[o].reshape(1, DF_V)
    lam_p = jnp.pad(b_df_lambda[o], ((0, 4), (0, LANE - DF_QK)))
    wr, br = router_w(l)
    post_odd = functools.partial(_post_odd_kernel, lam_init=lam_init)
    x1p, h2p, rtp, tbp, cnt_p = _post_mix(post_odd, xp, [hf_p.reshape(n_p, mw), hb_p.reshape(n_p, mw), mo_p,
                                                    od_p.reshape(n_p, 1024)],
                                     [mln, sub, lam_p, wa, wb], mod_all[l], g2, wr, br, cnt0, grp_p, "post_odd")
    x1s, h2s, rts, tbs, cnt_s = _post_mix(post_odd, xs, [hf_s.reshape(n_s, mw), hb_s.reshape(n_s, mw), mo_s,
                                                    od_s.reshape(n_s, 1024)],
                                     [mln, sub, lam_p, wa, wb], mod_all[l], g2, wr, br, cnt_p, grp_s, "post_odd")
    (_, y_p), (_, y_s) = moe_layer(l, x1p, x1s, h2p, h2s, rtp, rts, tbp, tbs, cnt_s, True)

    return (y_p.reshape(bp, lp, d), y_s.reshape(bs, ts, d), new_mla_ckv, new_mla_krope, new_na_k, new_na_v,
            new_mlstm_C, new_mlstm_n, new_mlstm_m, new_diff_k, new_diff_v)
```

```python
import functools
import math

import numpy as np
import jax
import jax.numpy as jnp
from jax import lax
from jax.experimental import pallas as pl
from jax.experimental.pallas import tpu as pltpu

F32 = jnp.float32
BF16 = jnp.bfloat16
HIGHEST = lax.Precision.HIGHEST

D_MODEL = 1024
GRID_W = 64
LANE = 128
EPS = 1e-6
ROPE_THETA = 10000.0

MLA_HEADS, MLA_Q_LORA, MLA_KV_LORA, MLA_NOPE, MLA_ROPE, MLA_V = 8, 256, 128, 64, 32, 64
NA_HEADS, NA_HD, NA_KR, NA_KW = 8, 64, 8, 16
ML_HEADS, ML_DQK, ML_DV, ML_CHUNK = 4, 64, 128, 64
DF_HEADS, DF_QK, DF_V = 4, 64, 128
N_EXPERTS, TOP_K, D_FF = 32, 4, 1024
SWIGLU_LIMIT, SWIGLU_ALPHA = 7.0, 1.702
MLA_SCALE = (MLA_NOPE + MLA_ROPE) ** -0.5

ROW_TILE = 256
POST_TILES = 2
MOE_ROWS = 512
ATT_TQ, ATT_TK = 2048, 256
MXU_AHEAD = 2
FF_CHUNK = 512
NEG_BIG = -1e30
VMEM_LIMIT = 56 * 1024 * 1024


def _cparams(*sem):
    return pltpu.CompilerParams(dimension_semantics=sem, vmem_limit_bytes=VMEM_LIMIT)


def _rms(x, g):
    return x * lax.rsqrt(jnp.mean(x * x, axis=-1, keepdims=True) + EPS) * g


def _sigmoid(x):
    return 1.0 / (1.0 + jnp.exp(-x))


def _mod_kernel(c_ref, w_ref, b_ref, o_ref):
    c = c_ref[...]
    s = c * _sigmoid(c)
    o_ref[0] = jnp.dot(s, w_ref[0], precision=HIGHEST, preferred_element_type=F32) + b_ref[0]


def _modulation(cond8, w_mod, b_mod):
    depth, d, n6 = w_mod.shape
    tn = 1024
    out = pl.pallas_call(
        _mod_kernel,
        grid=(depth, n6 // tn),
        in_specs=[
            pl.BlockSpec((8, d), lambda l, j: (0, 0)),
            pl.BlockSpec((1, d, tn), lambda l, j: (l, 0, j)),
            pl.BlockSpec((1, 1, tn), lambda l, j: (l, 0, j)),
        ],
        out_specs=pl.BlockSpec((1, 8, tn), lambda l, j: (l, 0, j)),
        out_shape=jax.ShapeDtypeStruct((depth, 8, n6), F32),
        compiler_params=_cparams("parallel", "parallel"),
        name="modulation",
    )(cond8, w_mod, b_mod.reshape(depth, 1, n6))
    return out.reshape(depth, 8, 6, d)


def _rope_block(x, cos, sin_a, sin_b, half):
    up = pltpu.roll(x, LANE - half, axis=1)
    dn = pltpu.roll(x, half, axis=1)
    return x * cos + up * sin_a + dn * sin_b


def _even_proj_kernel(*refs, rope):
    if rope:
        (x_ref, g_ref, mod_ref, w_ref, qn_ref, wuq_ref, kvn_ref, tq_ref, tk_ref,
         q_ref, ckv_ref, kr_ref, nq_ref, nk_ref, nv_ref) = refs
    else:
        (x_ref, g_ref, mod_ref, w_ref, qn_ref, wuq_ref, kvn_ref,
         q_ref, ckv_ref, kr_ref, nq_ref, nk_ref, nv_ref) = refs
    x = x_ref[...]
    h = _rms(x, g_ref[...]) * (1.0 + mod_ref[0, 1:2, :]) + mod_ref[0, 0:1, :]
    hb = h.astype(BF16)

    def proj(a, b):
        return jnp.dot(hb, w_ref[:, a:b], preferred_element_type=F32)

    q_lat = proj(0, 256)
    qn = _rms(q_lat, qn_ref[...]).astype(BF16)
    for hd in range(MLA_HEADS):
        qh = jnp.dot(qn, wuq_ref[:, hd * LANE:(hd + 1) * LANE], preferred_element_type=F32)
        if rope:
            qh = _rope_block(qh, tq_ref[0], tq_ref[1], tq_ref[2], MLA_ROPE // 2)
        q_ref[:, hd * LANE:(hd + 1) * LANE] = (qh * _exp2_scale(MLA_SCALE)).astype(q_ref.dtype)
    ckv_ref[...] = _rms(proj(256, 384), kvn_ref[...])
    kr = proj(384, 512)
    if rope:
        kr = _rope_block(kr, tk_ref[0], tk_ref[1], tk_ref[2], MLA_ROPE // 2)
    kr_ref[...] = kr
    nq = proj(512, 1536) if rope else proj(512, 1536) * _exp2_scale(NA_HD ** -0.5)
    nq_ref[...] = nq.astype(nq_ref.dtype)
    nk_ref[...] = proj(1536, 2560).astype(nk_ref.dtype)
    nv_ref[...] = proj(2560, 3584).astype(nv_ref.dtype)


def _even_proj(x, g, mod, w, qn, wuq, kvn, group_of_tile, rope_tabs, kv_dtype):
    n, d = x.shape
    tm = ROW_TILE
    rope = rope_tabs is not None
    in_specs = [
        pl.BlockSpec((tm, d), lambda i: (i, 0)),
        pl.BlockSpec((1, d), lambda i: (0, 0)),
        pl.BlockSpec((1, 6, d), lambda i: (group_of_tile(i), 0, 0)),
        pl.BlockSpec(w.shape, lambda i: (0, 0)),
        pl.BlockSpec(qn.shape, lambda i: (0, 0)),
        pl.BlockSpec(wuq.shape, lambda i: (0, 0)),
        pl.BlockSpec(kvn.shape, lambda i: (0, 0)),
    ]
    args = [x, g, mod, w, qn, wuq, kvn]
    if rope:
        tq, tk = rope_tabs
        nblk = tq.shape[1] // tm
        in_specs += [pl.BlockSpec((3, tm, LANE), lambda i: (0, i % nblk, 0)),
                     pl.BlockSpec((3, tm, LANE), lambda i: (0, i % nblk, 0))]
        args += [tq, tk]
    wide = MLA_HEADS * LANE
    out_shape = [
        jax.ShapeDtypeStruct((n, wide), BF16),
        jax.ShapeDtypeStruct((n, LANE), F32),
        jax.ShapeDtypeStruct((n, LANE), F32),
        jax.ShapeDtypeStruct((n, wide), BF16),
        jax.ShapeDtypeStruct((n, wide), kv_dtype),
        jax.ShapeDtypeStruct((n, wide), kv_dtype),
    ]
    out_specs = [pl.BlockSpec((tm, s.shape[1]), lambda i: (i, 0)) for s in out_shape]
    return pl.pallas_call(
        functools.partial(_even_proj_kernel, rope=rope),
        grid=(n // tm,),
        in_specs=in_specs,
        out_specs=out_specs,
        out_shape=out_shape,
        compiler_params=_cparams("parallel"),
        name="even_proj_rope" if rope else "even_proj",
    )(*args)


def _kv_expand_kernel(ckv_ref, kr_ref, wk_ref, sel_ref, wv_ref, k_ref, v_ref):
    c = ckv_ref[...].astype(BF16)
    r = kr_ref[...].astype(BF16)
    k = jnp.dot(c, wk_ref[...], preferred_element_type=F32)
    k = k + jnp.dot(r, sel_ref[...], preferred_element_type=F32)
    k_ref[...] = k.astype(BF16)
    v_ref[...] = jnp.dot(c, wv_ref[...], preferred_element_type=F32).astype(BF16)


def _kv_expand(ckv, kr, wk, sel, wv):
    n = ckv.shape[0]
    tm = ROW_TILE
    wide = MLA_HEADS * LANE
    const = lambda i: (0, 0)
    return pl.pallas_call(
        _kv_expand_kernel,
        grid=(n // tm,),
        in_specs=[pl.BlockSpec((tm, LANE), lambda i: (i, 0)),
                  pl.BlockSpec((tm, LANE), lambda i: (i, 0)),
                  pl.BlockSpec(wk.shape, const), pl.BlockSpec(sel.shape, const),
                  pl.BlockSpec(wv.shape, const)],
        out_specs=[pl.BlockSpec((tm, wide), lambda i: (i, 0))] * 2,
        out_shape=[jax.ShapeDtypeStruct((n, wide), BF16)] * 2,
        compiler_params=_cparams("parallel"),
        name="kv_expand",
    )(ckv, kr, wk, sel, wv)


def _exp2_scale(scale):
    return scale * math.log2(math.e)


def _flash_kernel(q_ref, k_ref, v_ref, o_ref, m_sc, l_sc, acc_sc, *, heads, v_of_head):
    kj = pl.program_id(2)

    @pl.when(kj == 0)
    def _():
        m_sc[...] = jnp.full(m_sc.shape, NEG_BIG, F32)
        l_sc[...] = jnp.zeros(l_sc.shape, F32)
        acc_sc[...] = jnp.zeros(acc_sc.shape, F32)

    def scores(hd):
        sl = slice(hd * LANE, (hd + 1) * LANE)
        return lax.dot_general(k_ref[0, :, sl], q_ref[0, :, sl], (((1,), (1,)), ((), ())),
                               preferred_element_type=F32)

    pending = [scores(hd) for hd in range(min(MXU_AHEAD, heads))]
    for hd in range(heads):
        sl = slice(hd * LANE, (hd + 1) * LANE)
        vh = v_of_head(hd)
        s_t = pending.pop(0)
        if hd + MXU_AHEAD < heads:
            pending.append(scores(hd + MXU_AHEAD))
        m_prev = m_sc[hd]
        m_new = jnp.maximum(m_prev, jnp.max(s_t, axis=0, keepdims=True))
        alpha = jnp.exp2(m_prev - m_new)
        p_t = jnp.exp2(s_t - m_new)
        l_sc[hd] = alpha * l_sc[hd] + jnp.sum(p_t, axis=0, keepdims=True)
        pv_t = lax.dot_general(v_ref[0, :, vh * LANE:(vh + 1) * LANE], p_t.astype(BF16),
                               (((0,), (0,)), ((), ())), preferred_element_type=F32)
        acc_sc[sl, :] = alpha * acc_sc[sl, :] + pv_t
        m_sc[hd] = m_new

    @pl.when(kj == pl.num_programs(2) - 1)
    def _():
        for hd in range(heads):
            sl = slice(hd * LANE, (hd + 1) * LANE)
            o_ref[0, :, sl] = (acc_sc[sl, :] / l_sc[hd]).T.astype(o_ref.dtype)


def _flash(q, k, v, *, v_of_head, out_dtype):
    b, s, wq = q.shape
    nk = k.shape[1]
    wv = v.shape[2]
    heads = wq // LANE
    tq = min(ATT_TQ, s)
    tk = min(ATT_TK, nk)
    return pl.pallas_call(
        functools.partial(_flash_kernel, heads=heads, v_of_head=v_of_head),
        grid=(b, s // tq, nk // tk),
        in_specs=[pl.BlockSpec((1, tq, wq), lambda bi, qi, ki: (bi, qi, 0)),
                  pl.BlockSpec((1, tk, wq), lambda bi, qi, ki: (bi, ki, 0)),
                  pl.BlockSpec((1, tk, wv), lambda bi, qi, ki: (bi, ki, 0))],
        out_specs=pl.BlockSpec((1, tq, wq), lambda bi, qi, ki: (bi, qi, 0)),
        out_shape=jax.ShapeDtypeStruct((b, s, wq), out_dtype),
        scratch_shapes=[pltpu.VMEM((heads, 1, tq), F32), pltpu.VMEM((heads, 1, tq), F32),
                        pltpu.VMEM((wq, tq), F32)],
        compiler_params=_cparams("parallel", "parallel", "arbitrary"),
        name="flash_attention",
    )(q, k, v)


NA_ROWS_PER_STEP = 4


def _na_kernel(q_ref, k_ref, v_ref, kc_ref, vc_ref, *rest, rows, scale):
    bias_refs, o_ref = rest[:-1], rest[-1]
    nloc = NA_KR * GRID_W
    dn = (((1,), (1,)), ((), ()))
    starts = []
    for t in range(NA_ROWS_PER_STEP):
        r = pl.program_id(1) * NA_ROWS_PER_STEP + t
        r0 = jnp.clip(r - NA_KR // 2, 0, rows - NA_KR)
        starts.append(pl.multiple_of(r0 * GRID_W, GRID_W))

    def scores(idx):
        t, hd = divmod(idx, NA_HEADS)
        sl = slice(hd * LANE, (hd + 1) * LANE)
        qh = q_ref[0, t * GRID_W:(t + 1) * GRID_W, sl]
        kh = k_ref[0, pl.ds(starts[t], nloc), sl]
        return (lax.dot_general(qh, kh, dn, preferred_element_type=F32),
                lax.dot_general(qh, kc_ref[0, :, sl], dn, preferred_element_type=F32))

    n_chain = NA_ROWS_PER_STEP * NA_HEADS
    ahead = 2 * MXU_AHEAD
    pending = [scores(idx) for idx in range(min(ahead, n_chain))]
    for idx in range(n_chain):
        t, hd = divmod(idx, NA_HEADS)
        sl = slice(hd * LANE, (hd + 1) * LANE)
        s_loc, s_ctx = pending.pop(0)
        if idx + ahead < n_chain:
            pending.append(scores(idx + ahead))
        vh = v_ref[0, pl.ds(starts[t], nloc), sl]
        s_loc = s_loc * scale + bias_refs[t][hd, 0]
        s_ctx = s_ctx * scale
        m = jnp.maximum(jnp.max(s_loc, axis=1, keepdims=True), jnp.max(s_ctx, axis=1, keepdims=True))
        p_loc = jnp.exp(s_loc - m)
        p_ctx = jnp.exp(s_ctx - m)
        l = jnp.sum(p_loc, axis=1, keepdims=True) + jnp.sum(p_ctx, axis=1, keepdims=True)
        o = (jnp.dot(p_loc.astype(BF16), vh, preferred_element_type=F32)
             + jnp.dot(p_ctx.astype(BF16), vc_ref[0, :, sl], preferred_element_type=F32))
        o_ref[0, t * GRID_W:(t + 1) * GRID_W, sl] = (o / l).astype(o_ref.dtype)


def _na_latent(q, k, v, kc, vc, bias_tab):
    b, s, wide = q.shape
    rows = s // GRID_W
    lc = kc.shape[1]
    per = NA_ROWS_PER_STEP

    def bias_spec(t):
        def idx(bi, j):
            r = j * per + t
            return (0, jnp.clip(r - NA_KR // 2, 0, rows - NA_KR) - r + NA_KR - 1, 0, 0)
        return pl.BlockSpec((NA_HEADS, 1, GRID_W, NA_KR * GRID_W), idx)

    return pl.pallas_call(
        functools.partial(_na_kernel, rows=rows, scale=NA_HD ** -0.5),
        grid=(b, rows // per),
        in_specs=[pl.BlockSpec((1, per * GRID_W, wide), lambda bi, j: (bi, j, 0)),
                  pl.BlockSpec((1, s, wide), lambda bi, j: (bi, 0, 0)),
                  pl.BlockSpec((1, s, wide), lambda bi, j: (bi, 0, 0)),
                  pl.BlockSpec((1, lc, wide), lambda bi, j: (bi, 0, 0)),
                  pl.BlockSpec((1, lc, wide), lambda bi, j: (bi, 0, 0))]
                 + [bias_spec(t) for t in range(per)],
        out_specs=pl.BlockSpec((1, per * GRID_W, wide), lambda bi, j: (bi, j, 0)),
        out_shape=jax.ShapeDtypeStruct((b, s, wide), BF16),
        compiler_params=_cparams("parallel", "arbitrary"),
        name="na_latent",
    )(q, k, v, kc, vc, *([bias_tab] * per))


def _odd_proj_kernel(*refs, rope):
    if rope:
        (x_ref, g_ref, mod_ref, w_ref, bg_ref, td_ref,
         mq_ref, mk_ref, mv_ref, mo_ref, gt_ref, dq_ref, dk_ref, dv_ref) = refs
    else:
        (x_ref, g_ref, mod_ref, w_ref, bg_ref,
         mq_ref, mk_ref, mv_ref, mo_ref, gt_ref, dq_ref, dk_ref, dv_ref) = refs
    x = x_ref[...]
    h = _rms(x, g_ref[...]) * (1.0 + mod_ref[0, 1:2, :]) + mod_ref[0, 0:1, :]
    hb = h.astype(BF16)

    def proj(a, b):
        return jnp.dot(hb, w_ref[:, a:b], preferred_element_type=F32)

    mq_ref[...] = proj(0, 512).astype(mq_ref.dtype)
    mk_ref[...] = (proj(512, 1024) * (ML_DQK ** -0.5)).astype(mk_ref.dtype)
    mv_ref[...] = proj(1024, 1536).astype(mv_ref.dtype)
    mo_ref[...] = proj(1536, 2048)
    for blk in range(8):
        qd = proj(2048 + blk * LANE, 2048 + (blk + 1) * LANE)
        kd = proj(3072 + blk * LANE, 3072 + (blk + 1) * LANE)
        if rope:
            qd = _rope_block(qd, td_ref[0], td_ref[1], td_ref[2], DF_QK // 2)
            kd = _rope_block(kd, td_ref[0], td_ref[1], td_ref[2], DF_QK // 2)
        dq_ref[:, blk * LANE:(blk + 1) * LANE] = (qd * _exp2_scale(DF_QK ** -0.5)).astype(dq_ref.dtype)
        dk_ref[:, blk * LANE:(blk + 1) * LANE] = kd.astype(dk_ref.dtype)
    dv_ref[...] = proj(4096, 4608).astype(dv_ref.dtype)
    gt = proj(4608, 4736) + bg_ref[...]
    lane = lax.broadcasted_iota(jnp.int32, gt.shape, 1)
    log_f = jnp.minimum(gt, 0.0) - jnp.log(1.0 + jnp.exp(-jnp.abs(gt)))
    gt_ref[...] = jnp.where((lane % 8) >= 4, log_f, gt)


def _odd_proj(x, g, mod, w, bg, group_of_tile, rope_tab, kv_dtype):
    n, d = x.shape
    tm = ROW_TILE
    rope = rope_tab is not None
    in_specs = [
        pl.BlockSpec((tm, d), lambda i: (i, 0)),
        pl.BlockSpec((1, d), lambda i: (0, 0)),
        pl.BlockSpec((1, 6, d), lambda i: (group_of_tile(i), 0, 0)),
        pl.BlockSpec(w.shape, lambda i: (0, 0)),
        pl.BlockSpec(bg.shape, lambda i: (0, 0)),
    ]
    args = [x, g, mod, w, bg]
    if rope:
        nblk = rope_tab.shape[1] // tm
        in_specs.append(pl.BlockSpec((3, tm, LANE), lambda i: (0, i % nblk, 0)))
        args.append(rope_tab)
    out_shape = [
        jax.ShapeDtypeStruct((n, 512), BF16),
        jax.ShapeDtypeStruct((n, 512), BF16),
        jax.ShapeDtypeStruct((n, 512), BF16),
        jax.ShapeDtypeStruct((n, 512), F32),
        jax.ShapeDtypeStruct((n, LANE), F32),
        jax.ShapeDtypeStruct((n, 1024), BF16),
        jax.ShapeDtypeStruct((n, 1024), kv_dtype),
        jax.ShapeDtypeStruct((n, 512), kv_dtype),
    ]
    out_specs = [pl.BlockSpec((tm, s.shape[1]), lambda i: (i, 0)) for s in out_shape]
    return pl.pallas_call(
        functools.partial(_odd_proj_kernel, rope=rope),
        grid=(n // tm,),
        in_specs=in_specs,
        out_specs=out_specs,
        out_shape=out_shape,
        compiler_params=_cparams("parallel"),
        name="odd_proj_rope" if rope else "odd_proj",
    )(*args)


def _mlstm_kernel(q_ref, k_ref, v_ref, g_ref, c0_ref, n0_ref, m0_ref,
                  h_ref, cf_ref, nf_ref, mf_ref, c_sc, n_sc, m_sc, *, bb, direction, reverse):
    step = pl.program_id(1)
    L = ML_CHUNK

    @pl.when(step == 0)
    def _():
        c_sc[...] = c0_ref[...]
        n_sc[...] = n0_ref[...]
        m_sc[...] = m0_ref[...]

    t_idx = lax.broadcasted_iota(jnp.int32, (L, L), 0)
    s_idx = lax.broadcasted_iota(jnp.int32, (L, L), 1)
    keep = (s_idx >= t_idx) if reverse else (s_idx <= t_idx)
    tri = keep.astype(F32)
    last = 0 if reverse else L - 1
    for bi in range(bb):
        g = g_ref[bi]
        bcum = jnp.dot(tri, g, precision=HIGHEST, preferred_element_type=F32)
        g_t = g.T
        b_t = bcum.T
        for hd in range(ML_HEADS):
            sl = slice(hd * LANE, (hd + 1) * LANE)
            ci = direction * 8 + hd
            cf = direction * 8 + 4 + hd
            li_row = g_t[ci:ci + 1, :]
            b_row = b_t[cf:cf + 1, :]
            li_col = g[:, ci:ci + 1]
            b_col = bcum[:, cf:cf + 1]
            m_old = m_sc[bi, hd][:, 0:1]
            n_old = n_sc[bi, hd]
            c_old = c_sc[bi, hd]
            qh = q_ref[bi, :, sl]
            kh = k_ref[bi, :, sl]
            vh = v_ref[bi, :, sl]
            dm = jnp.where(keep, b_col - b_row + li_row, NEG_BIG)
            m_inter = b_col + m_old
            m_t = jnp.maximum(m_inter, jnp.max(dm, axis=1, keepdims=True))
            qk = lax.dot_general(qh, kh, (((1,), (1,)), ((), ())), preferred_element_type=F32)
            w_intra = jnp.where(keep, jnp.exp(dm - m_t), 0.0) * qk
            w_inter = jnp.exp(m_inter - m_t)
            qc = lax.dot_general(qh, c_old.astype(BF16), (((1,), (1,)), ((), ())),
                                 preferred_element_type=F32)
            num = jnp.dot(w_intra.astype(BF16), vh, preferred_element_type=F32) + w_inter * qc
            qn = jnp.sum(qh.astype(F32) * n_old, axis=1, keepdims=True)
            den = jnp.sum(w_intra, axis=1, keepdims=True) + w_inter * qn
            h_ref[bi, :, sl] = num / jnp.maximum(jnp.abs(den), jnp.exp(-m_t))
            m_new = m_t[last:last + 1, :]
            b_last = b_col[last:last + 1, :]
            w_end = jnp.exp(b_last - b_col + li_col - m_new)
            decay = jnp.exp(b_last + m_old - m_new)
            wv = (w_end * vh.astype(F32)).astype(BF16)
            upd = lax.dot_general(wv, kh, (((0,), (0,)), ((), ())), preferred_element_type=F32)
            c_sc[bi, hd] = decay * c_old + upd
            n_sc[bi, hd] = decay * n_old + jnp.sum(w_end * kh.astype(F32), axis=0, keepdims=True)
            m_sc[bi, hd] = jnp.broadcast_to(m_new, (1, LANE))

    @pl.when(step == pl.num_programs(1) - 1)
    def _():
        cf_ref[...] = c_sc[...]
        nf_ref[...] = n_sc[...]
        mf_ref[...] = m_sc[...]


def _mlstm_dir(q, k, v, gates, c0, n0, m0, *, direction, bb):
    b, t, w = q.shape
    nc = t // ML_CHUNK
    reverse = direction == 1
    if reverse:
        cidx = lambda bi, c: (bi, nc - 1 - c, 0)
    else:
        cidx = lambda bi, c: (bi, c, 0)
    st4 = lambda bi, c: (bi, 0, 0, 0)
    seq = pl.BlockSpec((bb, ML_CHUNK, w), cidx)
    c_spec = pl.BlockSpec((bb, ML_HEADS, LANE, LANE), st4)
    n_spec = pl.BlockSpec((bb, ML_HEADS, 1, LANE), st4)
    return pl.pallas_call(
        functools.partial(_mlstm_kernel, bb=bb, direction=direction, reverse=reverse),
        grid=(b // bb, nc),
        in_specs=[seq, seq, seq, pl.BlockSpec((bb, ML_CHUNK, LANE), cidx), c_spec, n_spec, n_spec],
        out_specs=[seq, c_spec, n_spec, n_spec],
        out_shape=[jax.ShapeDtypeStruct((b, t, w), F32),
                   jax.ShapeDtypeStruct(c0.shape, F32),
                   jax.ShapeDtypeStruct(n0.shape, F32),
                   jax.ShapeDtypeStruct(m0.shape, F32)],
        scratch_shapes=[pltpu.VMEM((bb, ML_HEADS, LANE, LANE), F32),
                        pltpu.VMEM((bb, ML_HEADS, 1, LANE), F32),
                        pltpu.VMEM((bb, ML_HEADS, 1, LANE), F32)],
        compiler_params=_cparams("parallel", "arbitrary"),
        name="mlstm_bwd" if reverse else "mlstm_fwd",
    )(q, k, v, gates, c0, n0, m0)


ROUTE_IDX, ROUTE_ROW, ROUTE_GATE = 0, 4, 8
TAB_LEN, TAB_OFF, TAB_BASE = 0, 1, 2
SUBLANES = 8


def _post_tail(x_ref, mix, mod_ref, g2_ref, wr_ref, br_ref, cin_ref,
               x_out_ref, h_ref, rt_ref, tab_ref, cout_ref, cnt_sc):
    @pl.when(pl.program_id(0) == 0)
    def _():
        cnt_sc[...] = cin_ref[...]

    x1 = x_ref[...] + mod_ref[0, 2:3, :] * mix
    x_out_ref[...] = x1
    h2 = _rms(x1, g2_ref[...]) * (1.0 + mod_ref[0, 4:5, :]) + mod_ref[0, 3:4, :]
    h_ref[...] = h2
    logits = jnp.dot(h2, wr_ref[...], precision=HIGHEST, preferred_element_type=F32) + br_ref[...]
    tm = logits.shape[0]
    lane = lax.broadcasted_iota(jnp.int32, (tm, LANE), 1)
    lg = jnp.where(lane < N_EXPERTS, logits, NEG_BIG)
    vals, idxs, hots = [], [], []
    for _ in range(TOP_K):
        mk = jnp.max(lg, axis=1, keepdims=True)
        idx = jnp.min(jnp.where(lg == mk, lane, LANE), axis=1, keepdims=True)
        hot = lane == idx
        vals.append(mk)
        idxs.append(idx)
        hots.append(hot)
        lg = jnp.where(hot, NEG_BIG, lg)
    ex = [jnp.exp(v - vals[0]) for v in vals]
    tot = ex[0] + ex[1] + ex[2] + ex[3]
    sel = jnp.zeros((tm, LANE), F32)
    for hot in hots:
        sel = sel + hot.astype(F32)
    t_idx = lax.broadcasted_iota(jnp.int32, (ROW_TILE, ROW_TILE), 0)
    s_idx = lax.broadcasted_iota(jnp.int32, (ROW_TILE, ROW_TILE), 1)
    earlier = (s_idx < t_idx).astype(BF16)
    e_src = lax.broadcasted_iota(jnp.int32, (LANE, LANE), 0)
    e_dst = lax.broadcasted_iota(jnp.int32, (LANE, LANE), 1)
    before = (e_src < e_dst).astype(F32)
    sub = lax.broadcasted_iota(jnp.int32, (SUBLANES, LANE), 0)
    base = cnt_sc[...]
    local_parts = []
    for part in range(tm // ROW_TILE):
        sel_p = sel[part * ROW_TILE:(part + 1) * ROW_TILE]
        prefix = jnp.dot(earlier, sel_p.astype(BF16), preferred_element_type=F32)
        run_len = jnp.floor((jnp.sum(sel_p, axis=0, keepdims=True) + (SUBLANES - 1)) / SUBLANES) * SUBLANES
        run_off = jnp.dot(jnp.broadcast_to(run_len, (SUBLANES, LANE)), before, precision=HIGHEST,
                          preferred_element_type=F32)[0:1]
        local_parts.append(prefix + run_off)
        tab = jnp.where(sub == TAB_LEN, run_len, 0.0)
        tab = jnp.where(sub == TAB_OFF, run_off, tab)
        tab = jnp.where(sub == TAB_BASE, base, tab)
        tab_ref[part] = tab
        base = base + run_len
    cnt_sc[...] = base
    cout_ref[...] = base
    local = jnp.concatenate(local_parts, axis=0)
    route = jnp.zeros((tm, LANE), F32)
    for k in range(TOP_K):
        row = jnp.sum(jnp.where(hots[k], local, 0.0), axis=1, keepdims=True)
        route = jnp.where(lane == ROUTE_IDX + k, idxs[k].astype(F32), route)
        route = jnp.where(lane == ROUTE_ROW + k, row, route)
        route = jnp.where(lane == ROUTE_GATE + k, ex[k] / tot, route)
    rt_ref[...] = route


def _post_even_kernel(x_ref, a_ref, b_ref, wa_ref, wb_ref, *rest):
    mix = (jnp.dot(a_ref[...], wa_ref[...], preferred_element_type=F32)
           + jnp.dot(b_ref[...], wb_ref[...], preferred_element_type=F32))
    _post_tail(x_ref, mix, *rest)


def _post_odd_kernel(x_ref, hf_ref, hb_ref, mo_ref, od_ref, mln_ref, sub_ref, lam_ref,
                     wa_ref, wb_ref, *rest, lam_init):
    lp = lam_ref[...]
    lam = (jnp.exp(jnp.sum(lp[0:1] * lp[1:2], axis=1, keepdims=True))
           - jnp.exp(jnp.sum(lp[2:3] * lp[3:4], axis=1, keepdims=True)) + lam_init)
    mix = None
    for hd in range(ML_HEADS):
        sl = slice(hd * LANE, (hd + 1) * LANE)
        hm = hf_ref[:, sl] + hb_ref[:, sl]
        om = _rms(hm, mln_ref[:, sl]) * _sigmoid(mo_ref[:, sl])
        part = jnp.dot(om.astype(BF16), wa_ref[sl, :], preferred_element_type=F32)
        mix = part if mix is None else mix + part
    for hd in range(DF_HEADS):
        o1 = od_ref[:, (2 * hd) * LANE:(2 * hd + 1) * LANE]
        o2 = od_ref[:, (2 * hd + 1) * LANE:(2 * hd + 2) * LANE]
        od = _rms(o1 - lam * o2, sub_ref[...]) * (1.0 - lam_init)
        mix = mix + jnp.dot(od.astype(BF16), wb_ref[hd * LANE:(hd + 1) * LANE, :],
                            preferred_element_type=F32)
    _post_tail(x_ref, mix, *rest)


def _post_mix(kernel, x, row_inputs, const_inputs, mod, g2, wr, br, cnt_in, group_of_tile, name):
    n, d = x.shape
    parts = POST_TILES
    tm = parts * ROW_TILE
    row = lambda a: pl.BlockSpec((tm, a.shape[1]), lambda i: (i, 0))
    const = lambda a: pl.BlockSpec(a.shape, lambda i: (0,) * a.ndim)
    in_specs = ([row(x)] + [row(a) for a in row_inputs] + [const(a) for a in const_inputs]
                + [pl.BlockSpec((1, 6, d), lambda i: (group_of_tile(i * parts), 0, 0)),
                   const(g2), const(wr), const(br), const(cnt_in)])
    return pl.pallas_call(
        kernel,
        grid=(n // tm,),
        in_specs=in_specs,
        out_specs=[pl.BlockSpec((tm, d), lambda i: (i, 0)),
                   pl.BlockSpec((tm, d), lambda i: (i, 0)),
                   pl.BlockSpec((tm, LANE), lambda i: (i, 0)),
                   pl.BlockSpec((parts, SUBLANES, LANE), lambda i: (i, 0, 0)),
                   pl.BlockSpec((1, LANE), lambda i: (0, 0))],
        out_shape=[jax.ShapeDtypeStruct((n, d), F32), jax.ShapeDtypeStruct((n, d), F32),
                   jax.ShapeDtypeStruct((n, LANE), F32),
                   jax.ShapeDtypeStruct((n // ROW_TILE, SUBLANES, LANE), F32),
                   jax.ShapeDtypeStruct((1, LANE), F32)],
        scratch_shapes=[pltpu.VMEM((1, LANE), F32)],
        compiler_params=_cparams("arbitrary"),
        name=name,
    )(x, *row_inputs, *const_inputs, mod, g2, wr, br, cnt_in)


LOCAL_ROWS = ROW_TILE * TOP_K + N_EXPERTS * SUBLANES
TAB_WORDS = 128


def _split_bf16(x):
    hi = x.astype(BF16)
    return hi, (x - hi.astype(F32)).astype(BF16)


BIG_CHUNK = 4 * SUBLANES


def _run_copies(tab_smem, slot, make_copy, start):
    def per_expert(e, carry):
        n_big = tab_smem[slot, e]
        n_small = tab_smem[slot, N_EXPERTS + e]
        off = tab_smem[slot, 2 * N_EXPERTS + e]
        base = tab_smem[slot, 3 * N_EXPERTS + e]

        def chunk(rows, shift):
            def one(j, c):
                cp = make_copy(pl.multiple_of(off + shift + j * rows, SUBLANES),
                               pl.multiple_of(base + shift + j * rows, SUBLANES), rows)
                if start:
                    cp.start()
                else:
                    cp.wait()
                return c
            return one

        carry = lax.fori_loop(0, n_big, chunk(BIG_CHUNK, 0), carry)
        return lax.fori_loop(0, n_small, chunk(SUBLANES, n_big * BIG_CHUNK), carry)

    lax.fori_loop(0, N_EXPERTS, per_expert, 0)


def _dispatch_kernel(tail_ref, pend_ref, nu_ref, tab_hbm, hp_ref, hs_ref, rp_ref, rs_ref, xs_ref,
                     tab_smem, local_buf, zero_buf, sem_tab, sem_rows, sem_zero, *, tiles_p, n_blocks):
    i = pl.program_id(0)
    nt = pl.num_programs(0)
    slot = i % 2
    d = hp_ref.shape[1]

    def tab_copy(step, sl):
        return pltpu.make_async_copy(tab_hbm.at[step], tab_smem.at[sl], sem_tab.at[sl])

    def zero_block(start):
        return pltpu.make_async_copy(zero_buf, xs_ref.at[pl.ds(start, MOE_ROWS)], sem_zero)

    def zero_chunk(start):
        return pltpu.make_async_copy(zero_buf.at[pl.ds(0, SUBLANES)], xs_ref.at[pl.ds(start, SUBLANES)],
                                     sem_rows)

    @pl.when(i == 0)
    def _():
        tab_copy(0, 0).start()
        zero_buf[...] = jnp.zeros(zero_buf.shape, F32)

        def tails(start):
            def per_expert(e, carry):
                def one(j, c):
                    cp = zero_chunk(pl.multiple_of(tail_ref[e] + j * SUBLANES, SUBLANES))
                    if start:
                        cp.start()
                    else:
                        cp.wait()
                    return c
                return lax.fori_loop(0, (pend_ref[e] - tail_ref[e]) // SUBLANES, one, carry)
            lax.fori_loop(0, N_EXPERTS, per_expert, 0)

        def blocks(start):
            def one(b, c):
                cp = zero_block(pl.multiple_of(b * MOE_ROWS, MOE_ROWS))
                if start:
                    cp.start()
                else:
                    cp.wait()
                return c
            lax.fori_loop(nu_ref[0], n_blocks, one, 0)

        tails(True)
        blocks(True)
        tails(False)
        blocks(False)

    def sort_tile(h_ref, rt_ref):
        local = local_buf.at[slot]
        rt_t = rt_ref[...].T
        row_id = lax.broadcasted_iota(jnp.int32, (LOCAL_ROWS, ROW_TILE), 0).astype(F32)
        pick = jnp.zeros((LOCAL_ROWS, ROW_TILE), F32)
        gate = jnp.zeros((LOCAL_ROWS, ROW_TILE), F32)
        for k in range(TOP_K):
            hit = row_id == rt_t[ROUTE_ROW + k:ROUTE_ROW + k + 1, :]
            pick = jnp.where(hit, 1.0, pick)
            gate = jnp.where(hit, rt_t[ROUTE_GATE + k:ROUTE_GATE + k + 1, :], gate)
        local[:, 0:d] = jnp.dot(pick.astype(BF16), h_ref[...].astype(BF16), preferred_element_type=F32)
        g_hi, g_lo = _split_bf16(gate)
        ones = jnp.ones((ROW_TILE, LANE), BF16)
        local[:, d:d + LANE] = (jnp.dot(g_hi, ones, preferred_element_type=F32)
                                + jnp.dot(g_lo, ones, preferred_element_type=F32))

    @pl.when(i < tiles_p)
    def _():
        sort_tile(hp_ref, rp_ref)

    @pl.when(i >= tiles_p)
    def _():
        sort_tile(hs_ref, rs_ref)

    def run_copy(buf):
        def make(off, base, rows):
            return pltpu.make_async_copy(local_buf.at[buf, pl.ds(off, rows)], xs_ref.at[pl.ds(base, rows)],
                                         sem_rows)
        return make

    @pl.when(i > 0)
    def _():
        _run_copies(tab_smem, 1 - slot, run_copy(1 - slot), False)

    @pl.when(i + 1 < nt)
    def _():
        tab_copy(i + 1, 1 - slot).start()

    tab_copy(i, slot).wait()
    _run_copies(tab_smem, slot, run_copy(slot), True)

    @pl.when(i == nt - 1)
    def _():
        _run_copies(tab_smem, slot, run_copy(slot), False)


def _dispatch(tail_start, pad_end, n_used, tab, h2p, h2s, rtp, rts, n_blocks):
    d = h2p.shape[1]
    tm = ROW_TILE
    tiles_p = h2p.shape[0] // tm
    nt = tiles_p + h2s.shape[0] // tm
    idx_p = lambda i, *_: (jnp.minimum(i, tiles_p - 1), 0)
    idx_s = lambda i, *_: (jnp.maximum(i - tiles_p, 0), 0)
    grid_spec = pltpu.PrefetchScalarGridSpec(
        num_scalar_prefetch=3,
        grid=(nt,),
        in_specs=[pl.BlockSpec(memory_space=pl.ANY),
                  pl.BlockSpec((tm, d), idx_p), pl.BlockSpec((tm, d), idx_s),
                  pl.BlockSpec((tm, LANE), idx_p), pl.BlockSpec((tm, LANE), idx_s)],
        out_specs=pl.BlockSpec(memory_space=pl.ANY),
        scratch_shapes=[pltpu.SMEM((2, TAB_WORDS), jnp.int32),
                        pltpu.VMEM((2, LOCAL_ROWS, d + LANE), F32),
                        pltpu.VMEM((MOE_ROWS, d + LANE), F32),
                        pltpu.SemaphoreType.DMA((2,)),
                        pltpu.SemaphoreType.DMA(()),
                        pltpu.SemaphoreType.DMA(())],
    )
    return pl.pallas_call(
        functools.partial(_dispatch_kernel, tiles_p=tiles_p, n_blocks=n_blocks),
        grid_spec=grid_spec,
        out_shape=jax.ShapeDtypeStruct((n_blocks * MOE_ROWS, d + LANE), F32),
        compiler_params=_cparams("arbitrary"),
        name="moe_dispatch",
    )(tail_start, pad_end, n_used, tab, h2p, h2s, rtp, rts)


def _expert_kernel(be_ref, nu_ref, x_ref, w1_ref, b1_ref, w2_ref, b2_ref, o_ref, w1b, w2b):
    i = pl.program_id(0)
    d = o_ref.shape[1]
    new_expert = jnp.logical_or(i == 0, be_ref[i] != be_ref[jnp.maximum(i - 1, 0)])

    @pl.when(jnp.logical_and(new_expert, i < nu_ref[0]))
    def _():
        w1b[...] = w1_ref[0].astype(BF16)
        w2b[...] = w2_ref[0].astype(BF16)

    @pl.when(i < nu_ref[0])
    def _():
        x = x_ref[:, 0:d].astype(BF16)
        gate = x_ref[:, d:d + 1]
        acc = None
        fc = FF_CHUNK
        for c in range(D_FF // fc):
            glu = jnp.dot(x, w1b[:, c * fc:(c + 1) * fc], preferred_element_type=F32)
            glu = glu + b1_ref[0, :, c * fc:(c + 1) * fc]
            lin = jnp.dot(x, w1b[:, D_FF + c * fc:D_FF + (c + 1) * fc],
                          preferred_element_type=F32)
            lin = lin + b1_ref[0, :, D_FF + c * fc:D_FF + (c + 1) * fc]
            glu = jnp.minimum(glu, SWIGLU_LIMIT)
            lin = jnp.clip(lin, -SWIGLU_LIMIT, SWIGLU_LIMIT)
            act = glu * _sigmoid(SWIGLU_ALPHA * glu) * (lin + 1.0)
            part = jnp.dot(act.astype(BF16), w2b[c * fc:(c + 1) * fc, :],
                           preferred_element_type=F32)
            acc = part if acc is None else acc + part
        o_ref[...] = (acc + b2_ref[0]) * gate

    @pl.when(i >= nu_ref[0])
    def _():
        o_ref[...] = jnp.zeros(o_ref.shape, F32)


def _expert_ffn(blk_e, n_used, xs, layer, w1, b1, w2, b2):
    d = w2.shape[3]
    nb = blk_e.shape[0]
    wmap = lambda i, be, nu: (layer, be[i], 0, 0)
    sq = pl.Squeezed()
    grid_spec = pltpu.PrefetchScalarGridSpec(
        num_scalar_prefetch=2,
        grid=(nb,),
        in_specs=[pl.BlockSpec((MOE_ROWS, xs.shape[1]), lambda i, be, nu: (jnp.minimum(i, nu[0] - 1), 0)),
                  pl.BlockSpec((sq, 1, d, 2 * D_FF), wmap),
                  pl.BlockSpec((sq, 1, 1, 2 * D_FF), wmap),
                  pl.BlockSpec((sq, 1, D_FF, d), wmap),
                  pl.BlockSpec((sq, 1, 1, d), wmap)],
        out_specs=pl.BlockSpec((MOE_ROWS, d), lambda i, be, nu: (i, 0)),
        scratch_shapes=[pltpu.VMEM((d, 2 * D_FF), BF16), pltpu.VMEM((D_FF, d), BF16)],
    )
    return pl.pallas_call(
        _expert_kernel,
        grid_spec=grid_spec,
        out_shape=jax.ShapeDtypeStruct((nb * MOE_ROWS, d), F32),
        compiler_params=_cparams("arbitrary"),
        name="expert_ffn",
    )(blk_e, n_used, xs, w1, b1, w2, b2)


def _combine_kernel(*refs, final):
    if final:
        (tab_hbm, x_ref, rt_ref, mod_ref, fg_ref, yb_ref, x_out_ref, yn_ref,
         tab_smem, local_buf, sem_tab, sem_rows) = refs
    else:
        (tab_hbm, x_ref, rt_ref, mod_ref, yb_ref, x_out_ref,
         tab_smem, local_buf, sem_tab, sem_rows) = refs
    i = pl.program_id(0)
    nt = pl.num_programs(0)
    slot = i % 2

    def tab_copy(step):
        return pltpu.make_async_copy(tab_hbm.at[step], tab_smem.at[step % 3], sem_tab.at[step % 3])

    def run_copy(buf):
        def make(off, base, rows):
            return pltpu.make_async_copy(yb_ref.at[pl.ds(base, rows)], local_buf.at[buf, pl.ds(off, rows)],
                                         sem_rows.at[buf])
        return make

    @pl.when(i == 0)
    def _():
        local_buf[...] = jnp.zeros(local_buf.shape, F32)
        tab_copy(0).start()
        tab_copy(0).wait()
        _run_copies(tab_smem, 0, run_copy(0), True)

        @pl.when(nt > 1)
        def _():
            tab_copy(1).start()

    @pl.when(i + 1 < nt)
    def _():
        tab_copy(i + 1).wait()
        _run_copies(tab_smem, (i + 1) % 3, run_copy(1 - slot), True)

    @pl.when(i + 2 < nt)
    def _():
        tab_copy(i + 2).start()

    _run_copies(tab_smem, i % 3, run_copy(slot), False)
    local = local_buf.at[slot]

    rt = rt_ref[...]
    col_id = lax.broadcasted_iota(jnp.int32, (ROW_TILE, LOCAL_ROWS), 1).astype(F32)
    pick = jnp.zeros((ROW_TILE, LOCAL_ROWS), F32)
    for k in range(TOP_K):
        pick = jnp.where(col_id == rt[:, ROUTE_ROW + k:ROUTE_ROW + k + 1], 1.0, pick)
    pick = pick.astype(BF16)
    y_hi, y_lo = _split_bf16(local[...])
    moe = (jnp.dot(pick, y_hi, preferred_element_type=F32)
           + jnp.dot(pick, y_lo, preferred_element_type=F32))
    x2 = x_ref[...] + mod_ref[0, 5:6, :] * moe
    x_out_ref[...] = x2
    if final:
        yn_ref[...] = _rms(x2, fg_ref[...])


def _combine(x, route, tab, yb, mod, group_of_tile, final_g):
    n, d = x.shape
    tm = ROW_TILE
    nt = n // tm
    final = final_g is not None
    in_specs = [pl.BlockSpec(memory_space=pl.ANY),
                pl.BlockSpec((tm, d), lambda i: (i, 0)),
                pl.BlockSpec((tm, LANE), lambda i: (i, 0)),
                pl.BlockSpec((1, 6, d), lambda i: (group_of_tile(i), 0, 0))]
    args = [tab, x, route, mod]
    out_shape = [jax.ShapeDtypeStruct((n, d), F32)]
    if final:
        in_specs.append(pl.BlockSpec((1, d), lambda i: (0, 0)))
        args.append(final_g)
        out_shape.append(jax.ShapeDtypeStruct((n, d), F32))
    in_specs.append(pl.BlockSpec(memory_space=pl.ANY))
    args.append(yb)
    out_specs = [pl.BlockSpec((tm, d), lambda i: (i, 0)) for _ in out_shape]
    return pl.pallas_call(
        functools.partial(_combine_kernel, final=final),
        grid=(nt,),
        in_specs=in_specs,
        out_specs=out_specs,
        out_shape=out_shape,
        scratch_shapes=[pltpu.SMEM((3, TAB_WORDS), jnp.int32),
                        pltpu.VMEM((2, LOCAL_ROWS, d), F32),
                        pltpu.SemaphoreType.DMA((3,)),
                        pltpu.SemaphoreType.DMA((2,))],
        compiler_params=_cparams("arbitrary"),
        name="moe_combine_final" if final else "moe_combine",
    )(*args)


def _block_plan(counts_f, n_tiles):
    counts = counts_f[0, :N_EXPERTS].astype(jnp.int32)
    padded = (counts + MOE_ROWS - 1) // MOE_ROWS * MOE_ROWS
    pad_end = jnp.cumsum(padded)
    pad_start = pad_end - padded
    max_rows = n_tiles * LOCAL_ROWS
    nb = -(-max_rows // MOE_ROWS) + N_EXPERTS
    first_row = jnp.arange(nb, dtype=jnp.int32) * MOE_ROWS
    blk_e = jnp.minimum(jnp.sum((pad_end[None, :] <= first_row[:, None]).astype(jnp.int32), axis=1),
                        N_EXPERTS - 1)
    n_used = (pad_end[-1] // MOE_ROWS).reshape(1)
    return pad_start, pad_start + counts, pad_end, blk_e, n_used, nb


def _run_table(tab_f, pad_start):
    t = tab_f[:, :, :N_EXPERTS].astype(jnp.int32)
    n_big = t[:, TAB_LEN] // BIG_CHUNK
    n_small = (t[:, TAB_LEN] - n_big * BIG_CHUNK) // SUBLANES
    base = t[:, TAB_BASE] + pad_start[None, :]
    return jnp.concatenate([n_big, n_small, t[:, TAB_OFF], base], axis=1)


def _to_half_split(x):
    return jnp.concatenate([x[..., 0::2], x[..., 1::2]], axis=-1)


def _from_half_split(x):
    n = x.shape[-1] // 2
    return jnp.stack([x[..., :n], x[..., n:]], axis=-1).reshape(x.shape)


def _pad_heads(w, heads, hd):
    k = w.shape[0]
    w = w.reshape(k, heads, hd)
    return jnp.pad(w, ((0, 0), (0, 0), (0, LANE - hd))).reshape(k, heads * LANE)


def _pad_rows(w, heads, hd):
    n = w.shape[1]
    w = w.reshape(heads, hd, n)
    return jnp.pad(w, ((0, 0), (0, LANE - hd), (0, 0))).reshape(heads * LANE, n)


def _rope_tables(t, rot_dim, lane_off):
    tok = jnp.arange(t)
    row = (tok // GRID_W).astype(F32)
    col = (tok % GRID_W).astype(F32)
    nf = rot_dim // 4
    inv = ROPE_THETA ** (-jnp.arange(nf, dtype=F32) / nf)
    ang = jnp.concatenate([row[:, None] * inv, col[:, None] * inv], axis=-1)
    cos, sin = jnp.cos(ang), jnp.sin(ang)
    half = rot_dim // 2
    fill = lambda v, w: jnp.full((t, w), v, F32)
    rest = LANE - lane_off - rot_dim
    cos_t = jnp.concatenate([fill(1.0, lane_off), cos, cos, fill(1.0, rest)], axis=1)
    sin_a = jnp.concatenate([fill(0.0, lane_off), -sin, fill(0.0, half + rest)], axis=1)
    sin_b = jnp.concatenate([fill(0.0, lane_off + half), sin, fill(0.0, rest)], axis=1)
    return jnp.stack([cos_t, sin_a, sin_b])


def _na_bias_table(rpb):
    c = np.arange(GRID_W)[:, None]
    kc = np.arange(GRID_W)[None, :]
    wstart = np.clip(c - NA_KW // 2, 0, GRID_W - NA_KW)
    mask = (kc >= wstart) & (kc < wstart + NA_KW)
    n_dc = 2 * NA_KW - 1
    dcol = np.clip(kc - c + NA_KW - 1, 0, n_dc - 1)
    pick = (dcol[None] == np.arange(n_dc)[:, None, None]).astype(np.float32)
    toep = jnp.einsum('hrd,dck->hrck', rpb.astype(F32), pick, precision=HIGHEST)
    toep = jnp.where(mask[None, None], toep, NEG_BIG)
    tab = jnp.stack([toep[:, d0:d0 + NA_KR] for d0 in range(NA_KR)], axis=1)
    tab = jnp.transpose(tab, (0, 1, 3, 2, 4))
    return tab.reshape(NA_HEADS, NA_KR, GRID_W, NA_KR * GRID_W)


def kernel(x_prompt, x_sample, c, cache_mla_ckv, cache_mla_krope, cache_na_k, cache_na_v, state_mlstm_C, state_mlstm_n, state_mlstm_m, cache_diff_k, cache_diff_v, c_ctx, w_mod, b_mod, norm_g, final_g, a_w_in, a_q_norm, a_w_uq, a_kv_norm, a_w_ukv, a_rpb, a_w_out, b_w_in, b_gate_bias, b_ml_norm, b_df_lambda, b_df_subln, b_w_out, moe_w_router, moe_b_router, moe_w1, moe_b1, moe_w2, moe_b2):
    bp, lp, d = x_prompt.shape
    bs, ts, _ = x_sample.shape
    n_p, n_s = bp * lp, bs * ts
    past = cache_mla_ckv.shape[2]
    tiles_per_batch = ts // ROW_TILE
    grp_p = lambda i: 0
    grp_s = lambda i: 1 + i // tiles_per_batch

    xp = x_prompt.reshape(n_p, d)
    xs = x_sample.reshape(n_s, d)
    cond8 = jnp.concatenate([c_ctx[None, :], c, jnp.zeros((8 - 1 - bs, d), F32)], axis=0)
    mod_all = _modulation(cond8, w_mod, b_mod)


    def moe_layer(l, x1p, x1s, h2p, h2s, rtp, rts, tabp, tabs_, counts, final):
        n_tiles = (n_p + n_s) // ROW_TILE
        pad_start, tail_start, pad_end, blk_e, n_used, nb = _block_plan(counts, n_tiles)
        tab_p = _run_table(tabp, pad_start)
        tab_s = _run_table(tabs_, pad_start)
        xs_rows = _dispatch(tail_start, pad_end, n_used, jnp.concatenate([tab_p, tab_s], axis=0),
                            h2p, h2s, rtp, rts, nb)
        yb = _expert_ffn(blk_e, n_used, xs_rows, l, moe_w1, moe_b1[:, :, None, :],
                         moe_w2, moe_b2[:, :, None, :])
        fg = final_g.reshape(1, d) if final else None
        outp = _combine(x1p, rtp, tab_p, yb, mod_all[l], grp_p, fg)
        outs = _combine(x1s, rts, tab_s, yb, mod_all[l], grp_s, fg)
        return outp, outs

    def router_w(l):
        wr = jnp.pad(moe_w_router[l], ((0, 0), (0, LANE - N_EXPERTS)))
        br = jnp.pad(moe_b_router[l], (0, LANE - N_EXPERTS)).reshape(1, LANE)
        return wr, br

    l, e = 0, 0
    w_in = a_w_in[e]
    kr_cols = _to_half_split(w_in[:, 384:416])
    w0 = jnp.concatenate([
        w_in[:, :384], jnp.pad(kr_cols, ((0, 0), (0, LANE - MLA_ROPE))),
        _pad_heads(w_in[:, 416:928], NA_HEADS, NA_HD),
        _pad_heads(w_in[:, 928:1440], NA_HEADS, NA_HD),
        _pad_heads(w_in[:, 1440:1952], NA_HEADS, NA_HD)], axis=1).astype(BF16)
    wuq = a_w_uq[e].reshape(MLA_Q_LORA, MLA_HEADS, MLA_NOPE + MLA_ROPE)
    wuq = jnp.concatenate([wuq[:, :, :MLA_NOPE], _to_half_split(wuq[:, :, MLA_NOPE:])], axis=2)
    wuq = _pad_heads(wuq.reshape(MLA_Q_LORA, -1), MLA_HEADS, MLA_NOPE + MLA_ROPE).astype(BF16)
    wukv = a_w_ukv[e].reshape(MLA_KV_LORA, MLA_HEADS, MLA_NOPE + MLA_V)
    wk = _pad_heads(wukv[:, :, :MLA_NOPE].reshape(MLA_KV_LORA, -1), MLA_HEADS, MLA_NOPE).astype(BF16)
    wv = _pad_heads(wukv[:, :, MLA_NOPE:].reshape(MLA_KV_LORA, -1), MLA_HEADS, MLA_V).astype(BF16)
    sel_np = np.zeros((LANE, MLA_HEADS * LANE), np.float32)
    for hd in range(MLA_HEADS):
        sel_np[np.arange(MLA_ROPE), hd * LANE + MLA_NOPE + np.arange(MLA_ROPE)] = 1.0
    sel = jnp.asarray(sel_np, BF16)
    qn_g = a_q_norm[e].reshape(1, -1)
    kvn_g = a_kv_norm[e].reshape(1, -1)
    g1 = norm_g[l, 0].reshape(1, d)
    g2 = norm_g[l, 1].reshape(1, d)
    tabs_a = (_rope_tables(ts, MLA_ROPE, MLA_NOPE), _rope_tables(ts, MLA_ROPE, 0))
    wide = MLA_HEADS * LANE

    q_p, ckv_p, kr_p, nq_p, nk_p, nv_p = _even_proj(xp, g1, mod_all[l], w0, qn_g, wuq, kvn_g, grp_p, None, F32)
    new_mla_ckv = ckv_p.reshape(bp, 1, lp, MLA_KV_LORA)
    new_mla_krope = _from_half_split(kr_p[:, :MLA_ROPE]).reshape(bp, 1, lp, MLA_ROPE)
    new_na_k = nk_p.reshape(bp, 1, lp, NA_HEADS, LANE)[..., :NA_HD]
    new_na_v = nv_p.reshape(bp, 1, lp, NA_HEADS, LANE)[..., :NA_HD]
    k_p, v_p = _kv_expand(ckv_p, kr_p, wk, sel, wv)
    ident = lambda hd: hd
    o_mla_p = _flash(q_p.reshape(bp, lp, wide), k_p.reshape(bp, lp, wide), v_p.reshape(bp, lp, wide),
                     v_of_head=ident, out_dtype=BF16)
    o_na_p = _flash(nq_p.reshape(bp, lp, wide), nk_p.astype(BF16).reshape(bp, lp, wide),
                    nv_p.astype(BF16).reshape(bp, lp, wide),
                    v_of_head=ident, out_dtype=BF16)
    q_s, ckv_s, kr_s, nq_s, nk_s, nv_s = _even_proj(xs, g1, mod_all[l], w0, qn_g, wuq, kvn_g, grp_s, tabs_a, BF16)
    ckv_all = jnp.concatenate([cache_mla_ckv[:, e], ckv_s.reshape(bs, ts, LANE)], axis=1)
    kr_cache = jnp.pad(_to_half_split(cache_mla_krope[:, e]), ((0, 0), (0, 0), (0, LANE - MLA_ROPE)))
    kr_all = jnp.concatenate([kr_cache, kr_s.reshape(bs, ts, LANE)], axis=1)
    n_all = past + ts
    k_s, v_s = _kv_expand(ckv_all.reshape(bs * n_all, LANE), kr_all.reshape(bs * n_all, LANE), wk, sel, wv)
    o_mla_s = _flash(q_s.reshape(bs, ts, wide), k_s.reshape(bs, n_all, wide), v_s.reshape(bs, n_all, wide),
                     v_of_head=ident, out_dtype=BF16)
    pad_hd = ((0, 0), (0, 0), (0, 0), (0, LANE - NA_HD))
    kc = jnp.pad(cache_na_k[:, e], pad_hd).astype(BF16).reshape(bs, past, wide)
    vc = jnp.pad(cache_na_v[:, e], pad_hd).astype(BF16).reshape(bs, past, wide)
    o_na_s = _na_latent(nq_s.reshape(bs, ts, wide), nk_s.reshape(bs, ts, wide), nv_s.reshape(bs, ts, wide),
                        kc, vc, _na_bias_table(a_rpb[e]))
    w_out = a_w_out[e]
    wa = _pad_rows(w_out[:MLA_HEADS * MLA_V], MLA_HEADS, MLA_V).astype(BF16)
    wb = _pad_rows(w_out[MLA_HEADS * MLA_V:], NA_HEADS, NA_HD).astype(BF16)
    wr, br = router_w(l)
    cnt0 = jnp.zeros((1, LANE), F32)
    x1p, h2p, rtp, tbp, cnt_p = _post_mix(_post_even_kernel, xp,
                                     [o_mla_p.reshape(n_p, wide), o_na_p.reshape(n_p, wide)],
                                     [wa, wb], mod_all[l], g2, wr, br, cnt0, grp_p, "post_even")
    x1s, h2s, rts, tbs, cnt_s = _post_mix(_post_even_kernel, xs,
                                     [o_mla_s.reshape(n_s, wide), o_na_s.reshape(n_s, wide)],
                                     [wa, wb], mod_all[l], g2, wr, br, cnt_p, grp_s, "post_even")
    (xp,), (xs,) = moe_layer(l, x1p, x1s, h2p, h2s, rtp, rts, tbp, tbs, cnt_s, False)

    l, o = 1, 0
    lam_init = 0.8 - 0.6 * math.exp(-0.3 * l)
    w_in = b_w_in[o]
    cuts = np.cumsum([0, 256, 256, 512, 512, 16, 512, 512, 512])
    seg = lambda i: w_in[:, cuts[i]:cuts[i + 1]]

    def diff_cols(w):
        w = _to_half_split(w.reshape(d, 2 * DF_HEADS, DF_QK))
        return _pad_heads(w.reshape(d, -1), 2 * DF_HEADS, DF_QK)

    w1p = jnp.concatenate([
        _pad_heads(seg(0), ML_HEADS, ML_DQK), _pad_heads(seg(1), ML_HEADS, ML_DQK), seg(2), seg(3),
        diff_cols(seg(5)), diff_cols(seg(6)), seg(7),
        jnp.pad(seg(4), ((0, 0), (0, LANE - 16)))], axis=1).astype(BF16)
    bg = jnp.pad(b_gate_bias[o].reshape(1, 16), ((0, 0), (0, LANE - 16)))
    g1 = norm_g[l, 0].reshape(1, d)
    g2 = norm_g[l, 1].reshape(1, d)
    tab_d = _rope_tables(ts, DF_QK, 0)
    df_v_of = lambda hd: hd // 2

    mq_p, mk_p, mv_p, mo_p, gt_p, dq_p, dk_p, dv_p = _odd_proj(xp, g1, mod_all[l], w1p, bg, grp_p, None, F32)
    new_diff_k = _from_half_split(dk_p.reshape(bp, 1, lp, DF_HEADS, 2, LANE)[..., :DF_QK])
    new_diff_v = dv_p.reshape(bp, 1, lp, DF_HEADS, DF_V)
    mw = ML_HEADS * LANE
    zc = jnp.zeros((bp, ML_HEADS, LANE, LANE), F32)
    zn = jnp.zeros((bp, ML_HEADS, 1, LANE), F32)
    seq_p = lambda a: a.reshape(bp, lp, -1)
    hf_p, cf_f, nf_f, mf_f = _mlstm_dir(seq_p(mq_p), seq_p(mk_p), seq_p(mv_p), seq_p(gt_p), zc, zn, zn,
                                        direction=0, bb=4)
    hb_p, cf_b, nf_b, mf_b = _mlstm_dir(seq_p(mq_p), seq_p(mk_p), seq_p(mv_p), seq_p(gt_p), zc, zn, zn,
                                        direction=1, bb=4)
    new_mlstm_C = jnp.stack([cf_f, cf_b], axis=1)[..., :ML_DQK][:, None]
    new_mlstm_n = jnp.stack([nf_f, nf_b], axis=1)[:, :, :, 0, :ML_DQK][:, None]
    new_mlstm_m = jnp.stack([mf_f, mf_b], axis=1)[:, :, :, 0, 0][:, None]
    od_p = _flash(seq_p(dq_p), seq_p(dk_p.astype(BF16)), seq_p(dv_p.astype(BF16)),
                  v_of_head=df_v_of, out_dtype=F32)

    mq_s, mk_s, mv_s, mo_s, gt_s, dq_s, dk_s, dv_s = _odd_proj(xs, g1, mod_all[l], w1p, bg, grp_s, tab_d, BF16)
    seq_s = lambda a: a.reshape(bs, ts, -1)
    c0 = jnp.pad(state_mlstm_C[:, o], ((0, 0), (0, 0), (0, 0), (0, 0), (0, LANE - ML_DQK)))
    n0 = jnp.pad(state_mlstm_n[:, o], ((0, 0), (0, 0), (0, 0), (0, LANE - ML_DQK)))[:, :, :, None, :]
    m0 = jnp.broadcast_to(state_mlstm_m[:, o][:, :, :, None, None], (bs, 2, ML_HEADS, 1, LANE))
    hf_s = _mlstm_dir(seq_s(mq_s), seq_s(mk_s), seq_s(mv_s), seq_s(gt_s), c0[:, 0], n0[:, 0], m0[:, 0],
                      direction=0, bb=4)[0]
    hb_s = _mlstm_dir(seq_s(mq_s), seq_s(mk_s), seq_s(mv_s), seq_s(gt_s), c0[:, 1], n0[:, 1], m0[:, 1],
                      direction=1, bb=4)[0]
    dk_cache = _to_half_split(cache_diff_k[:, o])
    dk_cache = jnp.pad(dk_cache, ((0, 0),) * 4 + ((0, LANE - DF_QK),)).astype(BF16).reshape(bs, past, 1024)
    dk_all = jnp.concatenate([dk_cache, seq_s(dk_s)], axis=1)
    dv_all = jnp.concatenate([cache_diff_v[:, o].astype(BF16).reshape(bs, past, 512), seq_s(dv_s)], axis=1)
    od_s = _flash(seq_s(dq_s), dk_all, dv_all, v_of_head=df_v_of, out_dtype=F32)

    w_out = b_w_out[o]
    wa = w_out[:ML_HEADS * ML_DV].astype(BF16)
    wb = w_out[ML_HEADS * ML_DV:].astype(BF16)
    mln = b_ml_norm[o].reshape(1, ML_HEADS * ML_DV)
    sub = b_df_subln[o].reshape(1, DF_V)
    lam_p = jnp.pad(b_df_lambda[o], ((0, 4), (0, LANE - DF_QK)))
    wr, br = router_w(l)
    post_odd = functools.partial(_post_odd_kernel, lam_init=lam_init)
    x1p, h2p, rtp, tbp, cnt_p = _post_mix(post_odd, xp, [hf_p.reshape(n_p, mw), hb_p.reshape(n_p, mw), mo_p,
                                                    od_p.reshape(n_p, 1024)],
                                     [mln, sub, lam_p, wa, wb], mod_all[l], g2, wr, br, cnt0, grp_p, "post_odd")
    x1s, h2s, rts, tbs, cnt_s = _post_mix(post_odd, xs, [hf_s.reshape(n_s, mw), hb_s.reshape(n_s, mw), mo_s,
                                                    od_s.reshape(n_s, 1024)],
                                     [mln, sub, lam_p, wa, wb], mod_all[l], g2, wr, br, cnt_p, grp_s, "post_odd")
    (_, y_p), (_, y_s) = moe_layer(l, x1p, x1s, h2p, h2s, rtp, rts, tbp, tbs, cnt_s, True)

    return (y_p.reshape(bp, lp, d), y_s.reshape(bs, ts, d), new_mla_ckv, new_mla_krope, new_na_k, new_na_v,
            new_mlstm_C, new_mlstm_n, new_mlstm_m, new_diff_k, new_diff_v)
```

```python
import functools
import math

import numpy as np
import jax
import jax.numpy as jnp
from jax import lax
from jax.experimental import pallas as pl
from jax.experimental.pallas import tpu as pltpu

F32 = jnp.float32
BF16 = jnp.bfloat16
HIGHEST = lax.Precision.HIGHEST

D_MODEL = 1024
GRID_W = 64
LANE = 128
EPS = 1e-6
ROPE_THETA = 10000.0

MLA_HEADS, MLA_Q_LORA, MLA_KV_LORA, MLA_NOPE, MLA_ROPE, MLA_V = 8, 256, 128, 64, 32, 64
NA_HEADS, NA_HD, NA_KR, NA_KW = 8, 64, 8, 16
ML_HEADS, ML_DQK, ML_DV, ML_CHUNK = 4, 64, 128, 64
DF_HEADS, DF_QK, DF_V = 4, 64, 128
N_EXPERTS, TOP_K, D_FF = 32, 4, 1024
SWIGLU_LIMIT, SWIGLU_ALPHA = 7.0, 1.702
MLA_SCALE = (MLA_NOPE + MLA_ROPE) ** -0.5

ROW_TILE = 256
POST_TILES = 2
MOE_ROWS = 512
ATT_TQ, ATT_TK = 2048, 256
MXU_AHEAD = 2
FF_CHUNK = 512
NEG_BIG = -1e30
VMEM_LIMIT = 56 * 1024 * 1024


def _cparams(*sem):
    return pltpu.CompilerParams(dimension_semantics=sem, vmem_limit_bytes=VMEM_LIMIT)


def _rms(x, g):
    return x * lax.rsqrt(jnp.mean(x * x, axis=-1, keepdims=True) + EPS) * g


def _sigmoid(x):
    return 1.0 / (1.0 + jnp.exp(-x))


def _mod_kernel(c_ref, w_ref, b_ref, o_ref):
    c = c_ref[...]
    s = c * _sigmoid(c)
    o_ref[0] = jnp.dot(s, w_ref[0], precision=HIGHEST, preferred_element_type=F32) + b_ref[0]


def _modulation(cond8, w_mod, b_mod):
    depth, d, n6 = w_mod.shape
    tn = 1024
    out = pl.pallas_call(
        _mod_kernel,
        grid=(depth, n6 // tn),
        in_specs=[
            pl.BlockSpec((8, d), lambda l, j: (0, 0)),
            pl.BlockSpec((1, d, tn), lambda l, j: (l, 0, j)),
            pl.BlockSpec((1, 1, tn), lambda l, j: (l, 0, j)),
        ],
        out_specs=pl.BlockSpec((1, 8, tn), lambda l, j: (l, 0, j)),
        out_shape=jax.ShapeDtypeStruct((depth, 8, n6), F32),
        compiler_params=_cparams("parallel", "parallel"),
        name="modulation",
    )(cond8, w_mod, b_mod.reshape(depth, 1, n6))
    return out.reshape(depth, 8, 6, d)


def _rope_block(x, cos, sin_a, sin_b, half):
    up = pltpu.roll(x, LANE - half, axis=1)
    dn = pltpu.roll(x, half, axis=1)
    return x * cos + up * sin_a + dn * sin_b


def _even_proj_kernel(*refs, rope):
    if rope:
        (x_ref, g_ref, mod_ref, w_ref, qn_ref, wuq_ref, kvn_ref, tq_ref, tk_ref,
         q_ref, ckv_ref, kr_ref, nq_ref, nk_ref, nv_ref) = refs
    else:
        (x_ref, g_ref, mod_ref, w_ref, qn_ref, wuq_ref, kvn_ref,
         q_ref, ckv_ref, kr_ref, nq_ref, nk_ref, nv_ref) = refs
    x = x_ref[...]
    h = _rms(x, g_ref[...]) * (1.0 + mod_ref[0, 1:2, :]) + mod_ref[0, 0:1, :]
    hb = h.astype(BF16)

    def proj(a, b):
        return jnp.dot(hb, w_ref[:, a:b], preferred_element_type=F32)

    q_lat = proj(0, 256)
    qn = _rms(q_lat, qn_ref[...]).astype(BF16)
    for hd in range(MLA_HEADS):
        qh = jnp.dot(qn, wuq_ref[:, hd * LANE:(hd + 1) * LANE], preferred_element_type=F32)
        if rope:
            qh = _rope_block(qh, tq_ref[0], tq_ref[1], tq_ref[2], MLA_ROPE // 2)
        q_ref[:, hd * LANE:(hd + 1) * LANE] = (qh * _exp2_scale(MLA_SCALE)).astype(q_ref.dtype)
    ckv_ref[...] = _rms(proj(256, 384), kvn_ref[...])
    kr = proj(384, 512)
    if rope:
        kr = _rope_block(kr, tk_ref[0], tk_ref[1], tk_ref[2], MLA_ROPE // 2)
    kr_ref[...] = kr
    nq = proj(512, 1536) if rope else proj(512, 1536) * _exp2_scale(NA_HD ** -0.5)
    nq_ref[...] = nq.astype(nq_ref.dtype)
    nk_ref[...] = proj(1536, 2560).astype(nk_ref.dtype)
    nv_ref[...] = proj(2560, 3584).astype(nv_ref.dtype)


def _even_proj(x, g, mod, w, qn, wuq, kvn, group_of_tile, rope_tabs, kv_dtype):
    n, d = x.shape
    tm = ROW_TILE
    rope = rope_tabs is not None
    in_specs = [
        pl.BlockSpec((tm, d), lambda i: (i, 0)),
        pl.BlockSpec((1, d), lambda i: (0, 0)),
        pl.BlockSpec((1, 6, d), lambda i: (group_of_tile(i), 0, 0)),
        pl.BlockSpec(w.shape, lambda i: (0, 0)),
        pl.BlockSpec(qn.shape, lambda i: (0, 0)),
        pl.BlockSpec(wuq.shape, lambda i: (0, 0)),
        pl.BlockSpec(kvn.shape, lambda i: (0, 0)),
    ]
    args = [x, g, mod, w, qn, wuq, kvn]
    if rope:
        tq, tk = rope_tabs
        nblk = tq.shape[1] // tm
        in_specs += [pl.BlockSpec((3, tm, LANE), lambda i: (0, i % nblk, 0)),
                     pl.BlockSpec((3, tm, LANE), lambda i: (0, i % nblk, 0))]
        args += [tq, tk]
    wide = MLA_HEADS * LANE
    out_shape = [
        jax.ShapeDtypeStruct((n, wide), BF16),
        jax.ShapeDtypeStruct((n, LANE), F32),
        jax.ShapeDtypeStruct((n, LANE), F32),
        jax.ShapeDtypeStruct((n, wide), BF16),
        jax.ShapeDtypeStruct((n, wide), kv_dtype),
        jax.ShapeDtypeStruct((n, wide), kv_dtype),
    ]
    out_specs = [pl.BlockSpec((tm, s.shape[1]), lambda i: (i, 0)) for s in out_shape]
    return pl.pallas_call(
        functools.partial(_even_proj_kernel, rope=rope),
        grid=(n // tm,),
        in_specs=in_specs,
        out_specs=out_specs,
        out_shape=out_shape,
        compiler_params=_cparams("parallel"),
        name="even_proj_rope" if rope else "even_proj",
    )(*args)


def _kv_expand_kernel(ckv_ref, kr_ref, wk_ref, sel_ref, wv_ref, k_ref, v_ref):
    c = ckv_ref[...].astype(BF16)
    r = kr_ref[...].astype(BF16)
    k = jnp.dot(c, wk_ref[...], preferred_element_type=F32)
    k = k + jnp.dot(r, sel_ref[...], preferred_element_type=F32)
    k_ref[...] = k.astype(BF16)
    v_ref[...] = jnp.dot(c, wv_ref[...], preferred_element_type=F32).astype(BF16)


def _kv_expand(ckv, kr, wk, sel, wv):
    n = ckv.shape[0]
    tm = ROW_TILE
    wide = MLA_HEADS * LANE
    const = lambda i: (0, 0)
    return pl.pallas_call(
        _kv_expand_kernel,
        grid=(n // tm,),
        in_specs=[pl.BlockSpec((tm, LANE), lambda i: (i, 0)),
                  pl.BlockSpec((tm, LANE), lambda i: (i, 0)),
                  pl.BlockSpec(wk.shape, const), pl.BlockSpec(sel.shape, const),
                  pl.BlockSpec(wv.shape, const)],
        out_specs=[pl.BlockSpec((tm, wide), lambda i: (i, 0))] * 2,
        out_shape=[jax.ShapeDtypeStruct((n, wide), BF16)] * 2,
        compiler_params=_cparams("parallel"),
        name="kv_expand",
    )(ckv, kr, wk, sel, wv)


def _exp2_scale(scale):
    return scale * math.log2(math.e)


def _flash_kernel(q_ref, k_ref, v_ref, o_ref, m_sc, l_sc, acc_sc, *, heads, v_of_head):
    kj = pl.program_id(2)

    @pl.when(kj == 0)
    def _():
        m_sc[...] = jnp.full(m_sc.shape, NEG_BIG, F32)
        l_sc[...] = jnp.zeros(l_sc.shape, F32)
        acc_sc[...] = jnp.zeros(acc_sc.shape, F32)

    def scores(hd):
        sl = slice(hd * LANE, (hd + 1) * LANE)
        return lax.dot_general(k_ref[0, :, sl], q_ref[0, :, sl], (((1,), (1,)), ((), ())),
                               preferred_element_type=F32)

    pending = [scores(hd) for hd in range(min(MXU_AHEAD, heads))]
    for hd in range(heads):
        sl = slice(hd * LANE, (hd + 1) * LANE)
        vh = v_of_head(hd)
        s_t = pending.pop(0)
        if hd + MXU_AHEAD < heads:
            pending.append(scores(hd + MXU_AHEAD))
        m_prev = m_sc[hd]
        m_new = jnp.maximum(m_prev, jnp.max(s_t, axis=0, keepdims=True))
        alpha = jnp.exp2(m_prev - m_new)
        p_t = jnp.exp2(s_t - m_new)
        l_sc[hd] = alpha * l_sc[hd] + jnp.sum(p_t, axis=0, keepdims=True)
        pv_t = lax.dot_general(v_ref[0, :, vh * LANE:(vh + 1) * LANE], p_t.astype(BF16),
                               (((0,), (0,)), ((), ())), preferred_element_type=F32)
        acc_sc[sl, :] = alpha * acc_sc[sl, :] + pv_t
        m_sc[hd] = m_new

    @pl.when(kj == pl.num_programs(2) - 1)
    def _():
        for hd in range(heads):
            sl = slice(hd * LANE, (hd + 1) * LANE)
            o_ref[0, :, sl] = (acc_sc[sl, :] / l_sc[hd]).T.astype(o_ref.dtype)


def _flash(q, k, v, *, v_of_head, out_dtype):
    b, s, wq = q.shape
    nk = k.shape[1]
    wv = v.shape[2]
    heads = wq // LANE
    tq = min(ATT_TQ, s)
    tk = min(ATT_TK, nk)
    return pl.pallas_call(
        functools.partial(_flash_kernel, heads=heads, v_of_head=v_of_head),
        grid=(b, s // tq, nk // tk),
        in_specs=[pl.BlockSpec((1, tq, wq), lambda bi, qi, ki: (bi, qi, 0)),
                  pl.BlockSpec((1, tk, wq), lambda bi, qi, ki: (bi, ki, 0)),
                  pl.BlockSpec((1, tk, wv), lambda bi, qi, ki: (bi, ki, 0))],
        out_specs=pl.BlockSpec((1, tq, wq), lambda bi, qi, ki: (bi, qi, 0)),
        out_shape=jax.ShapeDtypeStruct((b, s, wq), out_dtype),
        scratch_shapes=[pltpu.VMEM((heads, 1, tq), F32), pltpu.VMEM((heads, 1, tq), F32),
                        pltpu.VMEM((wq, tq), F32)],
        compiler_params=_cparams("parallel", "parallel", "arbitrary"),
        name="flash_attention",
    )(q, k, v)


NA_ROWS_PER_STEP = 4


def _na_kernel(q_ref, k_ref, v_ref, kc_ref, vc_ref, *rest, rows, scale):
    bias_refs, o_ref = rest[:-1], rest[-1]
    nloc = NA_KR * GRID_W
    dn = (((1,), (1,)), ((), ()))
    starts = []
    for t in range(NA_ROWS_PER_STEP):
        r = pl.program_id(1) * NA_ROWS_PER_STEP + t
        r0 = jnp.clip(r - NA_KR // 2, 0, rows - NA_KR)
        starts.append(pl.multiple_of(r0 * GRID_W, GRID_W))

    def scores(idx):
        t, hd = divmod(idx, NA_HEADS)
        sl = slice(hd * LANE, (hd + 1) * LANE)
        qh = q_ref[0, t * GRID_W:(t + 1) * GRID_W, sl]
        kh = k_ref[0, pl.ds(starts[t], nloc), sl]
        return (lax.dot_general(qh, kh, dn, preferred_element_type=F32),
                lax.dot_general(qh, kc_ref[0, :, sl], dn, preferred_element_type=F32))

    n_chain = NA_ROWS_PER_STEP * NA_HEADS
    ahead = 2 * MXU_AHEAD
    pending = [scores(idx) for idx in range(min(ahead, n_chain))]
    for idx in range(n_chain):
        t, hd = divmod(idx, NA_HEADS)
        sl = slice(hd * LANE, (hd + 1) * LANE)
        s_loc, s_ctx = pending.pop(0)
        if idx + ahead < n_chain:
            pending.append(scores(idx + ahead))
        vh = v_ref[0, pl.ds(starts[t], nloc), sl]
        s_loc = s_loc * scale + bias_refs[t][hd, 0]
        s_ctx = s_ctx * scale
        m = jnp.maximum(jnp.max(s_loc, axis=1, keepdims=True), jnp.max(s_ctx, axis=1, keepdims=True))
        p_loc = jnp.exp(s_loc - m)
        p_ctx = jnp.exp(s_ctx - m)
        l = jnp.sum(p_loc, axis=1, keepdims=True) + jnp.sum(p_ctx, axis=1, keepdims=True)
        o = (jnp.dot(p_loc.astype(BF16), vh, preferred_element_type=F32)
             + jnp.dot(p_ctx.astype(BF16), vc_ref[0, :, sl], preferred_element_type=F32))
        o_ref[0, t * GRID_W:(t + 1) * GRID_W, sl] = (o / l).astype(o_ref.dtype)


def _na_latent(q, k, v, kc, vc, bias_tab):
    b, s, wide = q.shape
    rows = s // GRID_W
    lc = kc.shape[1]
    per = NA_ROWS_PER_STEP

    def bias_spec(t):
        def idx(bi, j):
            r = j * per + t
            return (0, jnp.clip(r - NA_KR // 2, 0, rows - NA_KR) - r + NA_KR - 1, 0, 0)
        return pl.BlockSpec((NA_HEADS, 1, GRID_W, NA_KR * GRID_W), idx)

    return pl.pallas_call(
        functools.partial(_na_kernel, rows=rows, scale=NA_HD ** -0.5),
        grid=(b, rows // per),
        in_specs=[pl.BlockSpec((1, per * GRID_W, wide), lambda bi, j: (bi, j, 0)),
                  pl.BlockSpec((1, s, wide), lambda bi, j: (bi, 0, 0)),
                  pl.BlockSpec((1, s, wide), lambda bi, j: (bi, 0, 0)),
                  pl.BlockSpec((1, lc, wide), lambda bi, j: (bi, 0, 0)),
                  pl.BlockSpec((1, lc, wide), lambda bi, j: (bi, 0, 0))]
                 + [bias_spec(t) for t in range(per)],
        out_specs=pl.BlockSpec((1, per * GRID_W, wide), lambda bi, j: (bi, j, 0)),
        out_shape=jax.ShapeDtypeStruct((b, s, wide), BF16),
        compiler_params=_cparams("parallel", "arbitrary"),
        name="na_latent",
    )(q, k, v, kc, vc, *([bias_tab] * per))


def _odd_proj_kernel(*refs, rope):
    if rope:
        (x_ref, g_ref, mod_ref, w_ref, bg_ref, td_ref,
         mq_ref, mk_ref, mv_ref, mo_ref, gt_ref, dq_ref, dk_ref, dv_ref) = refs
    else:
        (x_ref, g_ref, mod_ref, w_ref, bg_ref,
         mq_ref, mk_ref, mv_ref, mo_ref, gt_ref, dq_ref, dk_ref, dv_ref) = refs
    x = x_ref[...]
    h = _rms(x, g_ref[...]) * (1.0 + mod_ref[0, 1:2, :]) + mod_ref[0, 0:1, :]
    hb = h.astype(BF16)

    def proj(a, b):
        return jnp.dot(hb, w_ref[:, a:b], preferred_element_type=F32)

    mq_ref[...] = proj(0, 512).astype(mq_ref.dtype)
    mk_ref[...] = (proj(512, 1024) * (ML_DQK ** -0.5)).astype(mk_ref.dtype)
    mv_ref[...] = proj(1024, 1536).astype(mv_ref.dtype)
    mo_ref[...] = proj(1536, 2048)
    for blk in range(8):
        qd = proj(2048 + blk * LANE, 2048 + (blk + 1) * LANE)
        kd = proj(3072 + blk * LANE, 3072 + (blk + 1) * LANE)
        if rope:
            qd = _rope_block(qd, td_ref[0], td_ref[1], td_ref[2], DF_QK // 2)
            kd = _rope_block(kd, td_ref[0], td_ref[1], td_ref[2], DF_QK // 2)
        dq_ref[:, blk * LANE:(blk + 1) * LANE] = (qd * _exp2_scale(DF_QK ** -0.5)).astype(dq_ref.dtype)
        dk_ref[:, blk * LANE:(blk + 1) * LANE] = kd.astype(dk_ref.dtype)
    dv_ref[...] = proj(4096, 4608).astype(dv_ref.dtype)
    gt = proj(4608, 4736) + bg_ref[...]
    lane = lax.broadcasted_iota(jnp.int32, gt.shape, 1)
    log_f = jnp.minimum(gt, 0.0) - jnp.log(1.0 + jnp.exp(-jnp.abs(gt)))
    gt_ref[...] = jnp.where((lane % 8) >= 4, log_f, gt)


def _odd_proj(x, g, mod, w, bg, group_of_tile, rope_tab, kv_dtype):
    n, d = x.shape
    tm = ROW_TILE
    rope = rope_tab is not None
    in_specs = [
        pl.BlockSpec((tm, d), lambda i: (i, 0)),
        pl.BlockSpec((1, d), lambda i: (0, 0)),
        pl.BlockSpec((1, 6, d), lambda i: (group_of_tile(i), 0, 0)),
        pl.BlockSpec(w.shape, lambda i: (0, 0)),
        pl.BlockSpec(bg.shape, lambda i: (0, 0)),
    ]
    args = [x, g, mod, w, bg]
    if rope:
        nblk = rope_tab.shape[1] // tm
        in_specs.append(pl.BlockSpec((3, tm, LANE), lambda i: (0, i % nblk, 0)))
        args.append(rope_tab)
    out_shape = [
        jax.ShapeDtypeStruct((n, 512), BF16),
        jax.ShapeDtypeStruct((n, 512), BF16),
        jax.ShapeDtypeStruct((n, 512), BF16),
        jax.ShapeDtypeStruct((n, 512), F32),
        jax.ShapeDtypeStruct((n, LANE), F32),
        jax.ShapeDtypeStruct((n, 1024), BF16),
        jax.ShapeDtypeStruct((n, 1024), kv_dtype),
        jax.ShapeDtypeStruct((n, 512), kv_dtype),
    ]
    out_specs = [pl.BlockSpec((tm, s.shape[1]), lambda i: (i, 0)) for s in out_shape]
    return pl.pallas_call(
        functools.partial(_odd_proj_kernel, rope=rope),
        grid=(n // tm,),
        in_specs=in_specs,
        out_specs=out_specs,
        out_shape=out_shape,
        compiler_params=_cparams("parallel"),
        name="odd_proj_rope" if rope else "odd_proj",
    )(*args)


def _mlstm_kernel(q_ref, k_ref, v_ref, g_ref, c0_ref, n0_ref, m0_ref,
                  h_ref, cf_ref, nf_ref, mf_ref, c_sc, n_sc, m_sc, *, bb, direction, reverse):
    step = pl.program_id(1)
    L = ML_CHUNK

    @pl.when(step == 0)
    def _():
        c_sc[...] = c0_ref[...]
        n_sc[...] = n0_ref[...]
        m_sc[...] = m0_ref[...]

    t_idx = lax.broadcasted_iota(jnp.int32, (L, L), 0)
    s_idx = lax.broadcasted_iota(jnp.int32, (L, L), 1)
    keep = (s_idx >= t_idx) if reverse else (s_idx <= t_idx)
    tri = keep.astype(F32)
    last = 0 if reverse else L - 1
    for bi in range(bb):
        g = g_ref[bi]
        bcum = jnp.dot(tri, g, precision=HIGHEST, preferred_element_type=F32)
        g_t = g.T
        b_t = bcum.T
        for hd in range(ML_HEADS):
            sl = slice(hd * LANE, (hd + 1) * LANE)
            ci = direction * 8 + hd
            cf = direction * 8 + 4 + hd
            li_row = g_t[ci:ci + 1, :]
            b_row = b_t[cf:cf + 1, :]
            li_col = g[:, ci:ci + 1]
            b_col = bcum[:, cf:cf + 1]
            m_old = m_sc[bi, hd][:, 0:1]
            n_old = n_sc[bi, hd]
            c_old = c_sc[bi, hd]
            qh = q_ref[bi, :, sl]
            kh = k_ref[bi, :, sl]
            vh = v_ref[bi, :, sl]
            dm = jnp.where(keep, b_col - b_row + li_row, NEG_BIG)
            m_inter = b_col + m_old
            m_t = jnp.maximum(m_inter, jnp.max(dm, axis=1, keepdims=True))
            qk = lax.dot_general(qh, kh, (((1,), (1,)), ((), ())), preferred_element_type=F32)
            w_intra = jnp.where(keep, jnp.exp(dm - m_t), 0.0) * qk
            w_inter = jnp.exp(m_inter - m_t)
            qc = lax.dot_general(qh, c_old.astype(BF16), (((1,), (1,)), ((), ())),
                                 preferred_element_type=F32)
            num = jnp.dot(w_intra.astype(BF16), vh, preferred_element_type=F32) + w_inter * qc
            qn = jnp.sum(qh.astype(F32) * n_old, axis=1, keepdims=True)
            den = jnp.sum(w_intra, axis=1, keepdims=True) + w_inter * qn
            h_ref[bi, :, sl] = num / jnp.maximum(jnp.abs(den), jnp.exp(-m_t))
            m_new = m_t[last:last + 1, :]
            b_last = b_col[last:last + 1, :]
            w_end = jnp.exp(b_last - b_col + li_col - m_new)
            decay = jnp.exp(b_last + m_old - m_new)
            wv = (w_end * vh.astype(F32)).astype(BF16)
            upd = lax.dot_general(wv, kh, (((0,), (0,)), ((), ())), preferred_element_type=F32)
            c_sc[bi, hd] = decay * c_old + upd
            n_sc[bi, hd] = decay * n_old + jnp.sum(w_end * kh.astype(F32), axis=0, keepdims=True)
            m_sc[bi, hd] = jnp.broadcast_to(m_new, (1, LANE))

    @pl.when(step == pl.num_programs(1) - 1)
    def _():
        cf_ref[...] = c_sc[...]
        nf_ref[...] = n_sc[...]
        mf_ref[...] = m_sc[...]


def _mlstm_dir(q, k, v, gates, c0, n0, m0, *, direction, bb):
    b, t, w = q.shape
    nc = t // ML_CHUNK
    reverse = direction == 1
    if reverse:
        cidx = lambda bi, c: (bi, nc - 1 - c, 0)
    else:
        cidx = lambda bi, c: (bi, c, 0)
    st4 = lambda bi, c: (bi, 0, 0, 0)
    seq = pl.BlockSpec((bb, ML_CHUNK, w), cidx)
    c_spec = pl.BlockSpec((bb, ML_HEADS, LANE, LANE), st4)
    n_spec = pl.BlockSpec((bb, ML_HEADS, 1, LANE), st4)
    return pl.pallas_call(
        functools.partial(_mlstm_kernel, bb=bb, direction=direction, reverse=reverse),
        grid=(b // bb, nc),
        in_specs=[seq, seq, seq, pl.BlockSpec((bb, ML_CHUNK, LANE), cidx), c_spec, n_spec, n_spec],
        out_specs=[seq, c_spec, n_spec, n_spec],
        out_shape=[jax.ShapeDtypeStruct((b, t, w), F32),
                   jax.ShapeDtypeStruct(c0.shape, F32),
                   jax.ShapeDtypeStruct(n0.shape, F32),
                   jax.ShapeDtypeStruct(m0.shape, F32)],
        scratch_shapes=[pltpu.VMEM((bb, ML_HEADS, LANE, LANE), F32),
                        pltpu.VMEM((bb, ML_HEADS, 1, LANE), F32),
                        pltpu.VMEM((bb, ML_HEADS, 1, LANE), F32)],
        compiler_params=_cparams("parallel", "arbitrary"),
        name="mlstm_bwd" if reverse else "mlstm_fwd",
    )(q, k, v, gates, c0, n0, m0)


ROUTE_IDX, ROUTE_ROW, ROUTE_GATE = 0, 4, 8
TAB_LEN, TAB_OFF, TAB_BASE = 0, 1, 2
SUBLANES = 8


def _post_tail(x_ref, mix, mod_ref, g2_ref, wr_ref, br_ref, cin_ref,
               x_out_ref, h_ref, rt_ref, tab_ref, cout_ref, cnt_sc):
    @pl.when(pl.program_id(0) == 0)
    def _():
        cnt_sc[...] = cin_ref[...]

    x1 = x_ref[...] + mod_ref[0, 2:3, :] * mix
    x_out_ref[...] = x1
    h2 = _rms(x1, g2_ref[...]) * (1.0 + mod_ref[0, 4:5, :]) + mod_ref[0, 3:4, :]
    h_ref[...] = h2
    tm = h2.shape[0]
    logits_t = lax.dot_general(wr_ref[...], h2, (((0,), (1,)), ((), ())), precision=HIGHEST,
                               preferred_element_type=F32) + br_ref[...].T
    e_id = lax.broadcasted_iota(jnp.int32, (LANE, tm), 0)
    lg = jnp.where(e_id < N_EXPERTS, logits_t, NEG_BIG)
    vals, idxs, hots = [], [], []
    for _ in range(TOP_K):
        mk = jnp.max(lg, axis=0, keepdims=True)
        idx = jnp.min(jnp.where(lg == mk, e_id, LANE), axis=0, keepdims=True)
        hot = e_id == idx
        vals.append(mk)
        idxs.append(idx)
        hots.append(hot)
        lg = jnp.where(hot, NEG_BIG, lg)
    ex = [jnp.exp(v - vals[0]) for v in vals]
    tot = ex[0] + ex[1] + ex[2] + ex[3]
    sel = jnp.zeros((LANE, tm), F32)
    for hot in hots:
        sel = sel + hot.astype(F32)
    t_src = lax.broadcasted_iota(jnp.int32, (ROW_TILE, ROW_TILE), 0)
    t_dst = lax.broadcasted_iota(jnp.int32, (ROW_TILE, ROW_TILE), 1)
    earlier = (t_src < t_dst).astype(BF16)
    e_src = lax.broadcasted_iota(jnp.int32, (LANE, LANE), 0)
    e_dst = lax.broadcasted_iota(jnp.int32, (LANE, LANE), 1)
    lower = (e_dst < e_src).astype(F32)
    sub = lax.broadcasted_iota(jnp.int32, (SUBLANES, LANE), 0)
    base = cnt_sc[...]
    local_parts = []
    for part in range(tm // ROW_TILE):
        sel_p = sel[:, part * ROW_TILE:(part + 1) * ROW_TILE]
        prefix = jnp.dot(sel_p.astype(BF16), earlier, preferred_element_type=F32)
        len_col = jnp.floor((jnp.sum(sel_p, axis=1, keepdims=True) + (SUBLANES - 1)) / SUBLANES) * SUBLANES
        off_col = jnp.dot(lower, jnp.broadcast_to(len_col, (LANE, LANE)), precision=HIGHEST,
                          preferred_element_type=F32)[:, 0:1]
        local_parts.append(prefix + off_col)
        rows_t = jnp.concatenate([jnp.broadcast_to(len_col, (LANE, LANE)),
                                  jnp.broadcast_to(off_col, (LANE, LANE))], axis=1).T
        run_len = rows_t[0:1, :]
        run_off = rows_t[LANE:LANE + 1, :]
        tab = jnp.where(sub == TAB_LEN, run_len, 0.0)
        tab = jnp.where(sub == TAB_OFF, run_off, tab)
        tab = jnp.where(sub == TAB_BASE, base, tab)
        tab_ref[part] = tab
        base = base + run_len
    cnt_sc[...] = base
    cout_ref[...] = base
    local = jnp.concatenate(local_parts, axis=1)
    route_t = jnp.zeros((LANE, tm), F32)
    for k in range(TOP_K):
        row = jnp.sum(jnp.where(hots[k], local, 0.0), axis=0, keepdims=True)
        route_t = jnp.where(e_id == ROUTE_IDX + k, idxs[k].astype(F32), route_t)
        route_t = jnp.where(e_id == ROUTE_ROW + k, row, route_t)
        route_t = jnp.where(e_id == ROUTE_GATE + k, ex[k] / tot, route_t)
    rt_ref[...] = route_t.T


def _post_even_kernel(x_ref, a_ref, b_ref, wa_ref, wb_ref, *rest):
    mix = (jnp.dot(a_ref[...], wa_ref[...], preferred_element_type=F32)
           + jnp.dot(b_ref[...], wb_ref[...], preferred_element_type=F32))
    _post_tail(x_ref, mix, *rest)


def _post_odd_kernel(x_ref, hf_ref, hb_ref, mo_ref, od_ref, mln_ref, sub_ref, lam_ref,
                     wa_ref, wb_ref, *rest, lam_init):
    lp = lam_ref[...]
    lam = (jnp.exp(jnp.sum(lp[0:1] * lp[1:2], axis=1, keepdims=True))
           - jnp.exp(jnp.sum(lp[2:3] * lp[3:4], axis=1, keepdims=True)) + lam_init)
    mix = None
    for hd in range(ML_HEADS):
        sl = slice(hd * LANE, (hd + 1) * LANE)
        hm = hf_ref[:, sl] + hb_ref[:, sl]
        om = _rms(hm, mln_ref[:, sl]) * _sigmoid(mo_ref[:, sl])
        part = jnp.dot(om.astype(BF16), wa_ref[sl, :], preferred_element_type=F32)
        mix = part if mix is None else mix + part
    for hd in range(DF_HEADS):
        o1 = od_ref[:, (2 * hd) * LANE:(2 * hd + 1) * LANE]
        o2 = od_ref[:, (2 * hd + 1) * LANE:(2 * hd + 2) * LANE]
        od = _rms(o1 - lam * o2, sub_ref[...]) * (1.0 - lam_init)
        mix = mix + jnp.dot(od.astype(BF16), wb_ref[hd * LANE:(hd + 1) * LANE, :],
                            preferred_element_type=F32)
    _post_tail(x_ref, mix, *rest)


def _post_mix(kernel, x, row_inputs, const_inputs, mod, g2, wr, br, cnt_in, group_of_tile, name):
    n, d = x.shape
    parts = POST_TILES
    tm = parts * ROW_TILE
    row = lambda a: pl.BlockSpec((tm, a.shape[1]), lambda i: (i, 0))
    const = lambda a: pl.BlockSpec(a.shape, lambda i: (0,) * a.ndim)
    in_specs = ([row(x)] + [row(a) for a in row_inputs] + [const(a) for a in const_inputs]
                + [pl.BlockSpec((1, 6, d), lambda i: (group_of_tile(i * parts), 0, 0)),
                   const(g2), const(wr), const(br), const(cnt_in)])
    return pl.pallas_call(
        kernel,
        grid=(n // tm,),
        in_specs=in_specs,
        out_specs=[pl.BlockSpec((tm, d), lambda i: (i, 0)),
                   pl.BlockSpec((tm, d), lambda i: (i, 0)),
                   pl.BlockSpec((tm, LANE), lambda i: (i, 0)),
                   pl.BlockSpec((parts, SUBLANES, LANE), lambda i: (i, 0, 0)),
                   pl.BlockSpec((1, LANE), lambda i: (0, 0))],
        out_shape=[jax.ShapeDtypeStruct((n, d), F32), jax.ShapeDtypeStruct((n, d), F32),
                   jax.ShapeDtypeStruct((n, LANE), F32),
                   jax.ShapeDtypeStruct((n // ROW_TILE, SUBLANES, LANE), F32),
                   jax.ShapeDtypeStruct((1, LANE), F32)],
        scratch_shapes=[pltpu.VMEM((1, LANE), F32)],
        compiler_params=_cparams("arbitrary"),
        name=name,
    )(x, *row_inputs, *const_inputs, mod, g2, wr, br, cnt_in)


LOCAL_ROWS = ROW_TILE * TOP_K + N_EXPERTS * SUBLANES
TAB_WORDS = 128


def _split_bf16(x):
    hi = x.astype(BF16)
    return hi, (x - hi.astype(F32)).astype(BF16)


BIG_CHUNK = 4 * SUBLANES


def _run_copies(tab_smem, slot, make_copy, start):
    def per_expert(e, carry):
        n_big = tab_smem[slot, e]
        n_small = tab_smem[slot, N_EXPERTS + e]
        off = tab_smem[slot, 2 * N_EXPERTS + e]
        base = tab_smem[slot, 3 * N_EXPERTS + e]

        def chunk(rows, shift):
            def one(j, c):
                cp = make_copy(pl.multiple_of(off + shift + j * rows, SUBLANES),
                               pl.multiple_of(base + shift + j * rows, SUBLANES), rows)
                if start:
                    cp.start()
                else:
                    cp.wait()
                return c
            return one

        carry = lax.fori_loop(0, n_big, chunk(BIG_CHUNK, 0), carry)
        return lax.fori_loop(0, n_small, chunk(SUBLANES, n_big * BIG_CHUNK), carry)

    lax.fori_loop(0, N_EXPERTS, per_expert, 0)


def _dispatch_kernel(tail_ref, pend_ref, nu_ref, tab_hbm, hp_ref, hs_ref, rp_ref, rs_ref, xs_ref,
                     tab_smem, local_buf, zero_buf, sem_tab, sem_rows, sem_zero, *, tiles_p, n_blocks):
    i = pl.program_id(0)
    nt = pl.num_programs(0)
    slot = i % 2
    d = hp_ref.shape[1]

    def tab_copy(step, sl):
        return pltpu.make_async_copy(tab_hbm.at[step], tab_smem.at[sl], sem_tab.at[sl])

    def zero_block(start):
        return pltpu.make_async_copy(zero_buf, xs_ref.at[pl.ds(start, MOE_ROWS)], sem_zero)

    def zero_chunk(start):
        return pltpu.make_async_copy(zero_buf.at[pl.ds(0, SUBLANES)], xs_ref.at[pl.ds(start, SUBLANES)],
                                     sem_rows)

    @pl.when(i == 0)
    def _():
        tab_copy(0, 0).start()
        zero_buf[...] = jnp.zeros(zero_buf.shape, F32)

        def tails(start):
            def per_expert(e, carry):
                def one(j, c):
                    cp = zero_chunk(pl.multiple_of(tail_ref[e] + j * SUBLANES, SUBLANES))
                    if start:
                        cp.start()
                    else:
                        cp.wait()
                    return c
                return lax.fori_loop(0, (pend_ref[e] - tail_ref[e]) // SUBLANES, one, carry)
            lax.fori_loop(0, N_EXPERTS, per_expert, 0)

        def blocks(start):
            def one(b, c):
                cp = zero_block(pl.multiple_of(b * MOE_ROWS, MOE_ROWS))
                if start:
                    cp.start()
                else:
                    cp.wait()
                return c
            lax.fori_loop(nu_ref[0], n_blocks, one, 0)

        tails(True)
        blocks(True)
        tails(False)
        blocks(False)

    def sort_tile(h_ref, rt_ref):
        local = local_buf.at[slot]
        rt_t = rt_ref[...].T
        row_id = lax.broadcasted_iota(jnp.int32, (LOCAL_ROWS, ROW_TILE), 0).astype(F32)
        pick = jnp.zeros((LOCAL_ROWS, ROW_TILE), F32)
        gate = jnp.zeros((LOCAL_ROWS, ROW_TILE), F32)
        for k in range(TOP_K):
            hit = row_id == rt_t[ROUTE_ROW + k:ROUTE_ROW + k + 1, :]
            pick = jnp.where(hit, 1.0, pick)
            gate = jnp.where(hit, rt_t[ROUTE_GATE + k:ROUTE_GATE + k + 1, :], gate)
        local[:, 0:d] = jnp.dot(pick.astype(BF16), h_ref[...].astype(BF16), preferred_element_type=F32)
        g_hi, g_lo = _split_bf16(gate)
        ones = jnp.ones((ROW_TILE, LANE), BF16)
        local[:, d:d + LANE] = (jnp.dot(g_hi, ones, preferred_element_type=F32)
                                + jnp.dot(g_lo, ones, preferred_element_type=F32))

    @pl.when(i < tiles_p)
    def _():
        sort_tile(hp_ref, rp_ref)

    @pl.when(i >= tiles_p)
    def _():
        sort_tile(hs_ref, rs_ref)

    def run_copy(buf):
        def make(off, base, rows):
            return pltpu.make_async_copy(local_buf.at[buf, pl.ds(off, rows)], xs_ref.at[pl.ds(base, rows)],
                                         sem_rows)
        return make

    @pl.when(i > 0)
    def _():
        _run_copies(tab_smem, 1 - slot, run_copy(1 - slot), False)

    @pl.when(i + 1 < nt)
    def _():
        tab_copy(i + 1, 1 - slot).start()

    tab_copy(i, slot).wait()
    _run_copies(tab_smem, slot, run_copy(slot), True)

    @pl.when(i == nt - 1)
    def _():
        _run_copies(tab_smem, slot, run_copy(slot), False)


def _dispatch(tail_start, pad_end, n_used, tab, h2p, h2s, rtp, rts, n_blocks):
    d = h2p.shape[1]
    tm = ROW_TILE
    tiles_p = h2p.shape[0] // tm
    nt = tiles_p + h2s.shape[0] // tm
    idx_p = lambda i, *_: (jnp.minimum(i, tiles_p - 1), 0)
    idx_s = lambda i, *_: (jnp.maximum(i - tiles_p, 0), 0)
    grid_spec = pltpu.PrefetchScalarGridSpec(
        num_scalar_prefetch=3,
        grid=(nt,),
        in_specs=[pl.BlockSpec(memory_space=pl.ANY),
                  pl.BlockSpec((tm, d), idx_p), pl.BlockSpec((tm, d), idx_s),
                  pl.BlockSpec((tm, LANE), idx_p), pl.BlockSpec((tm, LANE), idx_s)],
        out_specs=pl.BlockSpec(memory_space=pl.ANY),
        scratch_shapes=[pltpu.SMEM((2, TAB_WORDS), jnp.int32),
                        pltpu.VMEM((2, LOCAL_ROWS, d + LANE), F32),
                        pltpu.VMEM((MOE_ROWS, d + LANE), F32),
                        pltpu.SemaphoreType.DMA((2,)),
                        pltpu.SemaphoreType.DMA(()),
                        pltpu.SemaphoreType.DMA(())],
    )
    return pl.pallas_call(
        functools.partial(_dispatch_kernel, tiles_p=tiles_p, n_blocks=n_blocks),
        grid_spec=grid_spec,
        out_shape=jax.ShapeDtypeStruct((n_blocks * MOE_ROWS, d + LANE), F32),
        compiler_params=_cparams("arbitrary"),
        name="moe_dispatch",
    )(tail_start, pad_end, n_used, tab, h2p, h2s, rtp, rts)


def _expert_kernel(be_ref, nu_ref, x_ref, w1_ref, b1_ref, w2_ref, b2_ref, o_ref, w1b, w2b):
    i = pl.program_id(0)
    d = o_ref.shape[1]
    new_expert = jnp.logical_or(i == 0, be_ref[i] != be_ref[jnp.maximum(i - 1, 0)])

    @pl.when(jnp.logical_and(new_expert, i < nu_ref[0]))
    def _():
        w1b[...] = w1_ref[0].astype(BF16)
        w2b[...] = w2_ref[0].astype(BF16)

    @pl.when(i < nu_ref[0])
    def _():
        x = x_ref[:, 0:d].astype(BF16)
        gate = x_ref[:, d:d + 1]
        acc = None
        fc = FF_CHUNK
        for c in range(D_FF // fc):
            glu = jnp.dot(x, w1b[:, c * fc:(c + 1) * fc], preferred_element_type=F32)
            glu = glu + b1_ref[0, :, c * fc:(c + 1) * fc]
            lin = jnp.dot(x, w1b[:, D_FF + c * fc:D_FF + (c + 1) * fc],
                          preferred_element_type=F32)
            lin = lin + b1_ref[0, :, D_FF + c * fc:D_FF + (c + 1) * fc]
            glu = jnp.minimum(glu, SWIGLU_LIMIT)
            lin = jnp.clip(lin, -SWIGLU_LIMIT, SWIGLU_LIMIT)
            act = glu * _sigmoid(SWIGLU_ALPHA * glu) * (lin + 1.0)
            part = jnp.dot(act.astype(BF16), w2b[c * fc:(c + 1) * fc, :],
                           preferred_element_type=F32)
            acc = part if acc is None else acc + part
        o_ref[...] = (acc + b2_ref[0]) * gate

    @pl.when(i >= nu_ref[0])
    def _():
        o_ref[...] = jnp.zeros(o_ref.shape, F32)


def _expert_ffn(blk_e, n_used, xs, layer, w1, b1, w2, b2):
    d = w2.shape[3]
    nb = blk_e.shape[0]
    wmap = lambda i, be, nu: (layer, be[i], 0, 0)
    sq = pl.Squeezed()
    grid_spec = pltpu.PrefetchScalarGridSpec(
        num_scalar_prefetch=2,
        grid=(nb,),
        in_specs=[pl.BlockSpec((MOE_ROWS, xs.shape[1]), lambda i, be, nu: (jnp.minimum(i, nu[0] - 1), 0)),
                  pl.BlockSpec((sq, 1, d, 2 * D_FF), wmap),
                  pl.BlockSpec((sq, 1, 1, 2 * D_FF), wmap),
                  pl.BlockSpec((sq, 1, D_FF, d), wmap),
                  pl.BlockSpec((sq, 1, 1, d), wmap)],
        out_specs=pl.BlockSpec((MOE_ROWS, d), lambda i, be, nu: (i, 0)),
        scratch_shapes=[pltpu.VMEM((d, 2 * D_FF), BF16), pltpu.VMEM((D_FF, d), BF16)],
    )
    return pl.pallas_call(
        _expert_kernel,
        grid_spec=grid_spec,
        out_shape=jax.ShapeDtypeStruct((nb * MOE_ROWS, d), F32),
        compiler_params=_cparams("arbitrary"),
        name="expert_ffn",
    )(blk_e, n_used, xs, w1, b1, w2, b2)


def _combine_kernel(*refs, final):
    if final:
        (tab_hbm, x_ref, rt_ref, mod_ref, fg_ref, yb_ref, x_out_ref, yn_ref,
         tab_smem, local_buf, sem_tab, sem_rows) = refs
    else:
        (tab_hbm, x_ref, rt_ref, mod_ref, yb_ref, x_out_ref,
         tab_smem, local_buf, sem_tab, sem_rows) = refs
    i = pl.program_id(0)
    nt = pl.num_programs(0)
    slot = i % 2

    def tab_copy(step):
        return pltpu.make_async_copy(tab_hbm.at[step], tab_smem.at[step % 3], sem_tab.at[step % 3])

    def run_copy(buf):
        def make(off, base, rows):
            return pltpu.make_async_copy(yb_ref.at[pl.ds(base, rows)], local_buf.at[buf, pl.ds(off, rows)],
                                         sem_rows.at[buf])
        return make

    @pl.when(i == 0)
    def _():
        local_buf[...] = jnp.zeros(local_buf.shape, F32)
        tab_copy(0).start()
        tab_copy(0).wait()
        _run_copies(tab_smem, 0, run_copy(0), True)

        @pl.when(nt > 1)
        def _():
            tab_copy(1).start()

    @pl.when(i + 1 < nt)
    def _():
        tab_copy(i + 1).wait()
        _run_copies(tab_smem, (i + 1) % 3, run_copy(1 - slot), True)

    @pl.when(i + 2 < nt)
    def _():
        tab_copy(i + 2).start()

    _run_copies(tab_smem, i % 3, run_copy(slot), False)
    local = local_buf.at[slot]

    rt = rt_ref[...]
    col_id = lax.broadcasted_iota(jnp.int32, (ROW_TILE, LOCAL_ROWS), 1).astype(F32)
    pick = jnp.zeros((ROW_TILE, LOCAL_ROWS), F32)
    for k in range(TOP_K):
        pick = jnp.where(col_id == rt[:, ROUTE_ROW + k:ROUTE_ROW + k + 1], 1.0, pick)
    pick = pick.astype(BF16)
    y_hi, y_lo = _split_bf16(local[...])
    moe = (jnp.dot(pick, y_hi, preferred_element_type=F32)
           + jnp.dot(pick, y_lo, preferred_element_type=F32))
    x2 = x_ref[...] + mod_ref[0, 5:6, :] * moe
    x_out_ref[...] = x2
    if final:
        yn_ref[...] = _rms(x2, fg_ref[...])


def _combine(x, route, tab, yb, mod, group_of_tile, final_g):
    n, d = x.shape
    tm = ROW_TILE
    nt = n // tm
    final = final_g is not None
    in_specs = [pl.BlockSpec(memory_space=pl.ANY),
                pl.BlockSpec((tm, d), lambda i: (i, 0)),
                pl.BlockSpec((tm, LANE), lambda i: (i, 0)),
                pl.BlockSpec((1, 6, d), lambda i: (group_of_tile(i), 0, 0))]
    args = [tab, x, route, mod]
    out_shape = [jax.ShapeDtypeStruct((n, d), F32)]
    if final:
        in_specs.append(pl.BlockSpec((1, d), lambda i: (0, 0)))
        args.append(final_g)
        out_shape.append(jax.ShapeDtypeStruct((n, d), F32))
    in_specs.append(pl.BlockSpec(memory_space=pl.ANY))
    args.append(yb)
    out_specs = [pl.BlockSpec((tm, d), lambda i: (i, 0)) for _ in out_shape]
    return pl.pallas_call(
        functools.partial(_combine_kernel, final=final),
        grid=(nt,),
        in_specs=in_specs,
        out_specs=out_specs,
        out_shape=out_shape,
        scratch_shapes=[pltpu.SMEM((3, TAB_WORDS), jnp.int32),
                        pltpu.VMEM((2, LOCAL_ROWS, d), F32),
                        pltpu.SemaphoreType.DMA((3,)),
                        pltpu.SemaphoreType.DMA((2,))],
        compiler_params=_cparams("arbitrary"),
        name="moe_combine_final" if final else "moe_combine",
    )(*args)


def _block_plan(counts_f, n_tiles):
    counts = counts_f[0, :N_EXPERTS].astype(jnp.int32)
    padded = (counts + MOE_ROWS - 1) // MOE_ROWS * MOE_ROWS
    pad_end = jnp.cumsum(padded)
    pad_start = pad_end - padded
    max_rows = n_tiles * LOCAL_ROWS
    nb = -(-max_rows // MOE_ROWS) + N_EXPERTS
    first_row = jnp.arange(nb, dtype=jnp.int32) * MOE_ROWS
    blk_e = jnp.minimum(jnp.sum((pad_end[None, :] <= first_row[:, None]).astype(jnp.int32), axis=1),
                        N_EXPERTS - 1)
    n_used = (pad_end[-1] // MOE_ROWS).reshape(1)
    return pad_start, pad_start + counts, pad_end, blk_e, n_used, nb


def _run_table(tab_f, pad_start):
    t = tab_f[:, :, :N_EXPERTS].astype(jnp.int32)
    n_big = t[:, TAB_LEN] // BIG_CHUNK
    n_small = (t[:, TAB_LEN] - n_big * BIG_CHUNK) // SUBLANES
    base = t[:, TAB_BASE] + pad_start[None, :]
    return jnp.concatenate([n_big, n_small, t[:, TAB_OFF], base], axis=1)


def _to_half_split(x):
    return jnp.concatenate([x[..., 0::2], x[..., 1::2]], axis=-1)


def _from_half_split(x):
    n = x.shape[-1] // 2
    return jnp.stack([x[..., :n], x[..., n:]], axis=-1).reshape(x.shape)


def _pad_heads(w, heads, hd):
    k = w.shape[0]
    w = w.reshape(k, heads, hd)
    return jnp.pad(w, ((0, 0), (0, 0), (0, LANE - hd))).reshape(k, heads * LANE)


def _pad_rows(w, heads, hd):
    n = w.shape[1]
    w = w.reshape(heads, hd, n)
    return jnp.pad(w, ((0, 0), (0, LANE - hd), (0, 0))).reshape(heads * LANE, n)


def _rope_tables(t, rot_dim, lane_off):
    tok = jnp.arange(t)
    row = (tok // GRID_W).astype(F32)
    col = (tok % GRID_W).astype(F32)
    nf = rot_dim // 4
    inv = ROPE_THETA ** (-jnp.arange(nf, dtype=F32) / nf)
    ang = jnp.concatenate([row[:, None] * inv, col[:, None] * inv], axis=-1)
    cos, sin = jnp.cos(ang), jnp.sin(ang)
    half = rot_dim // 2
    fill = lambda v, w: jnp.full((t, w), v, F32)
    rest = LANE - lane_off - rot_dim
    cos_t = jnp.concatenate([fill(1.0, lane_off), cos, cos, fill(1.0, rest)], axis=1)
    sin_a = jnp.concatenate([fill(0.0, lane_off), -sin, fill(0.0, half + rest)], axis=1)
    sin_b = jnp.concatenate([fill(0.0, lane_off + half), sin, fill(0.0, rest)], axis=1)
    return jnp.stack([cos_t, sin_a, sin_b])


def _na_bias_table(rpb):
    c = np.arange(GRID_W)[:, None]
    kc = np.arange(GRID_W)[None, :]
    wstart = np.clip(c - NA_KW // 2, 0, GRID_W - NA_KW)
    mask = (kc >= wstart) & (kc < wstart + NA_KW)
    n_dc = 2 * NA_KW - 1
    dcol = np.clip(kc - c + NA_KW - 1, 0, n_dc - 1)
    pick = (dcol[None] == np.arange(n_dc)[:, None, None]).astype(np.float32)
    toep = jnp.einsum('hrd,dck->hrck', rpb.astype(F32), pick, precision=HIGHEST)
    toep = jnp.where(mask[None, None], toep, NEG_BIG)
    tab = jnp.stack([toep[:, d0:d0 + NA_KR] for d0 in range(NA_KR)], axis=1)
    tab = jnp.transpose(tab, (0, 1, 3, 2, 4))
    return tab.reshape(NA_HEADS, NA_KR, GRID_W, NA_KR * GRID_W)


def kernel(x_prompt, x_sample, c, cache_mla_ckv, cache_mla_krope, cache_na_k, cache_na_v, state_mlstm_C, state_mlstm_n, state_mlstm_m, cache_diff_k, cache_diff_v, c_ctx, w_mod, b_mod, norm_g, final_g, a_w_in, a_q_norm, a_w_uq, a_kv_norm, a_w_ukv, a_rpb, a_w_out, b_w_in, b_gate_bias, b_ml_norm, b_df_lambda, b_df_subln, b_w_out, moe_w_router, moe_b_router, moe_w1, moe_b1, moe_w2, moe_b2):
    bp, lp, d = x_prompt.shape
    bs, ts, _ = x_sample.shape
    n_p, n_s = bp * lp, bs * ts
    past = cache_mla_ckv.shape[2]
    tiles_per_batch = ts // ROW_TILE
    grp_p = lambda i: 0
    grp_s = lambda i: 1 + i // tiles_per_batch

    xp = x_prompt.reshape(n_p, d)
    xs = x_sample.reshape(n_s, d)
    cond8 = jnp.concatenate([c_ctx[None, :], c, jnp.zeros((8 - 1 - bs, d), F32)], axis=0)
    mod_all = _modulation(cond8, w_mod, b_mod)


    def moe_layer(l, x1p, x1s, h2p, h2s, rtp, rts, tabp, tabs_, counts, final):
        n_tiles = (n_p + n_s) // ROW_TILE
        pad_start, tail_start, pad_end, blk_e, n_used, nb = _block_plan(counts, n_tiles)
        tab_p = _run_table(tabp, pad_start)
        tab_s = _run_table(tabs_, pad_start)
        xs_rows = _dispatch(tail_start, pad_end, n_used, jnp.concatenate([tab_p, tab_s], axis=0),
                            h2p, h2s, rtp, rts, nb)
        yb = _expert_ffn(blk_e, n_used, xs_rows, l, moe_w1, moe_b1[:, :, None, :],
                         moe_w2, moe_b2[:, :, None, :])
        fg = final_g.reshape(1, d) if final else None
        outp = _combine(x1p, rtp, tab_p, yb, mod_all[l], grp_p, fg)
        outs = _combine(x1s, rts, tab_s, yb, mod_all[l], grp_s, fg)
        return outp, outs

    def router_w(l):
        wr = jnp.pad(moe_w_router[l], ((0, 0), (0, LANE - N_EXPERTS)))
        br = jnp.pad(moe_b_router[l], (0, LANE - N_EXPERTS)).reshape(1, LANE)
        return wr, br

    l, e = 0, 0
    w_in = a_w_in[e]
    kr_cols = _to_half_split(w_in[:, 384:416])
    w0 = jnp.concatenate([
        w_in[:, :384], jnp.pad(kr_cols, ((0, 0), (0, LANE - MLA_ROPE))),
        _pad_heads(w_in[:, 416:928], NA_HEADS, NA_HD),
        _pad_heads(w_in[:, 928:1440], NA_HEADS, NA_HD),
        _pad_heads(w_in[:, 1440:1952], NA_HEADS, NA_HD)], axis=1).astype(BF16)
    wuq = a_w_uq[e].reshape(MLA_Q_LORA, MLA_HEADS, MLA_NOPE + MLA_ROPE)
    wuq = jnp.concatenate([wuq[:, :, :MLA_NOPE], _to_half_split(wuq[:, :, MLA_NOPE:])], axis=2)
    wuq = _pad_heads(wuq.reshape(MLA_Q_LORA, -1), MLA_HEADS, MLA_NOPE + MLA_ROPE).astype(BF16)
    wukv = a_w_ukv[e].reshape(MLA_KV_LORA, MLA_HEADS, MLA_NOPE + MLA_V)
    wk = _pad_heads(wukv[:, :, :MLA_NOPE].reshape(MLA_KV_LORA, -1), MLA_HEADS, MLA_NOPE).astype(BF16)
    wv = _pad_heads(wukv[:, :, MLA_NOPE:].reshape(MLA_KV_LORA, -1), MLA_HEADS, MLA_V).astype(BF16)
    sel_np = np.zeros((LANE, MLA_HEADS * LANE), np.float32)
    for hd in range(MLA_HEADS):
        sel_np[np.arange(MLA_ROPE), hd * LANE + MLA_NOPE + np.arange(MLA_ROPE)] = 1.0
    sel = jnp.asarray(sel_np, BF16)
    qn_g = a_q_norm[e].reshape(1, -1)
    kvn_g = a_kv_norm[e].reshape(1, -1)
    g1 = norm_g[l, 0].reshape(1, d)
    g2 = norm_g[l, 1].reshape(1, d)
    tabs_a = (_rope_tables(ts, MLA_ROPE, MLA_NOPE), _rope_tables(ts, MLA_ROPE, 0))
    wide = MLA_HEADS * LANE

    q_p, ckv_p, kr_p, nq_p, nk_p, nv_p = _even_proj(xp, g1, mod_all[l], w0, qn_g, wuq, kvn_g, grp_p, None, F32)
    new_mla_ckv = ckv_p.reshape(bp, 1, lp, MLA_KV_LORA)
    new_mla_krope = _from_half_split(kr_p[:, :MLA_ROPE]).reshape(bp, 1, lp, MLA_ROPE)
    new_na_k = nk_p.reshape(bp, 1, lp, NA_HEADS, LANE)[..., :NA_HD]
    new_na_v = nv_p.reshape(bp, 1, lp, NA_HEADS, LANE)[..., :NA_HD]
    k_p, v_p = _kv_expand(ckv_p, kr_p, wk, sel, wv)
    ident = lambda hd: hd
    o_mla_p = _flash(q_p.reshape(bp, lp, wide), k_p.reshape(bp, lp, wide), v_p.reshape(bp, lp, wide),
                     v_of_head=ident, out_dtype=BF16)
    o_na_p = _flash(nq_p.reshape(bp, lp, wide), nk_p.astype(BF16).reshape(bp, lp, wide),
                    nv_p.astype(BF16).reshape(bp, lp, wide),
                    v_of_head=ident, out_dtype=BF16)
    q_s, ckv_s, kr_s, nq_s, nk_s, nv_s = _even_proj(xs, g1, mod_all[l], w0, qn_g, wuq, kvn_g, grp_s, tabs_a, BF16)
    ckv_all = jnp.concatenate([cache_mla_ckv[:, e], ckv_s.reshape(bs, ts, LANE)], axis=1)
    kr_cache = jnp.pad(_to_half_split(cache_mla_krope[:, e]), ((0, 0), (0, 0), (0, LANE - MLA_ROPE)))
    kr_all = jnp.concatenate([kr_cache, kr_s.reshape(bs, ts, LANE)], axis=1)
    n_all = past + ts
    k_s, v_s = _kv_expand(ckv_all.reshape(bs * n_all, LANE), kr_all.reshape(bs * n_all, LANE), wk, sel, wv)
    o_mla_s = _flash(q_s.reshape(bs, ts, wide), k_s.reshape(bs, n_all, wide), v_s.reshape(bs, n_all, wide),
                     v_of_head=ident, out_dtype=BF16)
    pad_hd = ((0, 0), (0, 0), (0, 0), (0, LANE - NA_HD))
    kc = jnp.pad(cache_na_k[:, e], pad_hd).astype(BF16).reshape(bs, past, wide)
    vc = jnp.pad(cache_na_v[:, e], pad_hd).astype(BF16).reshape(bs, past, wide)
    o_na_s = _na_latent(nq_s.reshape(bs, ts, wide), nk_s.reshape(bs, ts, wide), nv_s.reshape(bs, ts, wide),
                        kc, vc, _na_bias_table(a_rpb[e]))
    w_out = a_w_out[e]
    wa = _pad_rows(w_out[:MLA_HEADS * MLA_V], MLA_HEADS, MLA_V).astype(BF16)
    wb = _pad_rows(w_out[MLA_HEADS * MLA_V:], NA_HEADS, NA_HD).astype(BF16)
    wr, br = router_w(l)
    cnt0 = jnp.zeros((1, LANE), F32)
    x1p, h2p, rtp, tbp, cnt_p = _post_mix(_post_even_kernel, xp,
                                     [o_mla_p.reshape(n_p, wide), o_na_p.reshape(n_p, wide)],
                                     [wa, wb], mod_all[l], g2, wr, br, cnt0, grp_p, "post_even")
    x1s, h2s, rts, tbs, cnt_s = _post_mix(_post_even_kernel, xs,
                                     [o_mla_s.reshape(n_s, wide), o_na_s.reshape(n_s, wide)],
                                     [wa, wb], mod_all[l], g2, wr, br, cnt_p, grp_s, "post_even")
    (xp,), (xs,) = moe_layer(l, x1p, x1s, h2p, h2s, rtp, rts, tbp, tbs, cnt_s, False)

    l, o = 1, 0
    lam_init = 0.8 - 0.6 * math.exp(-0.3 * l)
    w_in = b_w_in[o]
    cuts = np.cumsum([0, 256, 256, 512, 512, 16, 512, 512, 512])
    seg = lambda i: w_in[:, cuts[i]:cuts[i + 1]]

    def diff_cols(w):
        w = _to_half_split(w.reshape(d, 2 * DF_HEADS, DF_QK))
        return _pad_heads(w.reshape(d, -1), 2 * DF_HEADS, DF_QK)

    w1p = jnp.concatenate([
        _pad_heads(seg(0), ML_HEADS, ML_DQK), _pad_heads(seg(1), ML_HEADS, ML_DQK), seg(2), seg(3),
        diff_cols(seg(5)), diff_cols(seg(6)), seg(7),
        jnp.pad(seg(4), ((0, 0), (0, LANE - 16)))], axis=1).astype(BF16)
    bg = jnp.pad(b_gate_bias[o].reshape(1, 16), ((0, 0), (0, LANE - 16)))
    g1 = norm_g[l, 0].reshape(1, d)
    g2 = norm_g[l, 1].reshape(1, d)
    tab_d = _rope_tables(ts, DF_QK, 0)
    df_v_of = lambda hd: hd // 2

    mq_p, mk_p, mv_p, mo_p, gt_p, dq_p, dk_p, dv_p = _odd_proj(xp, g1, mod_all[l], w1p, bg, grp_p, None, F32)
    new_diff_k = _from_half_split(dk_p.reshape(bp, 1, lp, DF_HEADS, 2, LANE)[..., :DF_QK])
    new_diff_v = dv_p.reshape(bp, 1, lp, DF_HEADS, DF_V)
    mw = ML_HEADS * LANE
    zc = jnp.zeros((bp, ML_HEADS, LANE, LANE), F32)
    zn = jnp.zeros((bp, ML_HEADS, 1, LANE), F32)
    seq_p = lambda a: a.reshape(bp, lp, -1)
    hf_p, cf_f, nf_f, mf_f = _mlstm_dir(seq_p(mq_p), seq_p(mk_p), seq_p(mv_p), seq_p(gt_p), zc, zn, zn,
                                        direction=0, bb=4)
    hb_p, cf_b, nf_b, mf_b = _mlstm_dir(seq_p(mq_p), seq_p(mk_p), seq_p(mv_p), seq_p(gt_p), zc, zn, zn,
                                        direction=1, bb=4)
    new_mlstm_C = jnp.stack([cf_f, cf_b], axis=1)[..., :ML_DQK][:, None]
    new_mlstm_n = jnp.stack([nf_f, nf_b], axis=1)[:, :, :, 0, :ML_DQK][:, None]
    new_mlstm_m = jnp.stack([mf_f, mf_b], axis=1)[:, :, :, 0, 0][:, None]
    od_p = _flash(seq_p(dq_p), seq_p(dk_p.astype(BF16)), seq_p(dv_p.astype(BF16)),
                  v_of_head=df_v_of, out_dtype=F32)

    mq_s, mk_s, mv_s, mo_s, gt_s, dq_s, dk_s, dv_s = _odd_proj(xs, g1, mod_all[l], w1p, bg, grp_s, tab_d, BF16)
    seq_s = lambda a: a.reshape(bs, ts, -1)
    c0 = jnp.pad(state_mlstm_C[:, o], ((0, 0), (0, 0), (0, 0), (0, 0), (0, LANE - ML_DQK)))
    n0 = jnp.pad(state_mlstm_n[:, o], ((0, 0), (0, 0), (0, 0), (0, LANE - ML_DQK)))[:, :, :, None, :]
    m0 = jnp.broadcast_to(state_mlstm_m[:, o][:, :, :, None, None], (bs, 2, ML_HEADS, 1, LANE))
    hf_s = _mlstm_dir(seq_s(mq_s), seq_s(mk_s), seq_s(mv_s), seq_s(gt_s), c0[:, 0], n0[:, 0], m0[:, 0],
                      direction=0, bb=4)[0]
    hb_s = _mlstm_dir(seq_s(mq_s), seq_s(mk_s), seq_s(mv_s), seq_s(gt_s), c0[:, 1], n0[:, 1], m0[:, 1],
                      direction=1, bb=4)[0]
    dk_cache = _to_half_split(cache_diff_k[:, o])
    dk_cache = jnp.pad(dk_cache, ((0, 0),) * 4 + ((0, LANE - DF_QK),)).astype(BF16).reshape(bs, past, 1024)
    dk_all = jnp.concatenate([dk_cache, seq_s(dk_s)], axis=1)
    dv_all = jnp.concatenate([cache_diff_v[:, o].astype(BF16).reshape(bs, past, 512), seq_s(dv_s)], axis=1)
    od_s = _flash(seq_s(dq_s), dk_all, dv_all, v_of_head=df_v_of, out_dtype=F32)

    w_out = b_w_out[o]
    wa = w_out[:ML_HEADS * ML_DV].astype(BF16)
    wb = w_out[ML_HEADS * ML_DV:].astype(BF16)
    mln = b_ml_norm[o].reshape(1, ML_HEADS * ML_DV)
    sub = b_df_subln[o].reshape(1, DF_V)
    lam_p = jnp.pad(b_df_lambda[o], ((0, 4), (0, LANE - DF_QK)))
    wr, br = router_w(l)
    post_odd = functools.partial(_post_odd_kernel, lam_init=lam_init)
    x1p, h2p, rtp, tbp, cnt_p = _post_mix(post_odd, xp, [hf_p.reshape(n_p, mw), hb_p.reshape(n_p, mw), mo_p,
                                                    od_p.reshape(n_p, 1024)],
                                     [mln, sub, lam_p, wa, wb], mod_all[l], g2, wr, br, cnt0, grp_p, "post_odd")
    x1s, h2s, rts, tbs, cnt_s = _post_mix(post_odd, xs, [hf_s.reshape(n_s, mw), hb_s.reshape(n_s, mw), mo_s,
                                                    od_s.reshape(n_s, 1024)],
                                     [mln, sub, lam_p, wa, wb], mod_all[l], g2, wr, br, cnt_p, grp_s, "post_odd")
    (_, y_p), (_, y_s) = moe_layer(l, x1p, x1s, h2p, h2s, rtp, rts, tbp, tbs, cnt_s, True)

    return (y_p.reshape(bp, lp, d), y_s.reshape(bs, ts, d), new_mla_ckv, new_mla_krope, new_na_k, new_na_v,
            new_mlstm_C, new_mlstm_n, new_mlstm_m, new_diff_k, new_diff_v)
```
